```python
import math
import jax, jax.numpy as jnp
from jax import lax
import numpy as np

D_MODEL = 2048
BATCH = 8
SEQ = 2048
DEPTH = 2

N_A_LAYERS = DEPTH // 2
N_B_LAYERS = DEPTH - N_A_LAYERS
MEM_LEN = 256
MEM_HEADS = 4
MEM_HEAD_DIM = 128
MEM_WIDTH = MEM_HEADS * MEM_HEAD_DIM
MIX_WIDTH = D_MODEL
POOL_WIDTH = MIX_WIDTH - MEM_WIDTH
POOL_WINDOWS = (2, 4, 8, 16)
N_POOL_GROUPS = len(POOL_WINDOWS)
POOL_GROUP = POOL_WIDTH // N_POOL_GROUPS
DIFF_HEAD_DIM = 64
DIFF_V_DIM = 2 * DIFF_HEAD_DIM
DIFF_HEADS = POOL_WIDTH // DIFF_V_DIM
DIFF_QK_WIDTH = DIFF_HEADS * DIFF_HEAD_DIM
DIFF_V_WIDTH = DIFF_HEADS * DIFF_V_DIM
KV_WIDTH = 2 * DIFF_QK_WIDTH + DIFF_V_WIDTH
ROPE_THETA = 500000.0
ROPE_DIM = DIFF_HEAD_DIM // 4
D_FF = 4 * D_MODEL
Q_BLOCK = 128
EPS = 1e-6
NEG_INF = -1e30

kernel_name = 'yoco_pool_diffattn_memory_hybrid'


def rms_norm(x, g):
    xf = x.astype(jnp.float32)
    y = xf * lax.rsqrt(jnp.mean(xf * xf, axis=-1, keepdims=True) + EPS)
    return (y * g.astype(jnp.float32)).astype(x.dtype)


def rope_tables(seq):
    pos = jnp.arange(seq, dtype=jnp.float32)
    inv = ROPE_THETA ** (-(jnp.arange(ROPE_DIM // 2, dtype=jnp.float32) * 2.0) / ROPE_DIM)
    ang = pos[:, None] * inv[None, :]
    return jnp.cos(ang), jnp.sin(ang)


def partial_rope(t, cos, sin):
    half = ROPE_DIM // 2
    c = cos[None, :, None, :].astype(t.dtype)
    s = sin[None, :, None, :].astype(t.dtype)
    r1, r2, rest = t[..., :half], t[..., half:ROPE_DIM], t[..., ROPE_DIM:]
    return jnp.concatenate([r1 * c - r2 * s, r1 * s + r2 * c, rest], axis=-1)


def causal_multiscale_pool(u):
    B, S, _ = u.shape
    groups = u.astype(jnp.float32).reshape(B, S, N_POOL_GROUPS, POOL_GROUP)
    csum = jnp.cumsum(groups, axis=1)
    t = jnp.arange(S)
    outs = []
    for g, w in enumerate(POOL_WINDOWS):
        cg = csum[:, :, g]
        lag = jnp.pad(cg, ((0, 0), (w, 0), (0, 0)))[:, :S]
        count = jnp.minimum(t + 1, w).astype(jnp.float32)[None, :, None]
        outs.append((cg - lag) / count)
    mean = jnp.stack(outs, axis=2)
    return (mean - groups).astype(u.dtype)


def memory_cross_attention(q, mem, mem_norm_g, w_mem_kv, mq_g, mk_g):
    B, S, _ = q.shape
    M = mem.shape[1]
    mkv = rms_norm(mem, mem_norm_g) @ w_mem_kv
    k = mkv[..., :MEM_WIDTH].reshape(B, M, MEM_HEADS, MEM_HEAD_DIM)
    v = mkv[..., MEM_WIDTH:].reshape(B, M, MEM_HEADS, MEM_HEAD_DIM)
    q = rms_norm(q.reshape(B, S, MEM_HEADS, MEM_HEAD_DIM), mq_g)
    k = rms_norm(k, mk_g)
    s = jnp.einsum('bqhd,bkhd->bhqk', q, k).astype(jnp.float32) * (MEM_HEAD_DIM ** -0.5)
    p = jax.nn.softmax(s, axis=-1).astype(v.dtype)
    o = jnp.einsum('bhqk,bkhd->bqhd', p, v)
    return o.reshape(B, S, MEM_WIDTH)


def differential_attention(q1, q2, k1, k2, v, lam):
    B, S, H, dk = q1.shape
    dv = v.shape[-1]
    nb = S // Q_BLOCK
    scale = dk ** -0.5
    key_pos = jnp.arange(S)

    def blockify(t):
        return t.reshape(B, nb, Q_BLOCK, H, dk).transpose(1, 0, 2, 3, 4)

    def one_block(args):
        q1b, q2b, start = args
        q_pos = start + jnp.arange(Q_BLOCK)
        mask = key_pos[None, :] <= q_pos[:, None]

        def probs(qb, k):
            s = jnp.einsum('bqhd,bkhd->bhqk', qb, k).astype(jnp.float32) * scale
            return jax.nn.softmax(jnp.where(mask, s, NEG_INF), axis=-1)

        a = probs(q1b, k1) - lam * probs(q2b, k2)
        return jnp.einsum('bhqk,bkhd->bqhd', a.astype(v.dtype), v)

    starts = jnp.arange(nb) * Q_BLOCK
    out = lax.map(one_block, (blockify(q1), blockify(q2), starts))
    return out.transpose(1, 0, 2, 3, 4).reshape(B, S, H, dv)


def shared_kv(x, kv_norm, w_kv, k_norm, cos, sin):
    B, S, _ = x.shape
    kv = rms_norm(x, kv_norm) @ w_kv
    k1 = kv[..., :DIFF_QK_WIDTH].reshape(B, S, DIFF_HEADS, DIFF_HEAD_DIM)
    k2 = kv[..., DIFF_QK_WIDTH:2 * DIFF_QK_WIDTH].reshape(B, S, DIFF_HEADS, DIFF_HEAD_DIM)
    v = kv[..., 2 * DIFF_QK_WIDTH:].reshape(B, S, DIFF_HEADS, DIFF_V_DIM)
    k1 = partial_rope(rms_norm(k1, k_norm), cos, sin)
    k2 = partial_rope(rms_norm(k2, k_norm), cos, sin)
    return k1, k2, v


def squared_relu_mlp(h, w1, w2):
    z = jax.nn.relu(h @ w1)
    return (z * z) @ w2


def setup_inputs(seed: int = 0) -> dict:
    key = jax.random.key(seed)
    ks = jax.random.split(key, 20)
    f32 = jnp.float32

    def nrm(k, shape, scale):
        return jax.random.normal(k, shape, f32) * scale

    def gain(k, shape, noise=0.02):
        return 1.0 + noise * jax.random.normal(k, shape, f32)

    return {
        'x': nrm(ks[0], (BATCH, SEQ, D_MODEL), 1.0),
        'mem': nrm(ks[1], (BATCH, MEM_LEN, D_MODEL), 1.0),
        'mix_norm': gain(ks[2], (DEPTH, D_MODEL)),
        'w_in': nrm(ks[3], (DEPTH, D_MODEL, MIX_WIDTH), D_MODEL ** -0.5),
        'w_out': nrm(ks[4], (DEPTH, MIX_WIDTH, D_MODEL), 0.5 * MIX_WIDTH ** -0.5),
        'mem_norm': gain(ks[5], (DEPTH, D_MODEL)),
        'w_mem_kv': nrm(ks[6], (DEPTH, D_MODEL, 2 * MEM_WIDTH), D_MODEL ** -0.5),
        'mem_q_norm': gain(ks[7], (DEPTH, MEM_HEAD_DIM)),
        'mem_k_norm': gain(ks[8], (DEPTH, MEM_HEAD_DIM)),
        'ffn_norm': gain(ks[9], (DEPTH, D_MODEL)),
        'w_ff1': nrm(ks[10], (DEPTH, D_MODEL, D_FF), D_MODEL ** -0.5),
        'w_ff2': nrm(ks[11], (DEPTH, D_FF, D_MODEL), 0.5 * D_FF ** -0.5),
        'pool_w': nrm(ks[12], (N_A_LAYERS, N_POOL_GROUPS, POOL_GROUP, POOL_GROUP), POOL_GROUP ** -0.5),
        'pool_scale': gain(ks[13], (N_A_LAYERS, POOL_WIDTH), 0.1),
        'kv_norm': gain(ks[14], (D_MODEL,)),
        'w_kv': nrm(ks[15], (D_MODEL, KV_WIDTH), D_MODEL ** -0.5),
        'k_norm': gain(ks[16], (DIFF_HEAD_DIM,)),
        'q_norm': gain(ks[17], (N_B_LAYERS, DIFF_HEAD_DIM)),
        'diff_lambda': nrm(ks[18], (N_B_LAYERS, 4, DIFF_HEAD_DIM), 0.1),
        'subln_norm': gain(ks[19], (N_B_LAYERS, DIFF_V_DIM)),
    }


def reference(x, mem, mix_norm, w_in, w_out, mem_norm, w_mem_kv, mem_q_norm, mem_k_norm,
              ffn_norm, w_ff1, w_ff2, pool_w, pool_scale, kv_norm, w_kv, k_norm, q_norm,
              diff_lambda, subln_norm):
    B, S, _ = x.shape
    cos, sin = rope_tables(S)
    sk1 = sk2 = sv = None
    for i in range(DEPTH):
        if i == N_A_LAYERS:
            sk1, sk2, sv = shared_kv(x, kv_norm, w_kv, k_norm, cos, sin)
        u = rms_norm(x, mix_norm[i]) @ w_in[i]
        if i < N_A_LAYERS:
            a = i
            pooled = causal_multiscale_pool(u[..., :POOL_WIDTH])
            token_out = jnp.einsum('bsgc,gcd->bsgd', pooled, pool_w[a]).reshape(B, S, POOL_WIDTH) * pool_scale[a]
            mem_q = u[..., POOL_WIDTH:]
        else:
            b = i - N_A_LAYERS
            lam_init = 0.8 - 0.6 * math.exp(-0.3 * i)
            q1 = u[..., :DIFF_QK_WIDTH].reshape(B, S, DIFF_HEADS, DIFF_HEAD_DIM)
            q2 = u[..., DIFF_QK_WIDTH:2 * DIFF_QK_WIDTH].reshape(B, S, DIFF_HEADS, DIFF_HEAD_DIM)
            q1 = partial_rope(rms_norm(q1, q_norm[b]), cos, sin)
            q2 = partial_rope(rms_norm(q2, q_norm[b]), cos, sin)
            lq = diff_lambda[b].astype(jnp.float32)
            lam = jnp.exp(jnp.sum(lq[0] * lq[1])) - jnp.exp(jnp.sum(lq[2] * lq[3])) + lam_init
            o = differential_attention(q1, q2, sk1, sk2, sv, lam)
            token_out = (rms_norm(o, subln_norm[b]) * (1.0 - lam_init)).reshape(B, S, DIFF_V_WIDTH)
            mem_q = u[..., 2 * DIFF_QK_WIDTH:]
        mem_out = memory_cross_attention(mem_q, mem, mem_norm[i], w_mem_kv[i], mem_q_norm[i], mem_k_norm[i])
        x = x + jnp.concatenate([token_out, mem_out], axis=-1) @ w_out[i]
        x = x + squared_relu_mlp(rms_norm(x, ffn_norm[i]), w_ff1[i], w_ff2[i])
    return x
```

```python
import functools
import math

import jax
import jax.numpy as jnp
from jax import lax
from jax.experimental import pallas as pl
from jax.experimental.pallas import tpu as pltpu

D_MODEL = 2048
MEM_LEN = 256
MEM_HEADS = 4
MEM_HEAD_DIM = 128
MEM_WIDTH = MEM_HEADS * MEM_HEAD_DIM
POOL_WIDTH = D_MODEL - MEM_WIDTH
POOL_WINDOWS = (2, 4, 8, 16)
POOL_GROUP = POOL_WIDTH // len(POOL_WINDOWS)
POOL_HALO = 16
DIFF_HEAD_DIM = 64
DIFF_V_DIM = 128
DIFF_HEADS = POOL_WIDTH // DIFF_V_DIM
DIFF_QK_WIDTH = DIFF_HEADS * DIFF_HEAD_DIM
DIFF_V_WIDTH = DIFF_HEADS * DIFF_V_DIM
KV_WIDTH = 2 * DIFF_QK_WIDTH + DIFF_V_WIDTH
ROPE_THETA = 500000.0
ROPE_DIM = DIFF_HEAD_DIM // 4
D_FF = 4 * D_MODEL
EPS = 1e-6
NEG_INF = -1e30

LANES = 128
MXU_DIM = 256
VMEM_LIMIT_BYTES = 56 * 1024 * 1024

BF16 = jnp.bfloat16
F32 = jnp.float32


def _resident(shape, index_map):
    return pl.BlockSpec(shape, index_map, pipeline_mode=pl.Buffered(1))


def _rstd(x):
    return lax.rsqrt(jnp.mean(x * x, axis=-1, keepdims=True) + EPS)


def _dot(a, b):
    return jnp.dot(a, b, preferred_element_type=F32)


def _dot_nt(a, b):
    return lax.dot_general(a, b, (((1,), (1,)), ((), ())), preferred_element_type=F32)


def _memkv_kernel(mem_ref, g_ref, w_ref, kg_ref, k_ref, v_ref):
    x = mem_ref[...]
    xn = (x * _rstd(x) * g_ref[0]).astype(BF16)
    kv = _dot(xn, w_ref[0])
    for h in range(MEM_HEADS):
        hs = slice(h * MEM_HEAD_DIM, (h + 1) * MEM_HEAD_DIM)
        kh = kv[:, hs]
        k_ref[0, :, hs] = (kh * _rstd(kh) * kg_ref[0]).astype(BF16)
    v_ref[0] = kv[:, MEM_WIDTH:].astype(BF16)


def _memkv(mem2d, mem_norm, w_mem_kv_bf, mem_k_norm, batch):
    depth = mem_norm.shape[0]
    out_sd = jax.ShapeDtypeStruct((depth, batch * MEM_LEN, MEM_WIDTH), BF16)
    return pl.pallas_call(
        _memkv_kernel,
        grid=(depth, batch),
        in_specs=[
            pl.BlockSpec((MEM_LEN, D_MODEL), lambda l, b: (b, 0)),
            pl.BlockSpec((1, 1, D_MODEL), lambda l, b: (l, 0, 0)),
            pl.BlockSpec((1, D_MODEL, 2 * MEM_WIDTH), lambda l, b: (l, 0, 0)),
            pl.BlockSpec((1, 1, MEM_HEAD_DIM), lambda l, b: (l, 0, 0)),
        ],
        out_specs=[
            pl.BlockSpec((1, MEM_LEN, MEM_WIDTH), lambda l, b: (l, b, 0)),
            pl.BlockSpec((1, MEM_LEN, MEM_WIDTH), lambda l, b: (l, b, 0)),
        ],
        out_shape=[out_sd, out_sd],
        compiler_params=pltpu.CompilerParams(
            dimension_semantics=("arbitrary", "arbitrary"),
            vmem_limit_bytes=VMEM_LIMIT_BYTES),
        name="mem_kv",
    )(mem2d, mem_norm.reshape(depth, 1, D_MODEL), w_mem_kv_bf,
      mem_k_norm.reshape(depth, 1, MEM_HEAD_DIM))


def _mem_attention(q, qg, k, v, write):
    scale = MEM_HEAD_DIM ** -0.5
    for h in range(MEM_HEADS):
        hs = slice(h * MEM_HEAD_DIM, (h + 1) * MEM_HEAD_DIM)
        qh = q[:, hs]
        qn = (qh * _rstd(qh) * qg).astype(BF16)
        s = _dot_nt(qn, k[:, hs]) * scale
        p = jnp.exp(s - jnp.max(s, axis=-1, keepdims=True))
        l = jnp.sum(p, axis=-1, keepdims=True)
        write(h, _dot(p.astype(BF16), v[:, hs]) / l)


def _pool_bands():
    r = jnp.arange(MXU_DIM)[:, None]
    c = jnp.arange(MXU_DIM)[None, :]
    rh = jnp.arange(POOL_HALO)[:, None]
    ch = jnp.arange(POOL_HALO)[None, :]
    band = jnp.stack([((r - c >= 0) & (r - c < w)) for w in POOL_WINDOWS])
    bandh = jnp.stack([(rh + POOL_HALO - ch < w) for w in POOL_WINDOWS])
    return band.astype(BF16), bandh.astype(BF16)


def _layer_a_kernel(x_ref, xh_ref, g_ref, win_ref, band_ref, bandh_ref, poolw_ref, pscale_ref,
                    mk_ref, mv_ref, mqg_ref, wout_ref, o_ref, cat_ref, *, tm):
    i = pl.program_id(1)
    x = x_ref[...]
    xn = (x * _rstd(x) * g_ref[...]).astype(BF16)
    u = _dot(xn, win_ref[...])

    xh = xh_ref[...]
    xhn = (xh * _rstd(xh) * g_ref[...]).astype(BF16)
    uh = _dot(xhn, win_ref[:, :POOL_WIDTH])
    uh = jnp.where(i == 0, 0.0, uh).astype(BF16)

    ub = u[:, :POOL_WIDTH].astype(BF16)
    row = lax.broadcasted_iota(jnp.int32, (MXU_DIM, 1), 0)
    for g, w in enumerate(POOL_WINDOWS):
        cs = slice(g * POOL_GROUP, (g + 1) * POOL_GROUP)
        for sb in range(tm // MXU_DIM):
            r0 = sb * MXU_DIM
            rs = slice(r0, r0 + MXU_DIM)
            main = _dot(band_ref[g], ub[rs, cs])
            halo = uh[:, cs] if sb == 0 else ub[r0 - POOL_HALO:r0, cs]
            top = _dot(bandh_ref[g], halo)
            wsum = jnp.concatenate([main[:POOL_HALO] + top, main[POOL_HALO:]], axis=0)
            t = i * tm + r0 + row
            count = jnp.minimum(t + 1, w).astype(F32)
            pooled = wsum / count - u[rs, cs]
            tok = _dot(pooled.astype(BF16), poolw_ref[g]) * pscale_ref[:, cs]
            cat_ref[rs, cs] = tok.astype(BF16)

    def write(h, o):
        c0 = POOL_WIDTH + h * MEM_HEAD_DIM
        cat_ref[:, c0:c0 + MEM_HEAD_DIM] = o.astype(BF16)

    _mem_attention(u[:, POOL_WIDTH:], mqg_ref[...], mk_ref[0], mv_ref[0], write)
    o_ref[...] = x + _dot(cat_ref[...], wout_ref[...])


def _layer_a(x2d, g, win_bf, poolw_bf, pscale, mk, mv, mqg, wout_bf, batch, seq, tm):
    nt = seq // tm
    band, bandh = _pool_bands()
    ngrp = len(POOL_WINDOWS)
    const2 = lambda b, i: (0, 0)
    const3 = lambda b, i: (0, 0, 0)
    halo_blocks = tm // POOL_HALO
    return pl.pallas_call(
        functools.partial(_layer_a_kernel, tm=tm),
        grid=(batch, nt),
        in_specs=[
            pl.BlockSpec((tm, D_MODEL), lambda b, i: (b * nt + i, 0)),
            pl.BlockSpec((POOL_HALO, D_MODEL),
                         lambda b, i: (jnp.maximum((b * nt + i) * halo_blocks - 1, 0), 0)),
            _resident((1, D_MODEL), const2),
            _resident((D_MODEL, D_MODEL), const2),
            _resident((ngrp, MXU_DIM, MXU_DIM), const3),
            _resident((ngrp, POOL_HALO, POOL_HALO), const3),
            _resident((ngrp, POOL_GROUP, POOL_GROUP), const3),
            _resident((1, POOL_WIDTH), const2),
            pl.BlockSpec((1, MEM_LEN, MEM_WIDTH), lambda b, i: (0, b, 0)),
            pl.BlockSpec((1, MEM_LEN, MEM_WIDTH), lambda b, i: (0, b, 0)),
            _resident((1, MEM_HEAD_DIM), const2),
            _resident((D_MODEL, D_MODEL), const2),
        ],
        out_specs=pl.BlockSpec((tm, D_MODEL), lambda b, i: (b * nt + i, 0)),
        out_shape=jax.ShapeDtypeStruct(x2d.shape, F32),
        scratch_shapes=[pltpu.VMEM((tm, D_MODEL), BF16)],
        compiler_params=pltpu.CompilerParams(
            dimension_semantics=("arbitrary", "arbitrary"),
            vmem_limit_bytes=VMEM_LIMIT_BYTES),
        name="layer_a_mixer",
    )(x2d, x2d, g, win_bf, band, bandh, poolw_bf, pscale, mk, mv, mqg, wout_bf)


def _ffn_kernel(x_ref, g_ref, w1_ref, w2_ref, o_ref, xn_ref):
    k = pl.program_id(1)

    @pl.when(k == 0)
    def _():
        x = x_ref[...]
        xn_ref[...] = (x * _rstd(x) * g_ref[...]).astype(BF16)
        o_ref[...] = x

    z = jnp.maximum(_dot(xn_ref[...], w1_ref[...]), 0.0)
    o_ref[...] += _dot((z * z).astype(BF16), w2_ref[...])


def _ffn(x2d, g, w1_bf, w2_bf, tm, tf):
    m = x2d.shape[0]
    return pl.pallas_call(
        _ffn_kernel,
        grid=(m // tm, D_FF // tf),
        in_specs=[
            pl.BlockSpec((tm, D_MODEL), lambda i, k: (i, 0)),
            _resident((1, D_MODEL), lambda i, k: (0, 0)),
            pl.BlockSpec((D_MODEL, tf), lambda i, k: (0, k)),
            pl.BlockSpec((tf, D_MODEL), lambda i, k: (k, 0)),
        ],
        out_specs=pl.BlockSpec((tm, D_MODEL), lambda i, k: (i, 0)),
        out_shape=jax.ShapeDtypeStruct(x2d.shape, F32),
        scratch_shapes=[pltpu.VMEM((tm, D_MODEL), BF16)],
        compiler_params=pltpu.CompilerParams(
            dimension_semantics=("arbitrary", "arbitrary"),
            vmem_limit_bytes=VMEM_LIMIT_BYTES),
        name="ffn",
    )(x2d, g, w1_bf, w2_bf)


def _rope_coeffs(seq):
    half = ROPE_DIM // 2
    pos = jnp.arange(seq, dtype=F32)
    inv = ROPE_THETA ** (-(jnp.arange(half, dtype=F32) * 2.0) / ROPE_DIM)
    ang = pos[:, None] * inv[None, :]
    cos, sin = jnp.cos(ang), jnp.sin(ang)
    ones = jnp.ones((seq, DIFF_HEAD_DIM - ROPE_DIM), F32)
    zeros = jnp.zeros((seq, DIFF_HEAD_DIM - ROPE_DIM), F32)
    zh = jnp.zeros((seq, half), F32)
    c0 = jnp.concatenate([cos, cos, ones], axis=-1)
    c_up = jnp.concatenate([-sin, zh, zeros], axis=-1)
    c_dn = jnp.concatenate([zh, sin, zeros], axis=-1)
    rep = LANES // DIFF_HEAD_DIM
    return tuple(jnp.tile(c, (1, rep)) for c in (c0, c_up, c_dn))


def _head_norm_rope(t, gain, bd, c0, c_up, c_dn):
    ss = _dot((t * t).astype(BF16), bd)
    tn = t * lax.rsqrt(ss * (1.0 / DIFF_HEAD_DIM) + EPS) * gain
    outs = []
    half = ROPE_DIM // 2
    for p in range(MXU_DIM // LANES):
        tp = tn[:, p * LANES:(p + 1) * LANES]
        up = pltpu.roll(tp, LANES - half, axis=1)
        dn = pltpu.roll(tp, half, axis=1)
        outs.append(tp * c0 + up * c_up + dn * c_dn)
    return jnp.concatenate(outs, axis=-1)


def _layer_b_proj_kernel(x_ref, gmix_ref, gkv_ref, win_ref, wkv_ref, bd_ref, qg_ref, kg_ref,
                         c0_ref, cup_ref, cdn_ref, mk_ref, mv_ref, mqg_ref,
                         q1_ref, q2_ref, k1_ref, k2_ref, v_ref, mo_ref):
    x = x_ref[...]
    xr = x * _rstd(x)
    u = _dot((xr * gmix_ref[...]).astype(BF16), win_ref[...])
    kv = _dot((xr * gkv_ref[...]).astype(BF16), wkv_ref[...])
    bd = bd_ref[...]
    c0, cup, cdn = c0_ref[...], cup_ref[...], cdn_ref[...]
    qg = qg_ref[...] * (DIFF_HEAD_DIM ** -0.5)
    kg = kg_ref[...]
    for c in range(DIFF_QK_WIDTH // MXU_DIM):
        cs = slice(c * MXU_DIM, (c + 1) * MXU_DIM)
        q1_ref[:, cs] = _head_norm_rope(u[:, cs], qg, bd, c0, cup, cdn).astype(BF16)
        q2_ref[:, cs] = _head_norm_rope(
            u[:, DIFF_QK_WIDTH + c * MXU_DIM:DIFF_QK_WIDTH + (c + 1) * MXU_DIM],
            qg, bd, c0, cup, cdn).astype(BF16)
        k1_ref[:, cs] = _head_norm_rope(kv[:, cs], kg, bd, c0, cup, cdn).astype(BF16)
        k2_ref[:, cs] = _head_norm_rope(
            kv[:, DIFF_QK_WIDTH + c * MXU_DIM:DIFF_QK_WIDTH + (c + 1) * MXU_DIM],
            kg, bd, c0, cup, cdn).astype(BF16)
    v_ref[...] = kv[:, 2 * DIFF_QK_WIDTH:].astype(BF16)

    def write(h, o):
        mo_ref[:, h * MEM_HEAD_DIM:(h + 1) * MEM_HEAD_DIM] = o.astype(BF16)

    _mem_attention(u[:, 2 * DIFF_QK_WIDTH:], mqg_ref[...], mk_ref[0], mv_ref[0], write)


def _layer_b_proj(x2d, gmix, gkv, win_bf, wkv_bf, q_norm, k_norm, mk, mv, mqg, batch, seq, tm):
    nt = seq // tm
    m = x2d.shape[0]
    c0, cup, cdn = _rope_coeffs(seq)
    lane_head = jnp.arange(MXU_DIM) // DIFF_HEAD_DIM
    bd = (lane_head[:, None] == lane_head[None, :]).astype(BF16)
    qg = jnp.tile(q_norm.reshape(1, DIFF_HEAD_DIM), (1, MXU_DIM // DIFF_HEAD_DIM))
    kg = jnp.tile(k_norm.reshape(1, DIFF_HEAD_DIM), (1, MXU_DIM // DIFF_HEAD_DIM))
    const2 = lambda b, i: (0, 0)
    rows = lambda b, i: (b * nt + i, 0)
    qk_sd = jax.ShapeDtypeStruct((m, DIFF_QK_WIDTH), BF16)
    return pl.pallas_call(
        _layer_b_proj_kernel,
        grid=(batch, nt),
        in_specs=[
            pl.BlockSpec((tm, D_MODEL), rows),
            _resident((1, D_MODEL), const2),
            _resident((1, D_MODEL), const2),
            _resident((D_MODEL, D_MODEL), const2),
            _resident((D_MODEL, KV_WIDTH), const2),
            _resident((MXU_DIM, MXU_DIM), const2),
            _resident((1, MXU_DIM), const2),
            _resident((1, MXU_DIM), const2),
            pl.BlockSpec((tm, LANES), lambda b, i: (i, 0)),
            pl.BlockSpec((tm, LANES), lambda b, i: (i, 0)),
            pl.BlockSpec((tm, LANES), lambda b, i: (i, 0)),
            pl.BlockSpec((1, MEM_LEN, MEM_WIDTH), lambda b, i: (1, b, 0)),
            pl.BlockSpec((1, MEM_LEN, MEM_WIDTH), lambda b, i: (1, b, 0)),
            _resident((1, MEM_HEAD_DIM), const2),
        ],
        out_specs=[
            pl.BlockSpec((tm, DIFF_QK_WIDTH), rows),
            pl.BlockSpec((tm, DIFF_QK_WIDTH), rows),
            pl.BlockSpec((tm, DIFF_QK_WIDTH), rows),
            pl.BlockSpec((tm, DIFF_QK_WIDTH), rows),
            pl.BlockSpec((tm, DIFF_V_WIDTH), rows),
            pl.BlockSpec((tm, MEM_WIDTH), rows),
        ],
        out_shape=[qk_sd, qk_sd, qk_sd, qk_sd,
                   jax.ShapeDtypeStruct((m, DIFF_V_WIDTH), BF16),
                   jax.ShapeDtypeStruct((m, MEM_WIDTH), BF16)],
        compiler_params=pltpu.CompilerParams(
            dimension_semantics=("arbitrary", "arbitrary"),
            vmem_limit_bytes=VMEM_LIMIT_BYTES),
        name="layer_b_proj",
    )(x2d, gmix, gkv, win_bf, wkv_bf, bd, qg, kg, c0, cup, cdn, mk, mv, mqg)


def _diff_attn_kernel(q1_ref, q2_ref, k1_ref, k2_ref, v_ref, lam_ref, sg_ref, o_ref,
                      m_ref, l_ref, acc_ref, *, tq, lam_init):
    i = pl.program_id(2)
    lane = lax.broadcasted_iota(jnp.int32, (1, LANES), 1)
    row = lax.broadcasted_iota(jnp.int32, (tq, tq), 0)
    col = lax.broadcasted_iota(jnp.int32, (tq, tq), 1)
    causal = col <= row
    heads_per_step = LANES // DIFF_HEAD_DIM
    combos = [(hh, mp) for hh in range(heads_per_step) for mp in range(2)]

    qms = []
    for hh, mp in combos:
        q = (q1_ref, q2_ref)[mp][...]
        in_head = (lane >= hh * DIFF_HEAD_DIM) & (lane < (hh + 1) * DIFF_HEAD_DIM)
        qms.append(jnp.where(in_head, q, jnp.zeros_like(q)))

    def block(j, masked):
        r0 = pl.multiple_of(j * tq, tq)
        for c, (hh, mp) in enumerate(combos):
            kblk = (k1_ref, k2_ref)[mp][pl.ds(r0, tq), :]
            vblk = v_ref[pl.ds(r0, tq), hh * DIFF_V_DIM:(hh + 1) * DIFF_V_DIM]
            s = _dot_nt(qms[c], kblk)
            if masked:
                s = jnp.where(causal, s, NEG_INF)
                m_new = jnp.max(s, axis=-1, keepdims=True)
                p = jnp.exp(s - m_new)
                l_ref[c] = jnp.sum(p, axis=-1, keepdims=True)
                acc_ref[c] = _dot(p.astype(BF16), vblk)
            else:
                m_old = m_ref[c]
                m_new = jnp.maximum(m_old, jnp.max(s, axis=-1, keepdims=True))
                alpha = jnp.exp(m_old - m_new)
                p = jnp.exp(s - m_new)
                l_ref[c] = alpha * l_ref[c] + jnp.sum(p, axis=-1, keepdims=True)
                acc_ref[c] = alpha * acc_ref[c] + _dot(p.astype(BF16), vblk)
            m_ref[c] = m_new

    block(i, True)

    def body(j, carry):
        block(j, False)
        return carry

    lax.fori_loop(0, i, body, 0)

    lq = lam_ref[...]
    lam = (jnp.exp(jnp.sum(lq[0:1] * lq[1:2], axis=-1, keepdims=True))
           - jnp.exp(jnp.sum(lq[2:3] * lq[3:4], axis=-1, keepdims=True)) + lam_init)
    for hh in range(heads_per_step):
        c1, c2 = 2 * hh, 2 * hh + 1
        o = acc_ref[c1] / l_ref[c1] - lam * (acc_ref[c2] / l_ref[c2])
        o = o * _rstd(o) * sg_ref[...] * (1.0 - lam_init)
        o_ref[:, hh * DIFF_V_DIM:(hh + 1) * DIFF_V_DIM] = o.astype(BF16)


def _diff_attention(q1, q2, k1, k2, v, diff_lambda, subln, batch, seq, tq, lam_init):
    nq = seq // tq
    m = q1.shape[0]
    npairs = DIFF_QK_WIDTH // LANES
    vw = 2 * DIFF_V_DIM
    ncombo = 4
    qspec = pl.BlockSpec((tq, LANES), lambda b, hp, i: (b * nq + i, hp))
    kspec = pl.BlockSpec((seq, LANES), lambda b, hp, i: (b, hp))
    return pl.pallas_call(
        functools.partial(_diff_attn_kernel, tq=tq, lam_init=lam_init),
        grid=(batch, npairs, nq),
        in_specs=[
            qspec, qspec, kspec, kspec,
            pl.BlockSpec((seq, vw), lambda b, hp, i: (b, hp)),
            _resident((4, DIFF_HEAD_DIM), lambda b, hp, i: (0, 0)),
            _resident((1, DIFF_V_DIM), lambda b, hp, i: (0, 0)),
        ],
        out_specs=pl.BlockSpec((tq, vw), lambda b, hp, i: (b * nq + i, hp)),
        out_shape=jax.ShapeDtypeStruct((m, DIFF_V_WIDTH), BF16),
        scratch_shapes=[
            pltpu.VMEM((ncombo, tq, 1), F32),
            pltpu.VMEM((ncombo, tq, 1), F32),
            pltpu.VMEM((ncombo, tq, DIFF_V_DIM), F32),
        ],
        compiler_params=pltpu.CompilerParams(
            dimension_semantics=("arbitrary", "arbitrary", "arbitrary"),
            vmem_limit_bytes=VMEM_LIMIT_BYTES),
        name="diff_attention",
    )(q1, q2, k1, k2, v, diff_lambda, subln)


def _out_proj_kernel(tok_ref, mo_ref, x_ref, w_ref, o_ref):
    o_ref[...] = (x_ref[...] + _dot(tok_ref[...], w_ref[:DIFF_V_WIDTH, :])
                  + _dot(mo_ref[...], w_ref[DIFF_V_WIDTH:, :]))


def _out_proj(tok, mo, x2d, wout_bf, tm):
    m = x2d.shape[0]
    rows = lambda i: (i, 0)
    return pl.pallas_call(
        _out_proj_kernel,
        grid=(m // tm,),
        in_specs=[
            pl.BlockSpec((tm, DIFF_V_WIDTH), rows),
            pl.BlockSpec((tm, MEM_WIDTH), rows),
            pl.BlockSpec((tm, D_MODEL), rows),
            _resident((D_MODEL, D_MODEL), lambda i: (0, 0)),
        ],
        out_specs=pl.BlockSpec((tm, D_MODEL), rows),
        out_shape=jax.ShapeDtypeStruct(x2d.shape, F32),
        compiler_params=pltpu.CompilerParams(
            dimension_semantics=("arbitrary",),
            vmem_limit_bytes=VMEM_LIMIT_BYTES),
        name="out_proj",
    )(tok, mo, x2d, wout_bf)


def kernel(x, mem, mix_norm, w_in, w_out, mem_norm, w_mem_kv, mem_q_norm, mem_k_norm,
           ffn_norm, w_ff1, w_ff2, pool_w, pool_scale, kv_norm, w_kv, k_norm, q_norm,
           diff_lambda, subln_norm):
    batch, seq, _ = x.shape
    m = batch * seq
    x2d = x.reshape(m, D_MODEL)
    mem2d = mem.reshape(batch * MEM_LEN, D_MODEL)

    w_in_bf = w_in.astype(BF16)
    w_out_bf = w_out.astype(BF16)
    w_ff1_bf = w_ff1.astype(BF16)
    w_ff2_bf = w_ff2.astype(BF16)
    w_kv_bf = w_kv.astype(BF16)

    mk, mv = _memkv(mem2d, mem_norm, w_mem_kv.astype(BF16), mem_k_norm, batch)

    x2d = _layer_a(x2d, mix_norm[0:1], w_in_bf[0], pool_w[0].astype(BF16), pool_scale[0:1],
                   mk, mv, mem_q_norm[0:1], w_out_bf[0], batch, seq, tm=512)
    x2d = _ffn(x2d, ffn_norm[0:1], w_ff1_bf[0], w_ff2_bf[0], tm=512, tf=1024)

    lam_init = 0.8 - 0.6 * math.exp(-0.3 * 1)
    q1, q2, k1, k2, v, mo = _layer_b_proj(
        x2d, mix_norm[1:2], kv_norm.reshape(1, D_MODEL), w_in_bf[1], w_kv_bf,
        q_norm[0], k_norm, mk, mv, mem_q_norm[1:2], batch, seq, tm=256)
    tok = _diff_attention(q1, q2, k1, k2, v, diff_lambda[0], subln_norm[0:1],
                          batch, seq, tq=256, lam_init=lam_init)
    x2d = _out_proj(tok, mo, x2d, w_out_bf[1], tm=512)
    x2d = _ffn(x2d, ffn_norm[1:2], w_ff1_bf[1], w_ff2_bf[1], tm=512, tf=1024)
    return x2d.reshape(batch, seq, D_MODEL)
```

```python
import functools
import math

import jax
import jax.numpy as jnp
from jax import lax
from jax.experimental import pallas as pl
from jax.experimental.pallas import tpu as pltpu

D_MODEL = 2048
MEM_LEN = 256
MEM_HEADS = 4
MEM_HEAD_DIM = 128
MEM_WIDTH = MEM_HEADS * MEM_HEAD_DIM
POOL_WIDTH = D_MODEL - MEM_WIDTH
POOL_WINDOWS = (2, 4, 8, 16)
POOL_GROUP = POOL_WIDTH // len(POOL_WINDOWS)
POOL_HALO = 16
DIFF_HEAD_DIM = 64
DIFF_V_DIM = 128
DIFF_HEADS = POOL_WIDTH // DIFF_V_DIM
DIFF_QK_WIDTH = DIFF_HEADS * DIFF_HEAD_DIM
DIFF_V_WIDTH = DIFF_HEADS * DIFF_V_DIM
VT_ROWS = DIFF_V_DIM + 16
KV_WIDTH = 2 * DIFF_QK_WIDTH + DIFF_V_WIDTH
ROPE_THETA = 500000.0
ROPE_DIM = DIFF_HEAD_DIM // 4
D_FF = 4 * D_MODEL
EPS = 1e-6
NEG_INF = -1e30

LANES = 128
MXU_DIM = 256
VMEM_LIMIT_BYTES = 56 * 1024 * 1024

BF16 = jnp.bfloat16
F32 = jnp.float32


def _resident(shape, index_map):
    return pl.BlockSpec(shape, index_map, pipeline_mode=pl.Buffered(1))


def _rstd(x):
    return lax.rsqrt(jnp.mean(x * x, axis=-1, keepdims=True) + EPS)


def _dot(a, b):
    return jnp.dot(a, b, preferred_element_type=F32)


def _dot_nt(a, b):
    return lax.dot_general(a, b, (((1,), (1,)), ((), ())), preferred_element_type=F32)


def _memkv_kernel(mem_ref, g_ref, w_ref, kg_ref, k_ref, v_ref):
    x = mem_ref[...]
    xn = (x * _rstd(x) * g_ref[0]).astype(BF16)
    kv = _dot(xn, w_ref[0])
    for h in range(MEM_HEADS):
        hs = slice(h * MEM_HEAD_DIM, (h + 1) * MEM_HEAD_DIM)
        kh = kv[:, hs]
        k_ref[0, :, hs] = (kh * _rstd(kh) * kg_ref[0]).astype(BF16)
    v_ref[0] = kv[:, MEM_WIDTH:].astype(BF16)


def _memkv(mem2d, mem_norm, w_mem_kv_bf, mem_k_norm, batch):
    depth = mem_norm.shape[0]
    out_sd = jax.ShapeDtypeStruct((depth, batch * MEM_LEN, MEM_WIDTH), BF16)
    return pl.pallas_call(
        _memkv_kernel,
        grid=(depth, batch),
        in_specs=[
            pl.BlockSpec((MEM_LEN, D_MODEL), lambda l, b: (b, 0)),
            pl.BlockSpec((1, 1, D_MODEL), lambda l, b: (l, 0, 0)),
            pl.BlockSpec((1, D_MODEL, 2 * MEM_WIDTH), lambda l, b: (l, 0, 0)),
            pl.BlockSpec((1, 1, MEM_HEAD_DIM), lambda l, b: (l, 0, 0)),
        ],
        out_specs=[
            pl.BlockSpec((1, MEM_LEN, MEM_WIDTH), lambda l, b: (l, b, 0)),
            pl.BlockSpec((1, MEM_LEN, MEM_WIDTH), lambda l, b: (l, b, 0)),
        ],
        out_shape=[out_sd, out_sd],
        compiler_params=pltpu.CompilerParams(
            dimension_semantics=("arbitrary", "arbitrary"),
            vmem_limit_bytes=VMEM_LIMIT_BYTES),
        name="mem_kv",
    )(mem2d, mem_norm.reshape(depth, 1, D_MODEL), w_mem_kv_bf,
      mem_k_norm.reshape(depth, 1, MEM_HEAD_DIM))


def _mem_attention(q, qg, k, v, write):
    scale = MEM_HEAD_DIM ** -0.5
    for h in range(MEM_HEADS):
        hs = slice(h * MEM_HEAD_DIM, (h + 1) * MEM_HEAD_DIM)
        qh = q[:, hs]
        qn = (qh * _rstd(qh) * qg).astype(BF16)
        s = _dot_nt(qn, k[:, hs]) * scale
        p = jnp.exp(s - jnp.max(s, axis=-1, keepdims=True))
        l = jnp.sum(p, axis=-1, keepdims=True)
        write(h, _dot(p.astype(BF16), v[:, hs]) / l)


def _pool_bands():
    r = jnp.arange(MXU_DIM)[:, None]
    c = jnp.arange(MXU_DIM)[None, :]
    rh = jnp.arange(POOL_HALO)[:, None]
    ch = jnp.arange(POOL_HALO)[None, :]
    band = jnp.stack([((r - c >= 0) & (r - c < w)) for w in POOL_WINDOWS])
    bandh = jnp.stack([(rh + POOL_HALO - ch < w) for w in POOL_WINDOWS])
    return band.astype(BF16), bandh.astype(BF16)


def _layer_a_kernel(x_ref, xh_ref, g_ref, win_ref, band_ref, bandh_ref, poolw_ref, pscale_ref,
                    mk_ref, mv_ref, mqg_ref, wout_ref, o_ref, cat_ref, *, tm):
    i = pl.program_id(1)
    x = x_ref[...]
    xn = (x * _rstd(x) * g_ref[...]).astype(BF16)
    u = _dot(xn, win_ref[...])

    xh = xh_ref[...]
    xhn = (xh * _rstd(xh) * g_ref[...]).astype(BF16)
    uh = _dot(xhn, win_ref[:, :POOL_WIDTH])
    uh = jnp.where(i == 0, 0.0, uh).astype(BF16)

    ub = u[:, :POOL_WIDTH].astype(BF16)
    row = lax.broadcasted_iota(jnp.int32, (MXU_DIM, 1), 0)
    for g, w in enumerate(POOL_WINDOWS):
        cs = slice(g * POOL_GROUP, (g + 1) * POOL_GROUP)
        for sb in range(tm // MXU_DIM):
            r0 = sb * MXU_DIM
            rs = slice(r0, r0 + MXU_DIM)
            main = _dot(band_ref[g], ub[rs, cs])
            halo = uh[:, cs] if sb == 0 else ub[r0 - POOL_HALO:r0, cs]
            top = _dot(bandh_ref[g], halo)
            wsum = jnp.concatenate([main[:POOL_HALO] + top, main[POOL_HALO:]], axis=0)
            t = i * tm + r0 + row
            count = jnp.minimum(t + 1, w).astype(F32)
            pooled = wsum / count - u[rs, cs]
            tok = _dot(pooled.astype(BF16), poolw_ref[g]) * pscale_ref[:, cs]
            cat_ref[rs, cs] = tok.astype(BF16)

    def write(h, o):
        c0 = POOL_WIDTH + h * MEM_HEAD_DIM
        cat_ref[:, c0:c0 + MEM_HEAD_DIM] = o.astype(BF16)

    _mem_attention(u[:, POOL_WIDTH:], mqg_ref[...], mk_ref[0], mv_ref[0], write)
    o_ref[...] = x + _dot(cat_ref[...], wout_ref[...])


def _layer_a(x2d, g, win_bf, poolw_bf, pscale, mk, mv, mqg, wout_bf, batch, seq, tm):
    nt = seq // tm
    band, bandh = _pool_bands()
    ngrp = len(POOL_WINDOWS)
    const2 = lambda b, i: (0, 0)
    const3 = lambda b, i: (0, 0, 0)
    halo_blocks = tm // POOL_HALO
    return pl.pallas_call(
        functools.partial(_layer_a_kernel, tm=tm),
        grid=(batch, nt),
        in_specs=[
            pl.BlockSpec((tm, D_MODEL), lambda b, i: (b * nt + i, 0)),
            pl.BlockSpec((POOL_HALO, D_MODEL),
                         lambda b, i: (jnp.maximum((b * nt + i) * halo_blocks - 1, 0), 0)),
            _resident((1, D_MODEL), const2),
            _resident((D_MODEL, D_MODEL), const2),
            _resident((ngrp, MXU_DIM, MXU_DIM), const3),
            _resident((ngrp, POOL_HALO, POOL_HALO), const3),
            _resident((ngrp, POOL_GROUP, POOL_GROUP), const3),
            _resident((1, POOL_WIDTH), const2),
            pl.BlockSpec((1, MEM_LEN, MEM_WIDTH), lambda b, i: (0, b, 0)),
            pl.BlockSpec((1, MEM_LEN, MEM_WIDTH), lambda b, i: (0, b, 0)),
            _resident((1, MEM_HEAD_DIM), const2),
            _resident((D_MODEL, D_MODEL), const2),
        ],
        out_specs=pl.BlockSpec((tm, D_MODEL), lambda b, i: (b * nt + i, 0)),
        out_shape=jax.ShapeDtypeStruct(x2d.shape, F32),
        scratch_shapes=[pltpu.VMEM((tm, D_MODEL), BF16)],
        compiler_params=pltpu.CompilerParams(
            dimension_semantics=("arbitrary", "arbitrary"),
            vmem_limit_bytes=VMEM_LIMIT_BYTES),
        name="layer_a_mixer",
    )(x2d, x2d, g, win_bf, band, bandh, poolw_bf, pscale, mk, mv, mqg, wout_bf)


def _ffn_kernel(x_ref, g_ref, w1_ref, w2_ref, o_ref, xn_ref):
    k = pl.program_id(1)

    @pl.when(k == 0)
    def _():
        x = x_ref[...]
        xn_ref[...] = (x * _rstd(x) * g_ref[...]).astype(BF16)
        o_ref[...] = x

    z = jnp.maximum(_dot(xn_ref[...], w1_ref[...]), 0.0)
    o_ref[...] += _dot((z * z).astype(BF16), w2_ref[...])


def _ffn(x2d, g, w1_bf, w2_bf, tm, tf):
    m = x2d.shape[0]
    return pl.pallas_call(
        _ffn_kernel,
        grid=(m // tm, D_FF // tf),
        in_specs=[
            pl.BlockSpec((tm, D_MODEL), lambda i, k: (i, 0)),
            _resident((1, D_MODEL), lambda i, k: (0, 0)),
            pl.BlockSpec((D_MODEL, tf), lambda i, k: (0, k)),
            pl.BlockSpec((tf, D_MODEL), lambda i, k: (k, 0)),
        ],
        out_specs=pl.BlockSpec((tm, D_MODEL), lambda i, k: (i, 0)),
        out_shape=jax.ShapeDtypeStruct(x2d.shape, F32),
        scratch_shapes=[pltpu.VMEM((tm, D_MODEL), BF16)],
        compiler_params=pltpu.CompilerParams(
            dimension_semantics=("arbitrary", "arbitrary"),
            vmem_limit_bytes=VMEM_LIMIT_BYTES),
        name="ffn",
    )(x2d, g, w1_bf, w2_bf)


def _rope_coeffs(seq):
    half = ROPE_DIM // 2
    pos = jnp.arange(seq, dtype=F32)
    inv = ROPE_THETA ** (-(jnp.arange(half, dtype=F32) * 2.0) / ROPE_DIM)
    ang = pos[:, None] * inv[None, :]
    cos, sin = jnp.cos(ang), jnp.sin(ang)
    ones = jnp.ones((seq, DIFF_HEAD_DIM - ROPE_DIM), F32)
    zeros = jnp.zeros((seq, DIFF_HEAD_DIM - ROPE_DIM), F32)
    zh = jnp.zeros((seq, half), F32)
    c0 = jnp.concatenate([cos, cos, ones], axis=-1)
    c_up = jnp.concatenate([-sin, zh, zeros], axis=-1)
    c_dn = jnp.concatenate([zh, sin, zeros], axis=-1)
    rep = LANES // DIFF_HEAD_DIM
    return tuple(jnp.tile(c, (1, rep)) for c in (c0, c_up, c_dn))


def _head_norm_rope(t, gain, bd, c0, c_up, c_dn):
    ss = _dot((t * t).astype(BF16), bd)
    tn = t * lax.rsqrt(ss * (1.0 / DIFF_HEAD_DIM) + EPS) * gain
    outs = []
    half = ROPE_DIM // 2
    for p in range(MXU_DIM // LANES):
        tp = tn[:, p * LANES:(p + 1) * LANES]
        up = pltpu.roll(tp, LANES - half, axis=1)
        dn = pltpu.roll(tp, half, axis=1)
        outs.append(tp * c0 + up * c_up + dn * c_dn)
    return jnp.concatenate(outs, axis=-1)


def _layer_b_proj_kernel(x_ref, gmix_ref, gkv_ref, win_ref, wkv_ref, bd_ref, qg_ref, kg_ref,
                         c0_ref, cup_ref, cdn_ref, mk_ref, mv_ref, mqg_ref,
                         q1_ref, q2_ref, k1_ref, k2_ref, v_ref, mo_ref):
    x = x_ref[...]
    xr = x * _rstd(x)
    u = _dot((xr * gmix_ref[...]).astype(BF16), win_ref[...])
    kv = _dot((xr * gkv_ref[...]).astype(BF16), wkv_ref[...])
    bd = bd_ref[...]
    c0, cup, cdn = c0_ref[...], cup_ref[...], cdn_ref[...]
    qg = qg_ref[...] * (DIFF_HEAD_DIM ** -0.5 * math.log2(math.e))
    kg = kg_ref[...]
    for c in range(DIFF_QK_WIDTH // MXU_DIM):
        cs = slice(c * MXU_DIM, (c + 1) * MXU_DIM)
        q1_ref[:, cs] = _head_norm_rope(u[:, cs], qg, bd, c0, cup, cdn).astype(BF16)
        q2_ref[:, cs] = _head_norm_rope(
            u[:, DIFF_QK_WIDTH + c * MXU_DIM:DIFF_QK_WIDTH + (c + 1) * MXU_DIM],
            qg, bd, c0, cup, cdn).astype(BF16)
        k1_ref[:, cs] = _head_norm_rope(kv[:, cs], kg, bd, c0, cup, cdn).astype(BF16)
        k2_ref[:, cs] = _head_norm_rope(
            kv[:, DIFF_QK_WIDTH + c * MXU_DIM:DIFF_QK_WIDTH + (c + 1) * MXU_DIM],
            kg, bd, c0, cup, cdn).astype(BF16)
    v_ref[...] = kv[:, 2 * DIFF_QK_WIDTH:].astype(BF16)

    def write(h, o):
        mo_ref[:, h * MEM_HEAD_DIM:(h + 1) * MEM_HEAD_DIM] = o.astype(BF16)

    _mem_attention(u[:, 2 * DIFF_QK_WIDTH:], mqg_ref[...], mk_ref[0], mv_ref[0], write)


def _layer_b_proj(x2d, gmix, gkv, win_bf, wkv_bf, q_norm, k_norm, mk, mv, mqg, batch, seq, tm):
    nt = seq // tm
    m = x2d.shape[0]
    c0, cup, cdn = _rope_coeffs(seq)
    lane_head = jnp.arange(MXU_DIM) // DIFF_HEAD_DIM
    bd = (lane_head[:, None] == lane_head[None, :]).astype(BF16)
    qg = jnp.tile(q_norm.reshape(1, DIFF_HEAD_DIM), (1, MXU_DIM // DIFF_HEAD_DIM))
    kg = jnp.tile(k_norm.reshape(1, DIFF_HEAD_DIM), (1, MXU_DIM // DIFF_HEAD_DIM))
    const2 = lambda b, i: (0, 0)
    rows = lambda b, i: (b * nt + i, 0)
    qk_sd = jax.ShapeDtypeStruct((m, DIFF_QK_WIDTH), BF16)
    return pl.pallas_call(
        _layer_b_proj_kernel,
        grid=(batch, nt),
        in_specs=[
            pl.BlockSpec((tm, D_MODEL), rows),
            _resident((1, D_MODEL), const2),
            _resident((1, D_MODEL), const2),
            _resident((D_MODEL, D_MODEL), const2),
            _resident((D_MODEL, KV_WIDTH), const2),
            _resident((MXU_DIM, MXU_DIM), const2),
            _resident((1, MXU_DIM), const2),
            _resident((1, MXU_DIM), const2),
            pl.BlockSpec((tm, LANES), lambda b, i: (i, 0)),
            pl.BlockSpec((tm, LANES), lambda b, i: (i, 0)),
            pl.BlockSpec((tm, LANES), lambda b, i: (i, 0)),
            pl.BlockSpec((1, MEM_LEN, MEM_WIDTH), lambda b, i: (1, b, 0)),
            pl.BlockSpec((1, MEM_LEN, MEM_WIDTH), lambda b, i: (1, b, 0)),
            _resident((1, MEM_HEAD_DIM), const2),
        ],
        out_specs=[
            pl.BlockSpec((tm, DIFF_QK_WIDTH), rows),
            pl.BlockSpec((tm, DIFF_QK_WIDTH), rows),
            pl.BlockSpec((tm, DIFF_QK_WIDTH), rows),
            pl.BlockSpec((tm, DIFF_QK_WIDTH), rows),
            pl.BlockSpec((tm, DIFF_V_WIDTH), rows),
            pl.BlockSpec((tm, MEM_WIDTH), rows),
        ],
        out_shape=[qk_sd, qk_sd, qk_sd, qk_sd,
                   jax.ShapeDtypeStruct((m, DIFF_V_WIDTH), BF16),
                   jax.ShapeDtypeStruct((m, MEM_WIDTH), BF16)],
        compiler_params=pltpu.CompilerParams(
            dimension_semantics=("arbitrary", "arbitrary"),
            vmem_limit_bytes=VMEM_LIMIT_BYTES),
        name="layer_b_proj",
    )(x2d, gmix, gkv, win_bf, wkv_bf, bd, qg, kg, c0, cup, cdn, mk, mv, mqg)


def _diff_attn_kernel(q1t_ref, q2t_ref, k1_ref, k2_ref, vt_ref, lam_ref, sg_ref, o_ref,
                      s_ref, m_ref, acc_ref, *, tq, lam_init):
    i = pl.program_id(2)
    feat = lax.broadcasted_iota(jnp.int32, (LANES, 1), 0)
    heads_per_step = LANES // DIFF_HEAD_DIM
    combos = [(hh, mp) for hh in range(heads_per_step) for mp in range(2)]

    qms = []
    for hh, mp in combos:
        qt = (q1t_ref, q2t_ref)[mp][0]
        in_head = (feat >= hh * DIFF_HEAD_DIM) & (feat < (hh + 1) * DIFF_HEAD_DIM)
        qms.append(jnp.where(in_head, qt, jnp.zeros_like(qt)))

    def scores(j):
        r0 = pl.multiple_of(j * tq, tq)
        return [_dot((k1_ref, k2_ref)[mp][pl.ds(r0, tq), :], qms[c])
                for c, (hh, mp) in enumerate(combos)]

    def consume(j):
        for c, (hh, mp) in enumerate(combos):
            st = s_ref[c]
            m_old = m_ref[c]
            m_new = jnp.maximum(m_old, jnp.max(st, axis=0, keepdims=True))
            alpha = jnp.exp2(m_old - m_new)
            p = jnp.exp2(st - m_new).astype(BF16)
            acc_ref[c] = alpha * acc_ref[c] + _dot(vt_ref[0, hh, j], p)
            m_ref[c] = m_new

    m_ref[...] = jnp.full(m_ref.shape, NEG_INF, F32)
    acc_ref[...] = jnp.zeros(acc_ref.shape, F32)
    key_pos = lax.broadcasted_iota(jnp.int32, (tq, tq), 0)
    q_pos = lax.broadcasted_iota(jnp.int32, (tq, tq), 1)
    for c, st in enumerate(scores(i)):
        s_ref[c] = jnp.where(key_pos <= q_pos, st, NEG_INF)

    def body(t, carry):
        nxt = scores(t)
        consume(jnp.where(t == 0, i, t - 1))
        for c, st in enumerate(nxt):
            s_ref[c] = st
        return carry

    lax.fori_loop(0, i, body, 0)
    consume(jnp.maximum(i - 1, 0))

    lq = lam_ref[...]
    lam = (jnp.exp(jnp.sum(lq[0:1] * lq[1:2], axis=-1, keepdims=True))
           - jnp.exp(jnp.sum(lq[2:3] * lq[3:4], axis=-1, keepdims=True)) + lam_init)
    for hh in range(heads_per_step):
        a1, a2 = acc_ref[2 * hh], acc_ref[2 * hh + 1]
        ot = (a1[:DIFF_V_DIM] * (1.0 / a1[DIFF_V_DIM:DIFF_V_DIM + 1])
              - lam * (a2[:DIFF_V_DIM] * (1.0 / a2[DIFF_V_DIM:DIFF_V_DIM + 1])))
        ot = ot * lax.rsqrt(jnp.mean(ot * ot, axis=0, keepdims=True) + EPS)
        o = ot.T * sg_ref[...] * (1.0 - lam_init)
        o_ref[:, hh * DIFF_V_DIM:(hh + 1) * DIFF_V_DIM] = o.astype(BF16)


def _diff_attention(q1, q2, k1, k2, v, diff_lambda, subln, batch, seq, tq, lam_init):
    nq = seq // tq
    m = q1.shape[0]
    npairs = DIFF_QK_WIDTH // LANES
    heads_per_step = LANES // DIFF_HEAD_DIM
    vw = heads_per_step * DIFF_V_DIM
    ncombo = 2 * heads_per_step
    q1t = q1.reshape(batch, seq, DIFF_QK_WIDTH).transpose(0, 2, 1)
    q2t = q2.reshape(batch, seq, DIFF_QK_WIDTH).transpose(0, 2, 1)
    vt = v.reshape(batch, nq, tq, DIFF_HEADS, DIFF_V_DIM).transpose(0, 3, 1, 4, 2)
    extra = jnp.zeros((batch, DIFF_HEADS, nq, VT_ROWS - DIFF_V_DIM, tq), BF16).at[:, :, :, 0].set(1.0)
    vt = jnp.concatenate([vt, extra], axis=3)
    qspec = pl.BlockSpec((1, LANES, tq), lambda b, hp, i: (b, hp, i))
    kspec = pl.BlockSpec((seq, LANES), lambda b, hp, i: (b, hp))
    return pl.pallas_call(
        functools.partial(_diff_attn_kernel, tq=tq, lam_init=lam_init),
        grid=(batch, npairs, nq),
        in_specs=[
            qspec, qspec, kspec, kspec,
            pl.BlockSpec((1, heads_per_step, nq, VT_ROWS, tq), lambda b, hp, i: (b, hp, 0, 0, 0)),
            _resident((4, DIFF_HEAD_DIM), lambda b, hp, i: (0, 0)),
            _resident((1, DIFF_V_DIM), lambda b, hp, i: (0, 0)),
        ],
        out_specs=pl.BlockSpec((tq, vw), lambda b, hp, i: (b * nq + i, hp)),
        out_shape=jax.ShapeDtypeStruct((m, DIFF_V_WIDTH), BF16),
        scratch_shapes=[
            pltpu.VMEM((ncombo, tq, tq), F32),
            pltpu.VMEM((ncombo, 1, tq), F32),
            pltpu.VMEM((ncombo, VT_ROWS, tq), F32),
        ],
        compiler_params=pltpu.CompilerParams(
            dimension_semantics=("arbitrary", "arbitrary", "arbitrary"),
            vmem_limit_bytes=VMEM_LIMIT_BYTES),
        name="diff_attention",
    )(q1t, q2t, k1, k2, vt, diff_lambda, subln)


def _out_proj_kernel(tok_ref, mo_ref, x_ref, w_ref, o_ref):
    o_ref[...] = (x_ref[...] + _dot(tok_ref[...], w_ref[:DIFF_V_WIDTH, :])
                  + _dot(mo_ref[...], w_ref[DIFF_V_WIDTH:, :]))


def _out_proj(tok, mo, x2d, wout_bf, tm):
    m = x2d.shape[0]
    rows = lambda i: (i, 0)
    return pl.pallas_call(
        _out_proj_kernel,
        grid=(m // tm,),
        in_specs=[
            pl.BlockSpec((tm, DIFF_V_WIDTH), rows),
            pl.BlockSpec((tm, MEM_WIDTH), rows),
            pl.BlockSpec((tm, D_MODEL), rows),
            _resident((D_MODEL, D_MODEL), lambda i: (0, 0)),
        ],
        out_specs=pl.BlockSpec((tm, D_MODEL), rows),
        out_shape=jax.ShapeDtypeStruct(x2d.shape, F32),
        compiler_params=pltpu.CompilerParams(
            dimension_semantics=("arbitrary",),
            vmem_limit_bytes=VMEM_LIMIT_BYTES),
        name="out_proj",
    )(tok, mo, x2d, wout_bf)


def kernel(x, mem, mix_norm, w_in, w_out, mem_norm, w_mem_kv, mem_q_norm, mem_k_norm,
           ffn_norm, w_ff1, w_ff2, pool_w, pool_scale, kv_norm, w_kv, k_norm, q_norm,
           diff_lambda, subln_norm):
    batch, seq, _ = x.shape
    m = batch * seq
    x2d = x.reshape(m, D_MODEL)
    mem2d = mem.reshape(batch * MEM_LEN, D_MODEL)

    w_in_bf = w_in.astype(BF16)
    w_out_bf = w_out.astype(BF16)
    w_ff1_bf = w_ff1.astype(BF16)
    w_ff2_bf = w_ff2.astype(BF16)
    w_kv_bf = w_kv.astype(BF16)

    mk, mv = _memkv(mem2d, mem_norm, w_mem_kv.astype(BF16), mem_k_norm, batch)

    x2d = _layer_a(x2d, mix_norm[0:1], w_in_bf[0], pool_w[0].astype(BF16), pool_scale[0:1],
                   mk, mv, mem_q_norm[0:1], w_out_bf[0], batch, seq, tm=512)
    x2d = _ffn(x2d, ffn_norm[0:1], w_ff1_bf[0], w_ff2_bf[0], tm=512, tf=1024)

    lam_init = 0.8 - 0.6 * math.exp(-0.3 * 1)
    q1, q2, k1, k2, v, mo = _layer_b_proj(
        x2d, mix_norm[1:2], kv_norm.reshape(1, D_MODEL), w_in_bf[1], w_kv_bf,
        q_norm[0], k_norm, mk, mv, mem_q_norm[1:2], batch, seq, tm=256)
    tok = _diff_attention(q1, q2, k1, k2, v, diff_lambda[0], subln_norm[0:1],
                          batch, seq, tq=256, lam_init=lam_init)
    x2d = _out_proj(tok, mo, x2d, w_out_bf[1], tm=512)
    x2d = _ffn(x2d, ffn_norm[1:2], w_ff1_bf[1], w_ff2_bf[1], tm=512, tf=1024)
    return x2d.reshape(batch, seq, D_MODEL)
```

```python
import functools
import math

import jax
import jax.numpy as jnp
from jax import lax
from jax.experimental import pallas as pl
from jax.experimental.pallas import tpu as pltpu

D_MODEL = 2048
MEM_LEN = 256
MEM_HEADS = 4
MEM_HEAD_DIM = 128
MEM_WIDTH = MEM_HEADS * MEM_HEAD_DIM
POOL_WIDTH = D_MODEL - MEM_WIDTH
POOL_WINDOWS = (2, 4, 8, 16)
POOL_GROUP = POOL_WIDTH // len(POOL_WINDOWS)
POOL_HALO = 16
DIFF_HEAD_DIM = 64
DIFF_V_DIM = 128
DIFF_HEADS = POOL_WIDTH // DIFF_V_DIM
DIFF_QK_WIDTH = DIFF_HEADS * DIFF_HEAD_DIM
DIFF_V_WIDTH = DIFF_HEADS * DIFF_V_DIM
VT_ROWS = DIFF_V_DIM + 16
KV_WIDTH = 2 * DIFF_QK_WIDTH + DIFF_V_WIDTH
ROPE_THETA = 500000.0
ROPE_DIM = DIFF_HEAD_DIM // 4
D_FF = 4 * D_MODEL
EPS = 1e-6
NEG_INF = -1e30

LANES = 128
MXU_DIM = 256
VMEM_LIMIT_BYTES = 56 * 1024 * 1024

BF16 = jnp.bfloat16
F32 = jnp.float32


def _resident(shape, index_map):
    return pl.BlockSpec(shape, index_map, pipeline_mode=pl.Buffered(1))


def _rstd(x):
    return lax.rsqrt(jnp.mean(x * x, axis=-1, keepdims=True) + EPS)


def _dot(a, b):
    return jnp.dot(a, b, preferred_element_type=F32)


def _dot_nt(a, b):
    return lax.dot_general(a, b, (((1,), (1,)), ((), ())), preferred_element_type=F32)


def _memkv_kernel(mem_ref, g_ref, w_ref, kg_ref, k_ref, v_ref):
    x = mem_ref[...]
    xn = (x * _rstd(x) * g_ref[0]).astype(BF16)
    kv = _dot(xn, w_ref[0])
    for h in range(MEM_HEADS):
        hs = slice(h * MEM_HEAD_DIM, (h + 1) * MEM_HEAD_DIM)
        kh = kv[:, hs]
        k_ref[0, :, hs] = (kh * _rstd(kh) * kg_ref[0]).astype(BF16)
    v_ref[0] = kv[:, MEM_WIDTH:].astype(BF16)


def _memkv(mem2d, mem_norm, w_mem_kv_bf, mem_k_norm, batch):
    depth = mem_norm.shape[0]
    out_sd = jax.ShapeDtypeStruct((depth, batch * MEM_LEN, MEM_WIDTH), BF16)
    return pl.pallas_call(
        _memkv_kernel,
        grid=(depth, batch),
        in_specs=[
            pl.BlockSpec((MEM_LEN, D_MODEL), lambda l, b: (b, 0)),
            pl.BlockSpec((1, 1, D_MODEL), lambda l, b: (l, 0, 0)),
            pl.BlockSpec((1, D_MODEL, 2 * MEM_WIDTH), lambda l, b: (l, 0, 0)),
            pl.BlockSpec((1, 1, MEM_HEAD_DIM), lambda l, b: (l, 0, 0)),
        ],
        out_specs=[
            pl.BlockSpec((1, MEM_LEN, MEM_WIDTH), lambda l, b: (l, b, 0)),
            pl.BlockSpec((1, MEM_LEN, MEM_WIDTH), lambda l, b: (l, b, 0)),
        ],
        out_shape=[out_sd, out_sd],
        compiler_params=pltpu.CompilerParams(
            dimension_semantics=("arbitrary", "arbitrary"),
            vmem_limit_bytes=VMEM_LIMIT_BYTES),
        name="mem_kv",
    )(mem2d, mem_norm.reshape(depth, 1, D_MODEL), w_mem_kv_bf,
      mem_k_norm.reshape(depth, 1, MEM_HEAD_DIM))


def _mem_attention(q, qg, k, v, write):
    scale = MEM_HEAD_DIM ** -0.5
    for h in range(MEM_HEADS):
        hs = slice(h * MEM_HEAD_DIM, (h + 1) * MEM_HEAD_DIM)
        qh = q[:, hs]
        qn = (qh * _rstd(qh) * qg).astype(BF16)
        s = _dot_nt(qn, k[:, hs]) * scale
        p = jnp.exp(s - jnp.max(s, axis=-1, keepdims=True))
        l = jnp.sum(p, axis=-1, keepdims=True)
        write(h, _dot(p.astype(BF16), v[:, hs]) / l)


def _pool_bands():
    r = jnp.arange(MXU_DIM)[:, None]
    c = jnp.arange(MXU_DIM)[None, :]
    rh = jnp.arange(POOL_HALO)[:, None]
    ch = jnp.arange(POOL_HALO)[None, :]
    band = jnp.stack([((r - c >= 0) & (r - c < w)) for w in POOL_WINDOWS])
    bandh = jnp.stack([(rh + POOL_HALO - ch < w) for w in POOL_WINDOWS])
    return band.astype(BF16), bandh.astype(BF16)


def _layer_a_kernel(x_ref, xh_ref, g_ref, win_ref, band_ref, bandh_ref, poolw_ref, pscale_ref,
                    mk_ref, mv_ref, mqg_ref, wout_ref, o_ref, cat_ref, *, tm):
    i = pl.program_id(1)
    x = x_ref[...]
    xn = (x * _rstd(x) * g_ref[...]).astype(BF16)
    u = _dot(xn, win_ref[...])

    xh = xh_ref[...]
    xhn = (xh * _rstd(xh) * g_ref[...]).astype(BF16)
    uh = _dot(xhn, win_ref[:, :POOL_WIDTH])
    uh = jnp.where(i == 0, 0.0, uh).astype(BF16)

    ub = u[:, :POOL_WIDTH].astype(BF16)
    row = lax.broadcasted_iota(jnp.int32, (MXU_DIM, 1), 0)
    for g, w in enumerate(POOL_WINDOWS):
        cs = slice(g * POOL_GROUP, (g + 1) * POOL_GROUP)
        for sb in range(tm // MXU_DIM):
            r0 = sb * MXU_DIM
            rs = slice(r0, r0 + MXU_DIM)
            main = _dot(band_ref[g], ub[rs, cs])
            halo = uh[:, cs] if sb == 0 else ub[r0 - POOL_HALO:r0, cs]
            top = _dot(bandh_ref[g], halo)
            wsum = jnp.concatenate([main[:POOL_HALO] + top, main[POOL_HALO:]], axis=0)
            t = i * tm + r0 + row
            count = jnp.minimum(t + 1, w).astype(F32)
            pooled = wsum / count - u[rs, cs]
            tok = _dot(pooled.astype(BF16), poolw_ref[g]) * pscale_ref[:, cs]
            cat_ref[rs, cs] = tok.astype(BF16)

    def write(h, o):
        c0 = POOL_WIDTH + h * MEM_HEAD_DIM
        cat_ref[:, c0:c0 + MEM_HEAD_DIM] = o.astype(BF16)

    _mem_attention(u[:, POOL_WIDTH:], mqg_ref[...], mk_ref[0], mv_ref[0], write)
    o_ref[...] = x + _dot(cat_ref[...], wout_ref[...])


def _layer_a(x2d, g, win_bf, poolw_bf, pscale, mk, mv, mqg, wout_bf, batch, seq, tm):
    nt = seq // tm
    band, bandh = _pool_bands()
    ngrp = len(POOL_WINDOWS)
    const2 = lambda b, i: (0, 0)
    const3 = lambda b, i: (0, 0, 0)
    halo_blocks = tm // POOL_HALO
    return pl.pallas_call(
        functools.partial(_layer_a_kernel, tm=tm),
        grid=(batch, nt),
        in_specs=[
            pl.BlockSpec((tm, D_MODEL), lambda b, i: (b * nt + i, 0)),
            pl.BlockSpec((POOL_HALO, D_MODEL),
                         lambda b, i: (jnp.maximum((b * nt + i) * halo_blocks - 1, 0), 0)),
            _resident((1, D_MODEL), const2),
            _resident((D_MODEL, D_MODEL), const2),
            _resident((ngrp, MXU_DIM, MXU_DIM), const3),
            _resident((ngrp, POOL_HALO, POOL_HALO), const3),
            _resident((ngrp, POOL_GROUP, POOL_GROUP), const3),
            _resident((1, POOL_WIDTH), const2),
            pl.BlockSpec((1, MEM_LEN, MEM_WIDTH), lambda b, i: (0, b, 0)),
            pl.BlockSpec((1, MEM_LEN, MEM_WIDTH), lambda b, i: (0, b, 0)),
            _resident((1, MEM_HEAD_DIM), const2),
            _resident((D_MODEL, D_MODEL), const2),
        ],
        out_specs=pl.BlockSpec((tm, D_MODEL), lambda b, i: (b * nt + i, 0)),
        out_shape=jax.ShapeDtypeStruct(x2d.shape, F32),
        scratch_shapes=[pltpu.VMEM((tm, D_MODEL), BF16)],
        compiler_params=pltpu.CompilerParams(
            dimension_semantics=("arbitrary", "arbitrary"),
            vmem_limit_bytes=VMEM_LIMIT_BYTES),
        name="layer_a_mixer",
    )(x2d, x2d, g, win_bf, band, bandh, poolw_bf, pscale, mk, mv, mqg, wout_bf)


def _ffn_kernel(x_ref, g_ref, w1_ref, w2_ref, o_ref, xn_ref):
    k = pl.program_id(1)

    @pl.when(k == 0)
    def _():
        x = x_ref[...]
        xn_ref[...] = (x * _rstd(x) * g_ref[...]).astype(BF16)
        o_ref[...] = x

    z = jnp.maximum(_dot(xn_ref[...], w1_ref[...]), 0.0)
    o_ref[...] += _dot((z * z).astype(BF16), w2_ref[...])


def _ffn(x2d, g, w1_bf, w2_bf, tm, tf):
    m = x2d.shape[0]
    return pl.pallas_call(
        _ffn_kernel,
        grid=(m // tm, D_FF // tf),
        in_specs=[
            pl.BlockSpec((tm, D_MODEL), lambda i, k: (i, 0)),
            _resident((1, D_MODEL), lambda i, k: (0, 0)),
            pl.BlockSpec((D_MODEL, tf), lambda i, k: (0, k)),
            pl.BlockSpec((tf, D_MODEL), lambda i, k: (k, 0)),
        ],
        out_specs=pl.BlockSpec((tm, D_MODEL), lambda i, k: (i, 0)),
        out_shape=jax.ShapeDtypeStruct(x2d.shape, F32),
        scratch_shapes=[pltpu.VMEM((tm, D_MODEL), BF16)],
        compiler_params=pltpu.CompilerParams(
            dimension_semantics=("arbitrary", "arbitrary"),
            vmem_limit_bytes=VMEM_LIMIT_BYTES),
        name="ffn",
    )(x2d, g, w1_bf, w2_bf)


def _rope_coeffs(seq):
    half = ROPE_DIM // 2
    pos = jnp.arange(seq, dtype=F32)
    inv = ROPE_THETA ** (-(jnp.arange(half, dtype=F32) * 2.0) / ROPE_DIM)
    ang = pos[:, None] * inv[None, :]
    cos, sin = jnp.cos(ang), jnp.sin(ang)
    ones = jnp.ones((seq, DIFF_HEAD_DIM - ROPE_DIM), F32)
    zeros = jnp.zeros((seq, DIFF_HEAD_DIM - ROPE_DIM), F32)
    zh = jnp.zeros((seq, half), F32)
    c0 = jnp.concatenate([cos, cos, ones], axis=-1)
    c_up = jnp.concatenate([-sin, zh, zeros], axis=-1)
    c_dn = jnp.concatenate([zh, sin, zeros], axis=-1)
    rep = LANES // DIFF_HEAD_DIM
    return tuple(jnp.tile(c, (1, rep)) for c in (c0, c_up, c_dn))


def _head_norm_rope(t, gain, bd, c0, c_up, c_dn):
    ss = _dot((t * t).astype(BF16), bd)
    tn = t * lax.rsqrt(ss * (1.0 / DIFF_HEAD_DIM) + EPS) * gain
    outs = []
    half = ROPE_DIM // 2
    for p in range(MXU_DIM // LANES):
        tp = tn[:, p * LANES:(p + 1) * LANES]
        up = pltpu.roll(tp, LANES - half, axis=1)
        dn = pltpu.roll(tp, half, axis=1)
        outs.append(tp * c0 + up * c_up + dn * c_dn)
    return jnp.concatenate(outs, axis=-1)


def _layer_b_proj_kernel(x_ref, gmix_ref, gkv_ref, win_ref, wkv_ref, bd_ref, qg_ref, kg_ref,
                         c0_ref, cup_ref, cdn_ref, mk_ref, mv_ref, mqg_ref,
                         q1t_ref, q2t_ref, k1_ref, k2_ref, vt_ref, mo_ref, *, tq):
    x = x_ref[...]
    xr = x * _rstd(x)
    u = _dot((xr * gmix_ref[...]).astype(BF16), win_ref[...])
    kv = _dot((xr * gkv_ref[...]).astype(BF16), wkv_ref[...])
    bd = bd_ref[...]
    c0, cup, cdn = c0_ref[...], cup_ref[...], cdn_ref[...]
    qg = qg_ref[...] * (DIFF_HEAD_DIM ** -0.5 * math.log2(math.e))
    kg = kg_ref[...]
    for c in range(DIFF_QK_WIDTH // MXU_DIM):
        cs = slice(c * MXU_DIM, (c + 1) * MXU_DIM)
        q1t_ref[0, cs, :] = _head_norm_rope(u[:, cs], qg, bd, c0, cup, cdn).T.astype(BF16)
        q2t_ref[0, cs, :] = _head_norm_rope(
            u[:, DIFF_QK_WIDTH + c * MXU_DIM:DIFF_QK_WIDTH + (c + 1) * MXU_DIM],
            qg, bd, c0, cup, cdn).T.astype(BF16)
        k1_ref[:, cs] = _head_norm_rope(kv[:, cs], kg, bd, c0, cup, cdn).astype(BF16)
        k2_ref[:, cs] = _head_norm_rope(
            kv[:, DIFF_QK_WIDTH + c * MXU_DIM:DIFF_QK_WIDTH + (c + 1) * MXU_DIM],
            kg, bd, c0, cup, cdn).astype(BF16)
    tm = x.shape[0]
    pad_rows = VT_ROWS - DIFF_V_DIM
    ones_row = (lax.broadcasted_iota(jnp.int32, (pad_rows, tq), 0) == 0).astype(BF16)
    for h in range(DIFF_HEADS):
        c0v = 2 * DIFF_QK_WIDTH + h * DIFF_V_DIM
        vt = kv[:, c0v:c0v + DIFF_V_DIM].T.astype(BF16)
        for jb in range(tm // tq):
            vt_ref[0, h, jb, :DIFF_V_DIM, :] = vt[:, jb * tq:(jb + 1) * tq]
            vt_ref[0, h, jb, DIFF_V_DIM:, :] = ones_row

    def write(h, o):
        mo_ref[:, h * MEM_HEAD_DIM:(h + 1) * MEM_HEAD_DIM] = o.astype(BF16)

    _mem_attention(u[:, 2 * DIFF_QK_WIDTH:], mqg_ref[...], mk_ref[0], mv_ref[0], write)


def _layer_b_proj(x2d, gmix, gkv, win_bf, wkv_bf, q_norm, k_norm, mk, mv, mqg, batch, seq, tm, tq):
    nt = seq // tm
    nq = seq // tq
    m = x2d.shape[0]
    c0, cup, cdn = _rope_coeffs(seq)
    lane_head = jnp.arange(MXU_DIM) // DIFF_HEAD_DIM
    bd = (lane_head[:, None] == lane_head[None, :]).astype(BF16)
    qg = jnp.tile(q_norm.reshape(1, DIFF_HEAD_DIM), (1, MXU_DIM // DIFF_HEAD_DIM))
    kg = jnp.tile(k_norm.reshape(1, DIFF_HEAD_DIM), (1, MXU_DIM // DIFF_HEAD_DIM))
    const2 = lambda b, i: (0, 0)
    rows = lambda b, i: (b * nt + i, 0)
    qk_sd = jax.ShapeDtypeStruct((m, DIFF_QK_WIDTH), BF16)
    qt_sd = jax.ShapeDtypeStruct((batch, DIFF_QK_WIDTH, seq), BF16)
    qt_spec = pl.BlockSpec((1, DIFF_QK_WIDTH, tm), lambda b, i: (b, 0, i))
    return pl.pallas_call(
        functools.partial(_layer_b_proj_kernel, tq=tq),
        grid=(batch, nt),
        in_specs=[
            pl.BlockSpec((tm, D_MODEL), rows),
            _resident((1, D_MODEL), const2),
            _resident((1, D_MODEL), const2),
            _resident((D_MODEL, D_MODEL), const2),
            _resident((D_MODEL, KV_WIDTH), const2),
            _resident((MXU_DIM, MXU_DIM), const2),
            _resident((1, MXU_DIM), const2),
            _resident((1, MXU_DIM), const2),
            pl.BlockSpec((tm, LANES), lambda b, i: (i, 0)),
            pl.BlockSpec((tm, LANES), lambda b, i: (i, 0)),
            pl.BlockSpec((tm, LANES), lambda b, i: (i, 0)),
            pl.BlockSpec((1, MEM_LEN, MEM_WIDTH), lambda b, i: (1, b, 0)),
            pl.BlockSpec((1, MEM_LEN, MEM_WIDTH), lambda b, i: (1, b, 0)),
            _resident((1, MEM_HEAD_DIM), const2),
        ],
        out_specs=[
            qt_spec,
            qt_spec,
            pl.BlockSpec((tm, DIFF_QK_WIDTH), rows),
            pl.BlockSpec((tm, DIFF_QK_WIDTH), rows),
            pl.BlockSpec((1, DIFF_HEADS, tm // tq, VT_ROWS, tq), lambda b, i: (b, 0, i, 0, 0)),
            pl.BlockSpec((tm, MEM_WIDTH), rows),
        ],
        out_shape=[qt_sd, qt_sd, qk_sd, qk_sd,
                   jax.ShapeDtypeStruct((batch, DIFF_HEADS, nq, VT_ROWS, tq), BF16),
                   jax.ShapeDtypeStruct((m, MEM_WIDTH), BF16)],
        compiler_params=pltpu.CompilerParams(
            dimension_semantics=("arbitrary", "arbitrary"),
            vmem_limit_bytes=VMEM_LIMIT_BYTES),
        name="layer_b_proj",
    )(x2d, gmix, gkv, win_bf, wkv_bf, bd, qg, kg, c0, cup, cdn, mk, mv, mqg)


def _diff_attn_kernel(q1t_ref, q2t_ref, k1_ref, k2_ref, vt_ref, lam_ref, sg_ref, o_ref,
                      s_ref, m_ref, acc_ref, *, tq, lam_init):
    i = pl.program_id(2)
    feat = lax.broadcasted_iota(jnp.int32, (LANES, 1), 0)
    heads_per_step = LANES // DIFF_HEAD_DIM
    combos = [(hh, mp) for hh in range(heads_per_step) for mp in range(2)]

    qms = []
    for hh, mp in combos:
        qt = (q1t_ref, q2t_ref)[mp][0]
        in_head = (feat >= hh * DIFF_HEAD_DIM) & (feat < (hh + 1) * DIFF_HEAD_DIM)
        qms.append(jnp.where(in_head, qt, jnp.zeros_like(qt)))

    def scores(j):
        r0 = pl.multiple_of(j * tq, tq)
        return [_dot((k1_ref, k2_ref)[mp][pl.ds(r0, tq), :], qms[c])
                for c, (hh, mp) in enumerate(combos)]

    def consume(j):
        for c, (hh, mp) in enumerate(combos):
            st = s_ref[c]
            m_old = m_ref[c]
            m_new = jnp.maximum(m_old, jnp.max(st, axis=0, keepdims=True))
            alpha = jnp.exp2(m_old - m_new)
            p = jnp.exp2(st - m_new).astype(BF16)
            acc_ref[c] = alpha * acc_ref[c] + _dot(vt_ref[0, hh, j], p)
            m_ref[c] = m_new

    m_ref[...] = jnp.full(m_ref.shape, NEG_INF, F32)
    acc_ref[...] = jnp.zeros(acc_ref.shape, F32)
    key_pos = lax.broadcasted_iota(jnp.int32, (tq, tq), 0)
    q_pos = lax.broadcasted_iota(jnp.int32, (tq, tq), 1)
    for c, st in enumerate(scores(i)):
        s_ref[c] = jnp.where(key_pos <= q_pos, st, NEG_INF)

    def body(t, carry):
        nxt = scores(t)
        consume(jnp.where(t == 0, i, t - 1))
        for c, st in enumerate(nxt):
            s_ref[c] = st
        return carry

    lax.fori_loop(0, i, body, 0)
    consume(jnp.maximum(i - 1, 0))

    lq = lam_ref[...]
    lam = (jnp.exp(jnp.sum(lq[0:1] * lq[1:2], axis=-1, keepdims=True))
           - jnp.exp(jnp.sum(lq[2:3] * lq[3:4], axis=-1, keepdims=True)) + lam_init)
    for hh in range(heads_per_step):
        a1, a2 = acc_ref[2 * hh], acc_ref[2 * hh + 1]
        ot = (a1[:DIFF_V_DIM] * (1.0 / a1[DIFF_V_DIM:DIFF_V_DIM + 1])
              - lam * (a2[:DIFF_V_DIM] * (1.0 / a2[DIFF_V_DIM:DIFF_V_DIM + 1])))
        ot = ot * lax.rsqrt(jnp.mean(ot * ot, axis=0, keepdims=True) + EPS)
        o = ot.T * sg_ref[...] * (1.0 - lam_init)
        o_ref[:, hh * DIFF_V_DIM:(hh + 1) * DIFF_V_DIM] = o.astype(BF16)


def _diff_attention(q1t, q2t, k1, k2, vt, diff_lambda, subln, batch, seq, tq, lam_init):
    nq = seq // tq
    m = k1.shape[0]
    npairs = DIFF_QK_WIDTH // LANES
    heads_per_step = LANES // DIFF_HEAD_DIM
    vw = heads_per_step * DIFF_V_DIM
    ncombo = 2 * heads_per_step
    qspec = pl.BlockSpec((1, LANES, tq), lambda b, hp, i: (b, hp, i))
    kspec = pl.BlockSpec((seq, LANES), lambda b, hp, i: (b, hp))
    return pl.pallas_call(
        functools.partial(_diff_attn_kernel, tq=tq, lam_init=lam_init),
        grid=(batch, npairs, nq),
        in_specs=[
            qspec, qspec, kspec, kspec,
            pl.BlockSpec((1, heads_per_step, nq, VT_ROWS, tq), lambda b, hp, i: (b, hp, 0, 0, 0)),
            _resident((4, DIFF_HEAD_DIM), lambda b, hp, i: (0, 0)),
            _resident((1, DIFF_V_DIM), lambda b, hp, i: (0, 0)),
        ],
        out_specs=pl.BlockSpec((tq, vw), lambda b, hp, i: (b * nq + i, hp)),
        out_shape=jax.ShapeDtypeStruct((m, DIFF_V_WIDTH), BF16),
        scratch_shapes=[
            pltpu.VMEM((ncombo, tq, tq), F32),
            pltpu.VMEM((ncombo, 1, tq), F32),
            pltpu.VMEM((ncombo, VT_ROWS, tq), F32),
        ],
        compiler_params=pltpu.CompilerParams(
            dimension_semantics=("arbitrary", "arbitrary", "arbitrary"),
            vmem_limit_bytes=VMEM_LIMIT_BYTES),
        name="diff_attention",
    )(q1t, q2t, k1, k2, vt, diff_lambda, subln)


def _out_proj_kernel(tok_ref, mo_ref, x_ref, w_ref, o_ref):
    o_ref[...] = (x_ref[...] + _dot(tok_ref[...], w_ref[:DIFF_V_WIDTH, :])
                  + _dot(mo_ref[...], w_ref[DIFF_V_WIDTH:, :]))


def _out_proj(tok, mo, x2d, wout_bf, tm):
    m = x2d.shape[0]
    rows = lambda i: (i, 0)
    return pl.pallas_call(
        _out_proj_kernel,
        grid=(m // tm,),
        in_specs=[
            pl.BlockSpec((tm, DIFF_V_WIDTH), rows),
            pl.BlockSpec((tm, MEM_WIDTH), rows),
            pl.BlockSpec((tm, D_MODEL), rows),
            _resident((D_MODEL, D_MODEL), lambda i: (0, 0)),
        ],
        out_specs=pl.BlockSpec((tm, D_MODEL), rows),
        out_shape=jax.ShapeDtypeStruct(x2d.shape, F32),
        compiler_params=pltpu.CompilerParams(
            dimension_semantics=("arbitrary",),
            vmem_limit_bytes=VMEM_LIMIT_BYTES),
        name="out_proj",
    )(tok, mo, x2d, wout_bf)


def kernel(x, mem, mix_norm, w_in, w_out, mem_norm, w_mem_kv, mem_q_norm, mem_k_norm,
           ffn_norm, w_ff1, w_ff2, pool_w, pool_scale, kv_norm, w_kv, k_norm, q_norm,
           diff_lambda, subln_norm):
    batch, seq, _ = x.shape
    m = batch * seq
    x2d = x.reshape(m, D_MODEL)
    mem2d = mem.reshape(batch * MEM_LEN, D_MODEL)

    bf = lambda w: w.astype(BF16)

    mk, mv = _memkv(mem2d, mem_norm, bf(w_mem_kv), mem_k_norm, batch)

    x2d = _layer_a(x2d, mix_norm[0:1], bf(w_in[0]), bf(pool_w[0]), pool_scale[0:1],
                   mk, mv, mem_q_norm[0:1], bf(w_out[0]), batch, seq, tm=512)
    x2d = _ffn(x2d, ffn_norm[0:1], bf(w_ff1[0]), bf(w_ff2[0]), tm=512, tf=1024)

    lam_init = 0.8 - 0.6 * math.exp(-0.3 * 1)
    tq = 256
    q1t, q2t, k1, k2, vt, mo = _layer_b_proj(
        x2d, mix_norm[1:2], kv_norm.reshape(1, D_MODEL), bf(w_in[1]), bf(w_kv),
        q_norm[0], k_norm, mk, mv, mem_q_norm[1:2], batch, seq, tm=256, tq=tq)
    tok = _diff_attention(q1t, q2t, k1, k2, vt, diff_lambda[0], subln_norm[0:1],
                          batch, seq, tq=tq, lam_init=lam_init)
    x2d = _out_proj(tok, mo, x2d, bf(w_out[1]), tm=512)
    x2d = _ffn(x2d, ffn_norm[1:2], bf(w_ff1[1]), bf(w_ff2[1]), tm=512, tf=1024)
    return x2d.reshape(batch, seq, D_MODEL)
```

```python
import functools
import math

import jax
import jax.numpy as jnp
from jax import lax
from jax.experimental import pallas as pl
from jax.experimental.pallas import tpu as pltpu

D_MODEL = 2048
MEM_LEN = 256
MEM_HEADS = 4
MEM_HEAD_DIM = 128
MEM_WIDTH = MEM_HEADS * MEM_HEAD_DIM
POOL_WIDTH = D_MODEL - MEM_WIDTH
POOL_WINDOWS = (2, 4, 8, 16)
POOL_GROUP = POOL_WIDTH // len(POOL_WINDOWS)
POOL_HALO = 16
DIFF_HEAD_DIM = 64
DIFF_V_DIM = 128
DIFF_HEADS = POOL_WIDTH // DIFF_V_DIM
DIFF_QK_WIDTH = DIFF_HEADS * DIFF_HEAD_DIM
DIFF_V_WIDTH = DIFF_HEADS * DIFF_V_DIM
VT_ROWS = DIFF_V_DIM + 16
KV_WIDTH = 2 * DIFF_QK_WIDTH + DIFF_V_WIDTH
ROPE_THETA = 500000.0
ROPE_DIM = DIFF_HEAD_DIM // 4
D_FF = 4 * D_MODEL
EPS = 1e-6
NEG_INF = -1e30

LANES = 128
MXU_DIM = 256
VMEM_LIMIT_BYTES = 56 * 1024 * 1024

BF16 = jnp.bfloat16
F32 = jnp.float32


def _resident(shape, index_map):
    return pl.BlockSpec(shape, index_map, pipeline_mode=pl.Buffered(1))


def _rstd(x):
    return lax.rsqrt(jnp.mean(x * x, axis=-1, keepdims=True) + EPS)


def _dot(a, b):
    return jnp.dot(a, b, preferred_element_type=F32)


def _dot_nt(a, b):
    return lax.dot_general(a, b, (((1,), (1,)), ((), ())), preferred_element_type=F32)


def _memkv_kernel(mem_ref, g_ref, w_ref, kg_ref, k_ref, v_ref):
    x = mem_ref[...]
    xn = (x * _rstd(x) * g_ref[0]).astype(BF16)
    kv = _dot(xn, w_ref[0])
    for h in range(MEM_HEADS):
        hs = slice(h * MEM_HEAD_DIM, (h + 1) * MEM_HEAD_DIM)
        kh = kv[:, hs]
        k_ref[0, :, hs] = (kh * _rstd(kh) * kg_ref[0]).astype(BF16)
    v_ref[0] = kv[:, MEM_WIDTH:].astype(BF16)


def _memkv(mem2d, mem_norm, w_mem_kv_bf, mem_k_norm, batch):
    depth = mem_norm.shape[0]
    out_sd = jax.ShapeDtypeStruct((depth, batch * MEM_LEN, MEM_WIDTH), BF16)
    return pl.pallas_call(
        _memkv_kernel,
        grid=(depth, batch),
        in_specs=[
            pl.BlockSpec((MEM_LEN, D_MODEL), lambda l, b: (b, 0)),
            pl.BlockSpec((1, 1, D_MODEL), lambda l, b: (l, 0, 0)),
            pl.BlockSpec((1, D_MODEL, 2 * MEM_WIDTH), lambda l, b: (l, 0, 0)),
            pl.BlockSpec((1, 1, MEM_HEAD_DIM), lambda l, b: (l, 0, 0)),
        ],
        out_specs=[
            pl.BlockSpec((1, MEM_LEN, MEM_WIDTH), lambda l, b: (l, b, 0)),
            pl.BlockSpec((1, MEM_LEN, MEM_WIDTH), lambda l, b: (l, b, 0)),
        ],
        out_shape=[out_sd, out_sd],
        compiler_params=pltpu.CompilerParams(
            dimension_semantics=("arbitrary", "arbitrary"),
            vmem_limit_bytes=VMEM_LIMIT_BYTES),
        name="mem_kv",
    )(mem2d, mem_norm.reshape(depth, 1, D_MODEL), w_mem_kv_bf,
      mem_k_norm.reshape(depth, 1, MEM_HEAD_DIM))


def _mem_attention(q, qg, k, v, write):
    scale = MEM_HEAD_DIM ** -0.5
    for h in range(MEM_HEADS):
        hs = slice(h * MEM_HEAD_DIM, (h + 1) * MEM_HEAD_DIM)
        qh = q[:, hs]
        qn = (qh * _rstd(qh) * qg).astype(BF16)
        s = _dot_nt(qn, k[:, hs]) * scale
        p = jnp.exp(s - jnp.max(s, axis=-1, keepdims=True))
        l = jnp.sum(p, axis=-1, keepdims=True)
        write(h, _dot(p.astype(BF16), v[:, hs]) / l)


def _pool_bands():
    r = jnp.arange(MXU_DIM)[:, None]
    c = jnp.arange(MXU_DIM)[None, :]
    rh = jnp.arange(POOL_HALO)[:, None]
    ch = jnp.arange(POOL_HALO)[None, :]
    band = jnp.stack([((r - c >= 0) & (r - c < w)) for w in POOL_WINDOWS])
    bandh = jnp.stack([(rh + POOL_HALO - ch < w) for w in POOL_WINDOWS])
    return band.astype(BF16), bandh.astype(BF16)


def _layer_a_kernel(x_ref, xh_ref, g_ref, win_ref, band_ref, bandh_ref, poolw_ref, pscale_ref,
                    mk_ref, mv_ref, mqg_ref, wout_ref, o_ref, cat_ref, *, tm):
    i = pl.program_id(1)
    x = x_ref[...]
    xn = (x * _rstd(x) * g_ref[...]).astype(BF16)
    u = _dot(xn, win_ref[...])

    xh = xh_ref[...]
    xhn = (xh * _rstd(xh) * g_ref[...]).astype(BF16)
    uh = _dot(xhn, win_ref[:, :POOL_WIDTH])
    uh = jnp.where(i == 0, 0.0, uh).astype(BF16)

    ub = u[:, :POOL_WIDTH].astype(BF16)
    row = lax.broadcasted_iota(jnp.int32, (MXU_DIM, 1), 0)
    for g, w in enumerate(POOL_WINDOWS):
        cs = slice(g * POOL_GROUP, (g + 1) * POOL_GROUP)
        for sb in range(tm // MXU_DIM):
            r0 = sb * MXU_DIM
            rs = slice(r0, r0 + MXU_DIM)
            main = _dot(band_ref[g], ub[rs, cs])
            halo = uh[:, cs] if sb == 0 else ub[r0 - POOL_HALO:r0, cs]
            top = _dot(bandh_ref[g], halo)
            wsum = jnp.concatenate([main[:POOL_HALO] + top, main[POOL_HALO:]], axis=0)
            t = i * tm + r0 + row
            count = jnp.minimum(t + 1, w).astype(F32)
            pooled = wsum / count - u[rs, cs]
            tok = _dot(pooled.astype(BF16), poolw_ref[g]) * pscale_ref[:, cs]
            cat_ref[rs, cs] = tok.astype(BF16)

    def write(h, o):
        c0 = POOL_WIDTH + h * MEM_HEAD_DIM
        cat_ref[:, c0:c0 + MEM_HEAD_DIM] = o.astype(BF16)

    _mem_attention(u[:, POOL_WIDTH:], mqg_ref[...], mk_ref[0], mv_ref[0], write)
    o_ref[...] = x + _dot(cat_ref[...], wout_ref[...])


def _layer_a(x2d, g, win_bf, wout_bf, layer, poolw_bf, pscale, mk, mv, mqg, batch, seq, tm):
    nt = seq // tm
    wslab = lambda b, i: (layer, 0, 0)
    band, bandh = _pool_bands()
    ngrp = len(POOL_WINDOWS)
    const2 = lambda b, i: (0, 0)
    const3 = lambda b, i: (0, 0, 0)
    halo_blocks = tm // POOL_HALO
    return pl.pallas_call(
        functools.partial(_layer_a_kernel, tm=tm),
        grid=(batch, nt),
        in_specs=[
            pl.BlockSpec((tm, D_MODEL), lambda b, i: (b * nt + i, 0)),
            pl.BlockSpec((POOL_HALO, D_MODEL),
                         lambda b, i: (jnp.maximum((b * nt + i) * halo_blocks - 1, 0), 0)),
            _resident((1, D_MODEL), const2),
            _resident((None, D_MODEL, D_MODEL), wslab),
            _resident((ngrp, MXU_DIM, MXU_DIM), const3),
            _resident((ngrp, POOL_HALO, POOL_HALO), const3),
            _resident((ngrp, POOL_GROUP, POOL_GROUP), const3),
            _resident((1, POOL_WIDTH), const2),
            pl.BlockSpec((1, MEM_LEN, MEM_WIDTH), lambda b, i: (layer, b, 0)),
            pl.BlockSpec((1, MEM_LEN, MEM_WIDTH), lambda b, i: (layer, b, 0)),
            _resident((1, MEM_HEAD_DIM), const2),
            _resident((None, D_MODEL, D_MODEL), wslab),
        ],
        out_specs=pl.BlockSpec((tm, D_MODEL), lambda b, i: (b * nt + i, 0)),
        out_shape=jax.ShapeDtypeStruct(x2d.shape, F32),
        scratch_shapes=[pltpu.VMEM((tm, D_MODEL), BF16)],
        compiler_params=pltpu.CompilerParams(
            dimension_semantics=("arbitrary", "arbitrary"),
            vmem_limit_bytes=VMEM_LIMIT_BYTES),
        name="layer_a_mixer",
    )(x2d, x2d, g, win_bf, band, bandh, poolw_bf, pscale, mk, mv, mqg, wout_bf)


def _ffn_kernel(x_ref, g_ref, w1_ref, w2_ref, o_ref, xn_ref):
    k = pl.program_id(1)

    @pl.when(k == 0)
    def _():
        x = x_ref[...]
        xn_ref[...] = (x * _rstd(x) * g_ref[...]).astype(BF16)
        o_ref[...] = x

    z = jnp.maximum(_dot(xn_ref[...], w1_ref[...]), 0.0)
    o_ref[...] += _dot((z * z).astype(BF16), w2_ref[...])


def _ffn(x2d, g, w1_bf, w2_bf, layer, tm, tf):
    m = x2d.shape[0]
    return pl.pallas_call(
        _ffn_kernel,
        grid=(m // tm, D_FF // tf),
        in_specs=[
            pl.BlockSpec((tm, D_MODEL), lambda i, k: (i, 0)),
            _resident((1, D_MODEL), lambda i, k: (0, 0)),
            pl.BlockSpec((None, D_MODEL, tf), lambda i, k: (layer, 0, k)),
            pl.BlockSpec((None, tf, D_MODEL), lambda i, k: (layer, k, 0)),
        ],
        out_specs=pl.BlockSpec((tm, D_MODEL), lambda i, k: (i, 0)),
        out_shape=jax.ShapeDtypeStruct(x2d.shape, F32),
        scratch_shapes=[pltpu.VMEM((tm, D_MODEL), BF16)],
        compiler_params=pltpu.CompilerParams(
            dimension_semantics=("arbitrary", "arbitrary"),
            vmem_limit_bytes=VMEM_LIMIT_BYTES),
        name="ffn",
    )(x2d, g, w1_bf, w2_bf)


def _rope_coeffs(seq):
    half = ROPE_DIM // 2
    pos = jnp.arange(seq, dtype=F32)
    inv = ROPE_THETA ** (-(jnp.arange(half, dtype=F32) * 2.0) / ROPE_DIM)
    ang = pos[:, None] * inv[None, :]
    cos, sin = jnp.cos(ang), jnp.sin(ang)
    ones = jnp.ones((seq, DIFF_HEAD_DIM - ROPE_DIM), F32)
    zeros = jnp.zeros((seq, DIFF_HEAD_DIM - ROPE_DIM), F32)
    zh = jnp.zeros((seq, half), F32)
    c0 = jnp.concatenate([cos, cos, ones], axis=-1)
    c_up = jnp.concatenate([-sin, zh, zeros], axis=-1)
    c_dn = jnp.concatenate([zh, sin, zeros], axis=-1)
    rep = LANES // DIFF_HEAD_DIM
    return tuple(jnp.tile(c, (1, rep)) for c in (c0, c_up, c_dn))


def _head_norm_rope(t, gain, bd, c0, c_up, c_dn):
    ss = _dot((t * t).astype(BF16), bd)
    tn = t * lax.rsqrt(ss * (1.0 / DIFF_HEAD_DIM) + EPS) * gain
    outs = []
    half = ROPE_DIM // 2
    for p in range(MXU_DIM // LANES):
        tp = tn[:, p * LANES:(p + 1) * LANES]
        up = pltpu.roll(tp, LANES - half, axis=1)
        dn = pltpu.roll(tp, half, axis=1)
        outs.append(tp * c0 + up * c_up + dn * c_dn)
    return jnp.concatenate(outs, axis=-1)


def _layer_b_proj_kernel(x_ref, gmix_ref, gkv_ref, win_ref, wkv_ref, bd_ref, qg_ref, kg_ref,
                         c0_ref, cup_ref, cdn_ref, mk_ref, mv_ref, mqg_ref,
                         q1t_ref, q2t_ref, k1_ref, k2_ref, vt_ref, mo_ref):
    x = x_ref[...]
    xr = x * _rstd(x)
    u = _dot((xr * gmix_ref[...]).astype(BF16), win_ref[...])
    kv = _dot((xr * gkv_ref[...]).astype(BF16), wkv_ref[...])
    bd = bd_ref[...]
    c0, cup, cdn = c0_ref[...], cup_ref[...], cdn_ref[...]
    qg = qg_ref[...] * (DIFF_HEAD_DIM ** -0.5 * math.log2(math.e))
    kg = kg_ref[...]
    pairs_per_chunk = MXU_DIM // LANES
    for c in range(DIFF_QK_WIDTH // MXU_DIM):
        cs = slice(c * MXU_DIM, (c + 1) * MXU_DIM)
        cs2 = slice(DIFF_QK_WIDTH + c * MXU_DIM, DIFF_QK_WIDTH + (c + 1) * MXU_DIM)
        q1t = _head_norm_rope(u[:, cs], qg, bd, c0, cup, cdn).T.astype(BF16)
        q2t = _head_norm_rope(u[:, cs2], qg, bd, c0, cup, cdn).T.astype(BF16)
        for p in range(pairs_per_chunk):
            q1t_ref[0, pairs_per_chunk * c + p, 0] = q1t[p * LANES:(p + 1) * LANES]
            q2t_ref[0, pairs_per_chunk * c + p, 0] = q2t[p * LANES:(p + 1) * LANES]
        k1_ref[:, cs] = _head_norm_rope(kv[:, cs], kg, bd, c0, cup, cdn).astype(BF16)
        k2_ref[:, cs] = _head_norm_rope(kv[:, cs2], kg, bd, c0, cup, cdn).astype(BF16)
    tm = x.shape[0]
    pad_rows = VT_ROWS - DIFF_V_DIM
    ones_row = (lax.broadcasted_iota(jnp.int32, (pad_rows, tm), 0) == 0).astype(BF16)
    for h in range(DIFF_HEADS):
        c0v = 2 * DIFF_QK_WIDTH + h * DIFF_V_DIM
        vt_ref[0, h, 0, :DIFF_V_DIM, :] = kv[:, c0v:c0v + DIFF_V_DIM].T.astype(BF16)
        vt_ref[0, h, 0, DIFF_V_DIM:, :] = ones_row

    def write(h, o):
        mo_ref[:, h * MEM_HEAD_DIM:(h + 1) * MEM_HEAD_DIM] = o.astype(BF16)

    _mem_attention(u[:, 2 * DIFF_QK_WIDTH:], mqg_ref[...], mk_ref[0], mv_ref[0], write)


def _layer_b_proj(x2d, gmix, gkv, win_bf, layer, wkv_bf, q_norm, k_norm, mk, mv, mqg,
                  batch, seq, tm, tq):
    nt = seq // tm
    nq = seq // tq
    sub = tq // tm
    npairs = DIFF_QK_WIDTH // LANES
    m = x2d.shape[0]
    c0, cup, cdn = _rope_coeffs(seq)
    lane_head = jnp.arange(MXU_DIM) // DIFF_HEAD_DIM
    bd = (lane_head[:, None] == lane_head[None, :]).astype(BF16)
    qg = jnp.tile(q_norm.reshape(1, DIFF_HEAD_DIM), (1, MXU_DIM // DIFF_HEAD_DIM))
    kg = jnp.tile(k_norm.reshape(1, DIFF_HEAD_DIM), (1, MXU_DIM // DIFF_HEAD_DIM))
    const2 = lambda b, i: (0, 0)
    rows = lambda b, i: (b * nt + i, 0)
    qk_sd = jax.ShapeDtypeStruct((m, DIFF_QK_WIDTH), BF16)
    qt_sd = jax.ShapeDtypeStruct((batch, npairs, nq, LANES, tq), BF16)
    qt_spec = pl.BlockSpec((1, npairs, 1, LANES, tm), lambda b, i: (b, 0, i // sub, 0, i % sub))
    return pl.pallas_call(
        _layer_b_proj_kernel,
        grid=(batch, nt),
        in_specs=[
            pl.BlockSpec((tm, D_MODEL), rows),
            _resident((1, D_MODEL), const2),
            _resident((1, D_MODEL), const2),
            _resident((None, D_MODEL, D_MODEL), lambda b, i: (layer, 0, 0)),
            _resident((D_MODEL, KV_WIDTH), const2),
            _resident((MXU_DIM, MXU_DIM), const2),
            _resident((1, MXU_DIM), const2),
            _resident((1, MXU_DIM), const2),
            pl.BlockSpec((tm, LANES), lambda b, i: (i, 0)),
            pl.BlockSpec((tm, LANES), lambda b, i: (i, 0)),
            pl.BlockSpec((tm, LANES), lambda b, i: (i, 0)),
            pl.BlockSpec((1, MEM_LEN, MEM_WIDTH), lambda b, i: (layer, b, 0)),
            pl.BlockSpec((1, MEM_LEN, MEM_WIDTH), lambda b, i: (layer, b, 0)),
            _resident((1, MEM_HEAD_DIM), const2),
        ],
        out_specs=[
            qt_spec,
            qt_spec,
            pl.BlockSpec((tm, DIFF_QK_WIDTH), rows),
            pl.BlockSpec((tm, DIFF_QK_WIDTH), rows),
            pl.BlockSpec((1, DIFF_HEADS, 1, VT_ROWS, tm),
                         lambda b, i: (b, 0, i // sub, 0, i % sub)),
            pl.BlockSpec((tm, MEM_WIDTH), rows),
        ],
        out_shape=[qt_sd, qt_sd, qk_sd, qk_sd,
                   jax.ShapeDtypeStruct((batch, DIFF_HEADS, nq, VT_ROWS, tq), BF16),
                   jax.ShapeDtypeStruct((m, MEM_WIDTH), BF16)],
        compiler_params=pltpu.CompilerParams(
            dimension_semantics=("arbitrary", "arbitrary"),
            vmem_limit_bytes=VMEM_LIMIT_BYTES),
        name="layer_b_proj",
    )(x2d, gmix, gkv, win_bf, wkv_bf, bd, qg, kg, c0, cup, cdn, mk, mv, mqg)


def _diff_attn_kernel(q1t_ref, q2t_ref, k1_ref, k2_ref, vt_ref, lam_ref, sg_ref, o_ref,
                      s_ref, m_ref, acc_ref, *, tq, nq, lam_init):
    feat = lax.broadcasted_iota(jnp.int32, (LANES, 1), 0)
    heads_per_step = LANES // DIFF_HEAD_DIM
    combos = [(hh, mp) for hh in range(heads_per_step) for mp in range(2)]
    in_head = [(feat >= hh * DIFF_HEAD_DIM) & (feat < (hh + 1) * DIFF_HEAD_DIM)
               for hh in range(heads_per_step)]
    key_pos = lax.broadcasted_iota(jnp.int32, (tq, tq), 0)
    q_pos = lax.broadcasted_iota(jnp.int32, (tq, tq), 1)

    def scores(i, blk):
        r0 = pl.multiple_of(blk * tq, tq)
        out = []
        for hh, mp in combos:
            qt = (q1t_ref, q2t_ref)[mp][0, 0, i]
            qm = jnp.where(in_head[hh], qt, jnp.zeros_like(qt))
            out.append(_dot((k1_ref, k2_ref)[mp][pl.ds(r0, tq), :], qm))
        return out

    def store_scores(sts, diagonal):
        for c, st in enumerate(sts):
            s_ref[c] = jnp.where(key_pos <= q_pos, st, NEG_INF) if diagonal else st

    def consume(blk, first):
        for c, (hh, mp) in enumerate(combos):
            st = s_ref[c]
            m_old = jnp.where(first, NEG_INF, m_ref[c])
            m_new = jnp.maximum(m_old, jnp.max(st, axis=0, keepdims=True))
            alpha = jnp.exp2(m_old - m_new)
            p = jnp.exp2(st - m_new).astype(BF16)
            acc_ref[c] = alpha * acc_ref[c] + _dot(vt_ref[0, hh, blk], p)
            m_ref[c] = m_new

    def finalize(i):
        lq = lam_ref[...]
        lam = (jnp.exp(jnp.sum(lq[0:1] * lq[1:2], axis=-1, keepdims=True))
               - jnp.exp(jnp.sum(lq[2:3] * lq[3:4], axis=-1, keepdims=True)) + lam_init)
        r0 = pl.multiple_of(i * tq, tq)
        for hh in range(heads_per_step):
            a1, a2 = acc_ref[2 * hh], acc_ref[2 * hh + 1]
            ot = (a1[:DIFF_V_DIM] * (1.0 / a1[DIFF_V_DIM:DIFF_V_DIM + 1])
                  - lam * (a2[:DIFF_V_DIM] * (1.0 / a2[DIFF_V_DIM:DIFF_V_DIM + 1])))
            ot = ot * lax.rsqrt(jnp.mean(ot * ot, axis=0, keepdims=True) + EPS)
            o = ot.T * sg_ref[...] * (1.0 - lam_init)
            o_ref[pl.ds(r0, tq), hh * DIFF_V_DIM:(hh + 1) * DIFF_V_DIM] = o.astype(BF16)

    acc_ref[...] = jnp.zeros(acc_ref.shape, F32)
    m_ref[...] = jnp.full(m_ref.shape, NEG_INF, F32)
    store_scores(scores(0, 0), True)

    def body(t, carry):
        i, pos = carry
        last = pos == i
        ni = jnp.where(last, i + 1, i)
        npos = jnp.where(last, 0, pos + 1)
        blk = jnp.where(pos == 0, i, pos - 1)
        nblk = jnp.where(npos == 0, ni, npos - 1)

        @pl.when(last)
        def _():
            nxt = scores(ni, nblk)
            consume(blk, pos == 0)
            store_scores(nxt, True)
            finalize(i)

        @pl.when(jnp.logical_not(last))
        def _():
            nxt = scores(ni, nblk)
            consume(blk, pos == 0)
            store_scores(nxt, False)

        return ni, npos

    ntasks = nq * (nq + 1) // 2
    lax.fori_loop(0, ntasks - 1, body, (jnp.int32(0), jnp.int32(0)))
    consume(max(nq - 2, 0), nq == 1)
    finalize(nq - 1)


def _diff_attention(q1t, q2t, k1, k2, vt, diff_lambda, subln, batch, seq, tq, lam_init):
    nq = seq // tq
    m = k1.shape[0]
    npairs = DIFF_QK_WIDTH // LANES
    heads_per_step = LANES // DIFF_HEAD_DIM
    vw = heads_per_step * DIFF_V_DIM
    ncombo = 2 * heads_per_step
    qspec = pl.BlockSpec((1, 1, nq, LANES, tq), lambda b, hp: (b, hp, 0, 0, 0))
    kspec = pl.BlockSpec((seq, LANES), lambda b, hp: (b, hp))
    return pl.pallas_call(
        functools.partial(_diff_attn_kernel, tq=tq, nq=nq, lam_init=lam_init),
        grid=(batch, npairs),
        in_specs=[
            qspec, qspec, kspec, kspec,
            pl.BlockSpec((1, heads_per_step, nq, VT_ROWS, tq), lambda b, hp: (b, hp, 0, 0, 0)),
            _resident((4, DIFF_HEAD_DIM), lambda b, hp: (0, 0)),
            _resident((1, DIFF_V_DIM), lambda b, hp: (0, 0)),
        ],
        out_specs=pl.BlockSpec((seq, vw), lambda b, hp: (b, hp)),
        out_shape=jax.ShapeDtypeStruct((m, DIFF_V_WIDTH), BF16),
        scratch_shapes=[
            pltpu.VMEM((ncombo, tq, tq), F32),
            pltpu.VMEM((ncombo, 1, tq), F32),
            pltpu.VMEM((ncombo, VT_ROWS, tq), F32),
        ],
        compiler_params=pltpu.CompilerParams(
            dimension_semantics=("arbitrary", "arbitrary"),
            vmem_limit_bytes=VMEM_LIMIT_BYTES),
        name="diff_attention",
    )(q1t, q2t, k1, k2, vt, diff_lambda, subln)


def _out_proj_kernel(tok_ref, mo_ref, x_ref, w_ref, o_ref):
    o_ref[...] = (x_ref[...] + _dot(tok_ref[...], w_ref[:DIFF_V_WIDTH, :])
                  + _dot(mo_ref[...], w_ref[DIFF_V_WIDTH:, :]))


def _out_proj(tok, mo, x2d, wout_bf, layer, tm):
    m = x2d.shape[0]
    rows = lambda i: (i, 0)
    return pl.pallas_call(
        _out_proj_kernel,
        grid=(m // tm,),
        in_specs=[
            pl.BlockSpec((tm, DIFF_V_WIDTH), rows),
            pl.BlockSpec((tm, MEM_WIDTH), rows),
            pl.BlockSpec((tm, D_MODEL), rows),
            _resident((None, D_MODEL, D_MODEL), lambda i: (layer, 0, 0)),
        ],
        out_specs=pl.BlockSpec((tm, D_MODEL), rows),
        out_shape=jax.ShapeDtypeStruct(x2d.shape, F32),
        compiler_params=pltpu.CompilerParams(
            dimension_semantics=("arbitrary",),
            vmem_limit_bytes=VMEM_LIMIT_BYTES),
        name="out_proj",
    )(tok, mo, x2d, wout_bf)


def kernel(x, mem, mix_norm, w_in, w_out, mem_norm, w_mem_kv, mem_q_norm, mem_k_norm,
           ffn_norm, w_ff1, w_ff2, pool_w, pool_scale, kv_norm, w_kv, k_norm, q_norm,
           diff_lambda, subln_norm):
    batch, seq, _ = x.shape
    m = batch * seq
    x2d = x.reshape(m, D_MODEL)
    mem2d = mem.reshape(batch * MEM_LEN, D_MODEL)

    w_in_bf, w_out_bf = w_in.astype(BF16), w_out.astype(BF16)
    w_ff1_bf, w_ff2_bf = w_ff1.astype(BF16), w_ff2.astype(BF16)

    mk, mv = _memkv(mem2d, mem_norm, w_mem_kv.astype(BF16), mem_k_norm, batch)

    x2d = _layer_a(x2d, mix_norm[0:1], w_in_bf, w_out_bf, 0, pool_w[0].astype(BF16),
                   pool_scale[0:1], mk, mv, mem_q_norm[0:1], batch, seq, tm=512)
    x2d = _ffn(x2d, ffn_norm[0:1], w_ff1_bf, w_ff2_bf, 0, tm=512, tf=1024)

    lam_init = 0.8 - 0.6 * math.exp(-0.3 * 1)
    tq = 256
    q1t, q2t, k1, k2, vt, mo = _layer_b_proj(
        x2d, mix_norm[1:2], kv_norm.reshape(1, D_MODEL), w_in_bf, 1, w_kv.astype(BF16),
        q_norm[0], k_norm, mk, mv, mem_q_norm[1:2], batch, seq, tm=256, tq=tq)
    tok = _diff_attention(q1t, q2t, k1, k2, vt, diff_lambda[0], subln_norm[0:1],
                          batch, seq, tq=tq, lam_init=lam_init)
    x2d = _out_proj(tok, mo, x2d, w_out_bf, 1, tm=512)
    x2d = _ffn(x2d, ffn_norm[1:2], w_ff1_bf, w_ff2_bf, 1, tm=512, tf=1024)
    return x2d.reshape(batch, seq, D_MODEL)
```

```python
import functools
import math

import jax
import jax.numpy as jnp
from jax import lax
from jax.experimental import pallas as pl
from jax.experimental.pallas import tpu as pltpu

D_MODEL = 2048
MEM_LEN = 256
MEM_HEADS = 4
MEM_HEAD_DIM = 128
MEM_WIDTH = MEM_HEADS * MEM_HEAD_DIM
POOL_WIDTH = D_MODEL - MEM_WIDTH
POOL_WINDOWS = (2, 4, 8, 16)
POOL_GROUP = POOL_WIDTH // len(POOL_WINDOWS)
POOL_HALO = 16
DIFF_HEAD_DIM = 64
DIFF_V_DIM = 128
DIFF_HEADS = POOL_WIDTH // DIFF_V_DIM
DIFF_QK_WIDTH = DIFF_HEADS * DIFF_HEAD_DIM
DIFF_V_WIDTH = DIFF_HEADS * DIFF_V_DIM
VT_ROWS = DIFF_V_DIM + 16
KV_WIDTH = 2 * DIFF_QK_WIDTH + DIFF_V_WIDTH
ROPE_THETA = 500000.0
ROPE_DIM = DIFF_HEAD_DIM // 4
D_FF = 4 * D_MODEL
EPS = 1e-6
NEG_INF = -1e30

LANES = 128
MXU_DIM = 256
VMEM_LIMIT_BYTES = 56 * 1024 * 1024

BF16 = jnp.bfloat16
F32 = jnp.float32


def _resident(shape, index_map):
    return pl.BlockSpec(shape, index_map, pipeline_mode=pl.Buffered(1))


def _rstd(x):
    return lax.rsqrt(jnp.mean(x * x, axis=-1, keepdims=True) + EPS)


def _dot(a, b):
    return jnp.dot(a, b, preferred_element_type=F32)


def _dot_nt(a, b):
    return lax.dot_general(a, b, (((1,), (1,)), ((), ())), preferred_element_type=F32)


def _memkv_kernel(mem_ref, g_ref, w_ref, kg_ref, k_ref, v_ref):
    x = mem_ref[...]
    xn = (x * _rstd(x) * g_ref[0]).astype(BF16)
    kv = _dot(xn, w_ref[0])
    for h in range(MEM_HEADS):
        hs = slice(h * MEM_HEAD_DIM, (h + 1) * MEM_HEAD_DIM)
        kh = kv[:, hs]
        k_ref[0, :, hs] = (kh * _rstd(kh) * kg_ref[0]).astype(BF16)
    v_ref[0] = kv[:, MEM_WIDTH:].astype(BF16)


def _memkv(mem2d, mem_norm, w_mem_kv_bf, mem_k_norm, batch):
    depth = mem_norm.shape[0]
    out_sd = jax.ShapeDtypeStruct((depth, batch * MEM_LEN, MEM_WIDTH), BF16)
    return pl.pallas_call(
        _memkv_kernel,
        grid=(depth, batch),
        in_specs=[
            pl.BlockSpec((MEM_LEN, D_MODEL), lambda l, b: (b, 0)),
            pl.BlockSpec((1, 1, D_MODEL), lambda l, b: (l, 0, 0)),
            pl.BlockSpec((1, D_MODEL, 2 * MEM_WIDTH), lambda l, b: (l, 0, 0)),
            pl.BlockSpec((1, 1, MEM_HEAD_DIM), lambda l, b: (l, 0, 0)),
        ],
        out_specs=[
            pl.BlockSpec((1, MEM_LEN, MEM_WIDTH), lambda l, b: (l, b, 0)),
            pl.BlockSpec((1, MEM_LEN, MEM_WIDTH), lambda l, b: (l, b, 0)),
        ],
        out_shape=[out_sd, out_sd],
        compiler_params=pltpu.CompilerParams(
            dimension_semantics=("arbitrary", "arbitrary"),
            vmem_limit_bytes=VMEM_LIMIT_BYTES),
        name="mem_kv",
    )(mem2d, mem_norm.reshape(depth, 1, D_MODEL), w_mem_kv_bf,
      mem_k_norm.reshape(depth, 1, MEM_HEAD_DIM))


def _mem_attention_head(qh, qg, kh, vh):
    qn = (qh * _rstd(qh) * qg).astype(BF16)
    s = _dot_nt(qn, kh) * (MEM_HEAD_DIM ** -0.5)
    p = jnp.exp(s - jnp.max(s, axis=-1, keepdims=True))
    l = jnp.sum(p, axis=-1, keepdims=True)
    return _dot(p.astype(BF16), vh) / l


def _pool_bands():
    r = jnp.arange(MXU_DIM)[:, None]
    c = jnp.arange(MXU_DIM)[None, :]
    rh = jnp.arange(POOL_HALO)[:, None]
    ch = jnp.arange(POOL_HALO)[None, :]
    band = jnp.stack([((r - c >= 0) & (r - c < w)) for w in POOL_WINDOWS])
    bandh = jnp.stack([(rh + POOL_HALO - ch < w) for w in POOL_WINDOWS])
    return band.astype(BF16), bandh.astype(BF16)


def _layer_a_kernel(x_ref, xh_ref, g_ref, win_ref, band_ref, bandh_ref, poolw_ref, pscale_ref,
                    mk_ref, mv_ref, mqg_ref, wout_ref, o_ref, cat_ref, *, tm):
    i = pl.program_id(1)
    x = x_ref[...]
    xn = (x * _rstd(x) * g_ref[...]).astype(BF16)

    xh = xh_ref[...]
    xhn = (xh * _rstd(xh) * g_ref[...]).astype(BF16)
    uh = _dot(xhn, win_ref[:, :POOL_WIDTH])
    uh = jnp.where(i == 0, 0.0, uh).astype(BF16)

    row = lax.broadcasted_iota(jnp.int32, (MXU_DIM, 1), 0)
    half_width = 2 * POOL_GROUP

    def pool_epilogue(u_half, half):
        ub = u_half.astype(BF16)
        for gl in range(2):
            g = 2 * half + gl
            w = POOL_WINDOWS[g]
            ls = slice(gl * POOL_GROUP, (gl + 1) * POOL_GROUP)
            cs = slice(g * POOL_GROUP, (g + 1) * POOL_GROUP)
            for sb in range(tm // MXU_DIM):
                r0 = sb * MXU_DIM
                rs = slice(r0, r0 + MXU_DIM)
                main = _dot(band_ref[g], ub[rs, ls])
                halo = uh[:, cs] if sb == 0 else ub[r0 - POOL_HALO:r0, ls]
                top = _dot(bandh_ref[g], halo)
                wsum = jnp.concatenate([main[:POOL_HALO] + top, main[POOL_HALO:]], axis=0)
                t = i * tm + r0 + row
                count = jnp.minimum(t + 1, w).astype(F32)
                pooled = wsum / count - u_half[rs, ls]
                tok = _dot(pooled.astype(BF16), poolw_ref[g]) * pscale_ref[:, cs]
                cat_ref[rs, cs] = tok.astype(BF16)

    def out_partial(c0, c1):
        return _dot(cat_ref[:, c0:c1], wout_ref[c0:c1, :])

    u0 = _dot(xn, win_ref[:, :half_width])
    u1 = _dot(xn, win_ref[:, half_width:POOL_WIDTH])
    pool_epilogue(u0, 0)
    uq = _dot(xn, win_ref[:, POOL_WIDTH:])
    acc = x + out_partial(0, half_width)
    pool_epilogue(u1, 1)
    acc = acc + out_partial(half_width, POOL_WIDTH)
    mk, mv = mk_ref[0], mv_ref[0]
    for h in range(MEM_HEADS):
        hs = slice(h * MEM_HEAD_DIM, (h + 1) * MEM_HEAD_DIM)
        o = _mem_attention_head(uq[:, hs], mqg_ref[...], mk[:, hs], mv[:, hs])
        cat_ref[:, POOL_WIDTH + h * MEM_HEAD_DIM:POOL_WIDTH + (h + 1) * MEM_HEAD_DIM] = o.astype(BF16)
    o_ref[...] = acc + out_partial(POOL_WIDTH, D_MODEL)


def _layer_a(x2d, g, win_bf, wout_bf, layer, poolw_bf, pscale, mk, mv, mqg, batch, seq, tm):
    nt = seq // tm
    wslab = lambda b, i: (layer, 0, 0)
    band, bandh = _pool_bands()
    ngrp = len(POOL_WINDOWS)
    const2 = lambda b, i: (0, 0)
    const3 = lambda b, i: (0, 0, 0)
    halo_blocks = tm // POOL_HALO
    return pl.pallas_call(
        functools.partial(_layer_a_kernel, tm=tm),
        grid=(batch, nt),
        in_specs=[
            pl.BlockSpec((tm, D_MODEL), lambda b, i: (b * nt + i, 0)),
            pl.BlockSpec((POOL_HALO, D_MODEL),
                         lambda b, i: (jnp.maximum((b * nt + i) * halo_blocks - 1, 0), 0)),
            _resident((1, D_MODEL), const2),
            _resident((None, D_MODEL, D_MODEL), wslab),
            _resident((ngrp, MXU_DIM, MXU_DIM), const3),
            _resident((ngrp, POOL_HALO, POOL_HALO), const3),
            _resident((ngrp, POOL_GROUP, POOL_GROUP), const3),
            _resident((1, POOL_WIDTH), const2),
            pl.BlockSpec((1, MEM_LEN, MEM_WIDTH), lambda b, i: (layer, b, 0)),
            pl.BlockSpec((1, MEM_LEN, MEM_WIDTH), lambda b, i: (layer, b, 0)),
            _resident((1, MEM_HEAD_DIM), const2),
            _resident((None, D_MODEL, D_MODEL), wslab),
        ],
        out_specs=pl.BlockSpec((tm, D_MODEL), lambda b, i: (b * nt + i, 0)),
        out_shape=jax.ShapeDtypeStruct(x2d.shape, F32),
        scratch_shapes=[pltpu.VMEM((tm, D_MODEL), BF16)],
        compiler_params=pltpu.CompilerParams(
            dimension_semantics=("arbitrary", "arbitrary"),
            vmem_limit_bytes=VMEM_LIMIT_BYTES),
        name="layer_a_mixer",
    )(x2d, x2d, g, win_bf, band, bandh, poolw_bf, pscale, mk, mv, mqg, wout_bf)


def _ffn_kernel(x_ref, g_ref, w1_ref, w2_ref, o_ref, xn_ref):
    k = pl.program_id(1)

    @pl.when(k == 0)
    def _():
        x = x_ref[...]
        xn_ref[...] = (x * _rstd(x) * g_ref[...]).astype(BF16)
        o_ref[...] = x

    z = jnp.maximum(_dot(xn_ref[...], w1_ref[...]), 0.0)
    o_ref[...] += _dot((z * z).astype(BF16), w2_ref[...])


def _ffn(x2d, g, w1_bf, w2_bf, layer, tm, tf):
    m = x2d.shape[0]
    return pl.pallas_call(
        _ffn_kernel,
        grid=(m // tm, D_FF // tf),
        in_specs=[
            pl.BlockSpec((tm, D_MODEL), lambda i, k: (i, 0)),
            _resident((1, D_MODEL), lambda i, k: (0, 0)),
            pl.BlockSpec((None, D_MODEL, tf), lambda i, k: (layer, 0, k)),
            pl.BlockSpec((None, tf, D_MODEL), lambda i, k: (layer, k, 0)),
        ],
        out_specs=pl.BlockSpec((tm, D_MODEL), lambda i, k: (i, 0)),
        out_shape=jax.ShapeDtypeStruct(x2d.shape, F32),
        scratch_shapes=[pltpu.VMEM((tm, D_MODEL), BF16)],
        compiler_params=pltpu.CompilerParams(
            dimension_semantics=("arbitrary", "arbitrary"),
            vmem_limit_bytes=VMEM_LIMIT_BYTES),
        name="ffn",
    )(x2d, g, w1_bf, w2_bf)


def _rope_coeffs(seq):
    half = ROPE_DIM // 2
    pos = jnp.arange(seq, dtype=F32)
    inv = ROPE_THETA ** (-(jnp.arange(half, dtype=F32) * 2.0) / ROPE_DIM)
    ang = pos[:, None] * inv[None, :]
    cos, sin = jnp.cos(ang), jnp.sin(ang)
    ones = jnp.ones((seq, DIFF_HEAD_DIM - ROPE_DIM), F32)
    zeros = jnp.zeros((seq, DIFF_HEAD_DIM - ROPE_DIM), F32)
    zh = jnp.zeros((seq, half), F32)
    c0 = jnp.concatenate([cos, cos, ones], axis=-1)
    c_up = jnp.concatenate([-sin, zh, zeros], axis=-1)
    c_dn = jnp.concatenate([zh, sin, zeros], axis=-1)
    rep = LANES // DIFF_HEAD_DIM
    return tuple(jnp.tile(c, (1, rep)) for c in (c0, c_up, c_dn))


def _head_norm_rope(t, gain, bd, c0, c_up, c_dn):
    ss = _dot((t * t).astype(BF16), bd)
    tn = t * lax.rsqrt(ss * (1.0 / DIFF_HEAD_DIM) + EPS) * gain
    outs = []
    half = ROPE_DIM // 2
    for p in range(MXU_DIM // LANES):
        tp = tn[:, p * LANES:(p + 1) * LANES]
        up = pltpu.roll(tp, LANES - half, axis=1)
        dn = pltpu.roll(tp, half, axis=1)
        outs.append(tp * c0 + up * c_up + dn * c_dn)
    return jnp.concatenate(outs, axis=-1)


def _layer_b_proj_kernel(x_ref, gmix_ref, gkv_ref, win_ref, wkv_ref, bd_ref, qg_ref, kg_ref,
                         c0_ref, cup_ref, cdn_ref, mk_ref, mv_ref, mqg_ref,
                         q1t_ref, q2t_ref, k1_ref, k2_ref, vt_ref, mo_ref):
    x = x_ref[...]
    tm = x.shape[0]
    xr = x * _rstd(x)
    xn_mix = (xr * gmix_ref[...]).astype(BF16)
    xn_kv = (xr * gkv_ref[...]).astype(BF16)
    bd = bd_ref[...]
    c0, cup, cdn = c0_ref[...], cup_ref[...], cdn_ref[...]
    qg = qg_ref[...] * (DIFF_HEAD_DIM ** -0.5 * math.log2(math.e))
    kg = kg_ref[...]
    pairs_per_chunk = MXU_DIM // LANES
    pad_rows = VT_ROWS - DIFF_V_DIM
    ones_row = (lax.broadcasted_iota(jnp.int32, (pad_rows, tm), 0) == 0).astype(BF16)
    mk, mv = mk_ref[0], mv_ref[0]

    def q_epilogue(qt_ref, c):
        def run(t):
            qt = _head_norm_rope(t, qg, bd, c0, cup, cdn).T.astype(BF16)
            for p in range(pairs_per_chunk):
                qt_ref[0, pairs_per_chunk * c + p, 0] = qt[p * LANES:(p + 1) * LANES]
        return run

    def k_epilogue(k_ref, c):
        def run(t):
            k_ref[:, c * MXU_DIM:(c + 1) * MXU_DIM] = _head_norm_rope(
                t, kg, bd, c0, cup, cdn).astype(BF16)
        return run

    def v_epilogue(c):
        def run(t):
            for p in range(pairs_per_chunk):
                h = pairs_per_chunk * c + p
                vt_ref[0, h, 0, :DIFF_V_DIM, :] = t[:, p * DIFF_V_DIM:(p + 1) * DIFF_V_DIM].T.astype(BF16)
                vt_ref[0, h, 0, DIFF_V_DIM:, :] = ones_row
        return run

    def mem_epilogue(c):
        def run(t):
            for p in range(pairs_per_chunk):
                h = pairs_per_chunk * c + p
                hs = slice(h * MEM_HEAD_DIM, (h + 1) * MEM_HEAD_DIM)
                o = _mem_attention_head(t[:, p * MEM_HEAD_DIM:(p + 1) * MEM_HEAD_DIM],
                                        mqg_ref[...], mk[:, hs], mv[:, hs])
                mo_ref[:, hs] = o.astype(BF16)
        return run

    chunks = []
    for c in range(DIFF_QK_WIDTH // MXU_DIM):
        chunks += [
            (xn_mix, win_ref, c * MXU_DIM, q_epilogue(q1t_ref, c)),
            (xn_kv, wkv_ref, c * MXU_DIM, k_epilogue(k1_ref, c)),
            (xn_kv, wkv_ref, 2 * DIFF_QK_WIDTH + 2 * c * MXU_DIM, v_epilogue(2 * c)),
            (xn_mix, win_ref, DIFF_QK_WIDTH + c * MXU_DIM, q_epilogue(q2t_ref, c)),
            (xn_kv, wkv_ref, DIFF_QK_WIDTH + c * MXU_DIM, k_epilogue(k2_ref, c)),
            (xn_kv, wkv_ref, 2 * DIFF_QK_WIDTH + (2 * c + 1) * MXU_DIM, v_epilogue(2 * c + 1)),
        ]
    for c in range(MEM_WIDTH // MXU_DIM):
        chunks.append((xn_mix, win_ref, 2 * DIFF_QK_WIDTH + c * MXU_DIM, mem_epilogue(c)))

    def project(chunk):
        xn, w_ref, col, _ = chunk
        return _dot(xn, w_ref[:, col:col + MXU_DIM])

    pending = project(chunks[0])
    for n, chunk in enumerate(chunks):
        ready = pending
        if n + 1 < len(chunks):
            pending = project(chunks[n + 1])
        chunk[3](ready)


def _layer_b_proj(x2d, gmix, gkv, win_bf, layer, wkv_bf, q_norm, k_norm, mk, mv, mqg,
                  batch, seq, tm, tq):
    nt = seq // tm
    nq = seq // tq
    sub = tq // tm
    npairs = DIFF_QK_WIDTH // LANES
    m = x2d.shape[0]
    c0, cup, cdn = _rope_coeffs(seq)
    lane_head = jnp.arange(MXU_DIM) // DIFF_HEAD_DIM
    bd = (lane_head[:, None] == lane_head[None, :]).astype(BF16)
    qg = jnp.tile(q_norm.reshape(1, DIFF_HEAD_DIM), (1, MXU_DIM // DIFF_HEAD_DIM))
    kg = jnp.tile(k_norm.reshape(1, DIFF_HEAD_DIM), (1, MXU_DIM // DIFF_HEAD_DIM))
    const2 = lambda b, i: (0, 0)
    rows = lambda b, i: (b * nt + i, 0)
    qk_sd = jax.ShapeDtypeStruct((m, DIFF_QK_WIDTH), BF16)
    qt_sd = jax.ShapeDtypeStruct((batch, npairs, nq, LANES, tq), BF16)
    qt_spec = pl.BlockSpec((1, npairs, 1, LANES, tm), lambda b, i: (b, 0, i // sub, 0, i % sub))
    return pl.pallas_call(
        _layer_b_proj_kernel,
        grid=(batch, nt),
        in_specs=[
            pl.BlockSpec((tm, D_MODEL), rows),
            _resident((1, D_MODEL), const2),
            _resident((1, D_MODEL), const2),
            _resident((None, D_MODEL, D_MODEL), lambda b, i: (layer, 0, 0)),
            _resident((D_MODEL, KV_WIDTH), const2),
            _resident((MXU_DIM, MXU_DIM), const2),
            _resident((1, MXU_DIM), const2),
            _resident((1, MXU_DIM), const2),
            pl.BlockSpec((tm, LANES), lambda b, i: (i, 0)),
            pl.BlockSpec((tm, LANES), lambda b, i: (i, 0)),
            pl.BlockSpec((tm, LANES), lambda b, i: (i, 0)),
            pl.BlockSpec((1, MEM_LEN, MEM_WIDTH), lambda b, i: (layer, b, 0)),
            pl.BlockSpec((1, MEM_LEN, MEM_WIDTH), lambda b, i: (layer, b, 0)),
            _resident((1, MEM_HEAD_DIM), const2),
        ],
        out_specs=[
            qt_spec,
            qt_spec,
            pl.BlockSpec((tm, DIFF_QK_WIDTH), rows),
            pl.BlockSpec((tm, DIFF_QK_WIDTH), rows),
            pl.BlockSpec((1, DIFF_HEADS, 1, VT_ROWS, tm),
                         lambda b, i: (b, 0, i // sub, 0, i % sub)),
            pl.BlockSpec((tm, MEM_WIDTH), rows),
        ],
        out_shape=[qt_sd, qt_sd, qk_sd, qk_sd,
                   jax.ShapeDtypeStruct((batch, DIFF_HEADS, nq, VT_ROWS, tq), BF16),
                   jax.ShapeDtypeStruct((m, MEM_WIDTH), BF16)],
        compiler_params=pltpu.CompilerParams(
            dimension_semantics=("arbitrary", "arbitrary"),
            vmem_limit_bytes=VMEM_LIMIT_BYTES),
        name="layer_b_proj",
    )(x2d, gmix, gkv, win_bf, wkv_bf, bd, qg, kg, c0, cup, cdn, mk, mv, mqg)


def _diff_attn_kernel(q1t_ref, q2t_ref, k1_ref, k2_ref, vt_ref, lam_ref, sg_ref, o_ref,
                      s_ref, m_ref, acc_ref, *, tq, nq, lam_init):
    feat = lax.broadcasted_iota(jnp.int32, (LANES, 1), 0)
    heads_per_step = LANES // DIFF_HEAD_DIM
    combos = [(hh, mp) for hh in range(heads_per_step) for mp in range(2)]
    in_head = [(feat >= hh * DIFF_HEAD_DIM) & (feat < (hh + 1) * DIFF_HEAD_DIM)
               for hh in range(heads_per_step)]
    key_pos = lax.broadcasted_iota(jnp.int32, (tq, tq), 0)
    q_pos = lax.broadcasted_iota(jnp.int32, (tq, tq), 1)

    def scores(i, blk):
        r0 = pl.multiple_of(blk * tq, tq)
        out = []
        for hh, mp in combos:
            qt = (q1t_ref, q2t_ref)[mp][0, 0, i]
            qm = jnp.where(in_head[hh], qt, jnp.zeros_like(qt))
            out.append(_dot((k1_ref, k2_ref)[mp][pl.ds(r0, tq), :], qm))
        return out

    def store_scores(sts, diagonal):
        for c, st in enumerate(sts):
            s_ref[c] = jnp.where(key_pos <= q_pos, st, NEG_INF) if diagonal else st

    def consume(blk, first):
        for c, (hh, mp) in enumerate(combos):
            st = s_ref[c]
            m_old = jnp.where(first, NEG_INF, m_ref[c])
            m_new = jnp.maximum(m_old, jnp.max(st, axis=0, keepdims=True))
            alpha = jnp.exp2(m_old - m_new)
            p = jnp.exp2(st - m_new).astype(BF16)
            acc_ref[c] = alpha * acc_ref[c] + _dot(vt_ref[0, hh, blk], p)
            m_ref[c] = m_new

    def finalize(i):
        lq = lam_ref[...]
        lam = (jnp.exp(jnp.sum(lq[0:1] * lq[1:2], axis=-1, keepdims=True))
               - jnp.exp(jnp.sum(lq[2:3] * lq[3:4], axis=-1, keepdims=True)) + lam_init)
        r0 = pl.multiple_of(i * tq, tq)
        for hh in range(heads_per_step):
            a1, a2 = acc_ref[2 * hh], acc_ref[2 * hh + 1]
            ot = (a1[:DIFF_V_DIM] * (1.0 / a1[DIFF_V_DIM:DIFF_V_DIM + 1])
                  - lam * (a2[:DIFF_V_DIM] * (1.0 / a2[DIFF_V_DIM:DIFF_V_DIM + 1])))
            ot = ot * lax.rsqrt(jnp.mean(ot * ot, axis=0, keepdims=True) + EPS)
            o = ot.T * sg_ref[...] * (1.0 - lam_init)
            o_ref[pl.ds(r0, tq), hh * DIFF_V_DIM:(hh + 1) * DIFF_V_DIM] = o.astype(BF16)

    acc_ref[...] = jnp.zeros(acc_ref.shape, F32)
    m_ref[...] = jnp.full(m_ref.shape, NEG_INF, F32)
    store_scores(scores(0, 0), True)

    def body(t, carry):
        i, pos = carry
        last = pos == i
        ni = jnp.where(last, i + 1, i)
        npos = jnp.where(last, 0, pos + 1)
        blk = jnp.where(pos == 0, i, pos - 1)
        nblk = jnp.where(npos == 0, ni, npos - 1)

        @pl.when(last)
        def _():
            nxt = scores(ni, nblk)
            consume(blk, pos == 0)
            store_scores(nxt, True)
            finalize(i)

        @pl.when(jnp.logical_not(last))
        def _():
            nxt = scores(ni, nblk)
            consume(blk, pos == 0)
            store_scores(nxt, False)

        return ni, npos

    ntasks = nq * (nq + 1) // 2
    lax.fori_loop(0, ntasks - 1, body, (jnp.int32(0), jnp.int32(0)))
    consume(max(nq - 2, 0), nq == 1)
    finalize(nq - 1)


def _diff_attention(q1t, q2t, k1, k2, vt, diff_lambda, subln, batch, seq, tq, lam_init):
    nq = seq // tq
    m = k1.shape[0]
    npairs = DIFF_QK_WIDTH // LANES
    heads_per_step = LANES // DIFF_HEAD_DIM
    vw = heads_per_step * DIFF_V_DIM
    ncombo = 2 * heads_per_step
    qspec = pl.BlockSpec((1, 1, nq, LANES, tq), lambda b, hp: (b, hp, 0, 0, 0))
    kspec = pl.BlockSpec((seq, LANES), lambda b, hp: (b, hp))
    return pl.pallas_call(
        functools.partial(_diff_attn_kernel, tq=tq, nq=nq, lam_init=lam_init),
        grid=(batch, npairs),
        in_specs=[
            qspec, qspec, kspec, kspec,
            pl.BlockSpec((1, heads_per_step, nq, VT_ROWS, tq), lambda b, hp: (b, hp, 0, 0, 0)),
            _resident((4, DIFF_HEAD_DIM), lambda b, hp: (0, 0)),
            _resident((1, DIFF_V_DIM), lambda b, hp: (0, 0)),
        ],
        out_specs=pl.BlockSpec((seq, vw), lambda b, hp: (b, hp)),
        out_shape=jax.ShapeDtypeStruct((m, DIFF_V_WIDTH), BF16),
        scratch_shapes=[
            pltpu.VMEM((ncombo, tq, tq), F32),
            pltpu.VMEM((ncombo, 1, tq), F32),
            pltpu.VMEM((ncombo, VT_ROWS, tq), F32),
        ],
        compiler_params=pltpu.CompilerParams(
            dimension_semantics=("arbitrary", "arbitrary"),
            vmem_limit_bytes=VMEM_LIMIT_BYTES),
        name="diff_attention",
    )(q1t, q2t, k1, k2, vt, diff_lambda, subln)


def _out_proj_kernel(tok_ref, mo_ref, x_ref, w_ref, o_ref):
    o_ref[...] = (x_ref[...] + _dot(tok_ref[...], w_ref[:DIFF_V_WIDTH, :])
                  + _dot(mo_ref[...], w_ref[DIFF_V_WIDTH:, :]))


def _out_proj(tok, mo, x2d, wout_bf, layer, tm):
    m = x2d.shape[0]
    rows = lambda i: (i, 0)
    return pl.pallas_call(
        _out_proj_kernel,
        grid=(m // tm,),
        in_specs=[
            pl.BlockSpec((tm, DIFF_V_WIDTH), rows),
            pl.BlockSpec((tm, MEM_WIDTH), rows),
            pl.BlockSpec((tm, D_MODEL), rows),
            _resident((None, D_MODEL, D_MODEL), lambda i: (layer, 0, 0)),
        ],
        out_specs=pl.BlockSpec((tm, D_MODEL), rows),
        out_shape=jax.ShapeDtypeStruct(x2d.shape, F32),
        compiler_params=pltpu.CompilerParams(
            dimension_semantics=("arbitrary",),
            vmem_limit_bytes=VMEM_LIMIT_BYTES),
        name="out_proj",
    )(tok, mo, x2d, wout_bf)


def kernel(x, mem, mix_norm, w_in, w_out, mem_norm, w_mem_kv, mem_q_norm, mem_k_norm,
           ffn_norm, w_ff1, w_ff2, pool_w, pool_scale, kv_norm, w_kv, k_norm, q_norm,
           diff_lambda, subln_norm):
    batch, seq, _ = x.shape
    m = batch * seq
    x2d = x.reshape(m, D_MODEL)
    mem2d = mem.reshape(batch * MEM_LEN, D_MODEL)

    w_in_bf, w_out_bf = w_in.astype(BF16), w_out.astype(BF16)
    w_ff1_bf, w_ff2_bf = w_ff1.astype(BF16), w_ff2.astype(BF16)

    mk, mv = _memkv(mem2d, mem_norm, w_mem_kv.astype(BF16), mem_k_norm, batch)

    x2d = _layer_a(x2d, mix_norm[0:1], w_in_bf, w_out_bf, 0, pool_w[0].astype(BF16),
                   pool_scale[0:1], mk, mv, mem_q_norm[0:1], batch, seq, tm=512)
    x2d = _ffn(x2d, ffn_norm[0:1], w_ff1_bf, w_ff2_bf, 0, tm=512, tf=1024)

    lam_init = 0.8 - 0.6 * math.exp(-0.3 * 1)
    tq = 256
    q1t, q2t, k1, k2, vt, mo = _layer_b_proj(
        x2d, mix_norm[1:2], kv_norm.reshape(1, D_MODEL), w_in_bf, 1, w_kv.astype(BF16),
        q_norm[0], k_norm, mk, mv, mem_q_norm[1:2], batch, seq, tm=256, tq=tq)
    tok = _diff_attention(q1t, q2t, k1, k2, vt, diff_lambda[0], subln_norm[0:1],
                          batch, seq, tq=tq, lam_init=lam_init)
    x2d = _out_proj(tok, mo, x2d, w_out_bf, 1, tm=512)
    x2d = _ffn(x2d, ffn_norm[1:2], w_ff1_bf, w_ff2_bf, 1, tm=512, tf=1024)
    return x2d.reshape(batch, seq, D_MODEL)
```

```python
import functools
import math

import jax
import jax.numpy as jnp
from jax import lax
from jax.experimental import pallas as pl
from jax.experimental.pallas import tpu as pltpu

D_MODEL = 2048
MEM_LEN = 256
MEM_HEADS = 4
MEM_HEAD_DIM = 128
MEM_WIDTH = MEM_HEADS * MEM_HEAD_DIM
POOL_WIDTH = D_MODEL - MEM_WIDTH
POOL_WINDOWS = (2, 4, 8, 16)
POOL_GROUP = POOL_WIDTH // len(POOL_WINDOWS)
POOL_HALO = 16
DIFF_HEAD_DIM = 64
DIFF_V_DIM = 128
DIFF_HEADS = POOL_WIDTH // DIFF_V_DIM
DIFF_QK_WIDTH = DIFF_HEADS * DIFF_HEAD_DIM
DIFF_V_WIDTH = DIFF_HEADS * DIFF_V_DIM
VT_ROWS = DIFF_V_DIM + 16
KV_WIDTH = 2 * DIFF_QK_WIDTH + DIFF_V_WIDTH
ROPE_THETA = 500000.0
ROPE_DIM = DIFF_HEAD_DIM // 4
D_FF = 4 * D_MODEL
EPS = 1e-6
NEG_INF = -1e30

LANES = 128
MXU_DIM = 256
VMEM_LIMIT_BYTES = 56 * 1024 * 1024

BF16 = jnp.bfloat16
F32 = jnp.float32


def _resident(shape, index_map):
    return pl.BlockSpec(shape, index_map, pipeline_mode=pl.Buffered(1))


def _rstd(x):
    return lax.rsqrt(jnp.mean(x * x, axis=-1, keepdims=True) + EPS)


def _dot(a, b):
    return jnp.dot(a, b, preferred_element_type=F32)


def _dot_nt(a, b):
    return lax.dot_general(a, b, (((1,), (1,)), ((), ())), preferred_element_type=F32)


def _memkv_kernel(mem_ref, g_ref, w_ref, kg_ref, k_ref, v_ref):
    x = mem_ref[...]
    xn = (x * _rstd(x) * g_ref[0]).astype(BF16)
    kv = _dot(xn, w_ref[0])
    for h in range(MEM_HEADS):
        hs = slice(h * MEM_HEAD_DIM, (h + 1) * MEM_HEAD_DIM)
        kh = kv[:, hs]
        k_ref[0, :, hs] = (kh * _rstd(kh) * kg_ref[0]).astype(BF16)
    v_ref[0] = kv[:, MEM_WIDTH:].astype(BF16)


def _memkv(mem2d, mem_norm, w_mem_kv_bf, mem_k_norm, batch):
    depth = mem_norm.shape[0]
    out_sd = jax.ShapeDtypeStruct((depth, batch * MEM_LEN, MEM_WIDTH), BF16)
    return pl.pallas_call(
        _memkv_kernel,
        grid=(depth, batch),
        in_specs=[
            pl.BlockSpec((MEM_LEN, D_MODEL), lambda l, b: (b, 0)),
            pl.BlockSpec((1, 1, D_MODEL), lambda l, b: (l, 0, 0)),
            pl.BlockSpec((1, D_MODEL, 2 * MEM_WIDTH), lambda l, b: (l, 0, 0)),
            pl.BlockSpec((1, 1, MEM_HEAD_DIM), lambda l, b: (l, 0, 0)),
        ],
        out_specs=[
            pl.BlockSpec((1, MEM_LEN, MEM_WIDTH), lambda l, b: (l, b, 0)),
            pl.BlockSpec((1, MEM_LEN, MEM_WIDTH), lambda l, b: (l, b, 0)),
        ],
        out_shape=[out_sd, out_sd],
        compiler_params=pltpu.CompilerParams(
            dimension_semantics=("arbitrary", "arbitrary"),
            vmem_limit_bytes=VMEM_LIMIT_BYTES),
        name="mem_kv",
    )(mem2d, mem_norm.reshape(depth, 1, D_MODEL), w_mem_kv_bf,
      mem_k_norm.reshape(depth, 1, MEM_HEAD_DIM))


def _mem_attention_head(qh, qg, kh, vh):
    qn = (qh * _rstd(qh) * qg).astype(BF16)
    s = _dot_nt(qn, kh) * (MEM_HEAD_DIM ** -0.5)
    p = jnp.exp(s - jnp.max(s, axis=-1, keepdims=True))
    l = jnp.sum(p, axis=-1, keepdims=True)
    return _dot(p.astype(BF16), vh) / l


def _pool_bands():
    r = jnp.arange(MXU_DIM)[:, None]
    c = jnp.arange(MXU_DIM)[None, :]
    rh = jnp.arange(POOL_HALO)[:, None]
    ch = jnp.arange(POOL_HALO)[None, :]
    band = jnp.stack([((r - c >= 0) & (r - c < w)) for w in POOL_WINDOWS])
    bandh = jnp.stack([(rh + POOL_HALO - ch < w) for w in POOL_WINDOWS])
    return band.astype(BF16), bandh.astype(BF16)


def _layer_a_kernel(x_ref, xh_ref, g_ref, win_ref, band_ref, bandh_ref, poolw_ref, pscale_ref,
                    mk_ref, mv_ref, mqg_ref, wout_ref, o_ref, cat_ref, *, tm):
    i = pl.program_id(1)
    x = x_ref[...]
    xn = (x * _rstd(x) * g_ref[...]).astype(BF16)

    xh = xh_ref[...]
    xhn = (xh * _rstd(xh) * g_ref[...]).astype(BF16)
    uh = _dot(xhn, win_ref[:, :POOL_WIDTH])
    uh = jnp.where(i == 0, 0.0, uh).astype(BF16)

    row = lax.broadcasted_iota(jnp.int32, (MXU_DIM, 1), 0)
    half_width = 2 * POOL_GROUP

    def pool_epilogue(u_half, half):
        ub = u_half.astype(BF16)
        for gl in range(2):
            g = 2 * half + gl
            w = POOL_WINDOWS[g]
            ls = slice(gl * POOL_GROUP, (gl + 1) * POOL_GROUP)
            cs = slice(g * POOL_GROUP, (g + 1) * POOL_GROUP)
            for sb in range(tm // MXU_DIM):
                r0 = sb * MXU_DIM
                rs = slice(r0, r0 + MXU_DIM)
                main = _dot(band_ref[g], ub[rs, ls])
                halo = uh[:, cs] if sb == 0 else ub[r0 - POOL_HALO:r0, ls]
                top = _dot(bandh_ref[g], halo)
                wsum = jnp.concatenate([main[:POOL_HALO] + top, main[POOL_HALO:]], axis=0)
                t = i * tm + r0 + row
                count = jnp.minimum(t + 1, w).astype(F32)
                pooled = wsum / count - u_half[rs, ls]
                tok = _dot(pooled.astype(BF16), poolw_ref[g]) * pscale_ref[:, cs]
                cat_ref[rs, cs] = tok.astype(BF16)

    def out_partial(c0, c1):
        return _dot(cat_ref[:, c0:c1], wout_ref[c0:c1, :])

    u0 = _dot(xn, win_ref[:, :half_width])
    u1 = _dot(xn, win_ref[:, half_width:POOL_WIDTH])
    pool_epilogue(u0, 0)
    uq = _dot(xn, win_ref[:, POOL_WIDTH:])
    pool_epilogue(u1, 1)
    acc = x + out_partial(0, POOL_WIDTH)
    mk, mv = mk_ref[0], mv_ref[0]
    for h in range(MEM_HEADS):
        hs = slice(h * MEM_HEAD_DIM, (h + 1) * MEM_HEAD_DIM)
        o = _mem_attention_head(uq[:, hs], mqg_ref[...], mk[:, hs], mv[:, hs])
        cat_ref[:, POOL_WIDTH + h * MEM_HEAD_DIM:POOL_WIDTH + (h + 1) * MEM_HEAD_DIM] = o.astype(BF16)
    o_ref[...] = acc + out_partial(POOL_WIDTH, D_MODEL)


def _layer_a(x2d, g, win_bf, wout_bf, layer, poolw_bf, pscale, mk, mv, mqg, batch, seq, tm):
    nt = seq // tm
    wslab = lambda b, i: (layer, 0, 0)
    band, bandh = _pool_bands()
    ngrp = len(POOL_WINDOWS)
    const2 = lambda b, i: (0, 0)
    const3 = lambda b, i: (0, 0, 0)
    halo_blocks = tm // POOL_HALO
    return pl.pallas_call(
        functools.partial(_layer_a_kernel, tm=tm),
        grid=(batch, nt),
        in_specs=[
            pl.BlockSpec((tm, D_MODEL), lambda b, i: (b * nt + i, 0)),
            pl.BlockSpec((POOL_HALO, D_MODEL),
                         lambda b, i: (jnp.maximum((b * nt + i) * halo_blocks - 1, 0), 0)),
            _resident((1, D_MODEL), const2),
            _resident((None, D_MODEL, D_MODEL), wslab),
            _resident((ngrp, MXU_DIM, MXU_DIM), const3),
            _resident((ngrp, POOL_HALO, POOL_HALO), const3),
            _resident((ngrp, POOL_GROUP, POOL_GROUP), const3),
            _resident((1, POOL_WIDTH), const2),
            pl.BlockSpec((1, MEM_LEN, MEM_WIDTH), lambda b, i: (layer, b, 0)),
            pl.BlockSpec((1, MEM_LEN, MEM_WIDTH), lambda b, i: (layer, b, 0)),
            _resident((1, MEM_HEAD_DIM), const2),
            _resident((None, D_MODEL, D_MODEL), wslab),
        ],
        out_specs=pl.BlockSpec((tm, D_MODEL), lambda b, i: (b * nt + i, 0)),
        out_shape=jax.ShapeDtypeStruct(x2d.shape, F32),
        scratch_shapes=[pltpu.VMEM((tm, D_MODEL), BF16)],
        compiler_params=pltpu.CompilerParams(
            dimension_semantics=("arbitrary", "arbitrary"),
            vmem_limit_bytes=VMEM_LIMIT_BYTES),
        name="layer_a_mixer",
    )(x2d, x2d, g, win_bf, band, bandh, poolw_bf, pscale, mk, mv, mqg, wout_bf)


def _ffn_kernel(x_ref, g_ref, w1_ref, w2_ref, o_ref, xn_ref):
    k = pl.program_id(1)

    @pl.when(k == 0)
    def _():
        x = x_ref[...]
        xn_ref[...] = (x * _rstd(x) * g_ref[...]).astype(BF16)
        o_ref[...] = x

    z = jnp.maximum(_dot(xn_ref[...], w1_ref[...]), 0.0)
    o_ref[...] += _dot((z * z).astype(BF16), w2_ref[...])


def _ffn(x2d, g, w1_bf, w2_bf, layer, tm, tf):
    m = x2d.shape[0]
    return pl.pallas_call(
        _ffn_kernel,
        grid=(m // tm, D_FF // tf),
        in_specs=[
            pl.BlockSpec((tm, D_MODEL), lambda i, k: (i, 0)),
            _resident((1, D_MODEL), lambda i, k: (0, 0)),
            pl.BlockSpec((None, D_MODEL, tf), lambda i, k: (layer, 0, k)),
            pl.BlockSpec((None, tf, D_MODEL), lambda i, k: (layer, k, 0)),
        ],
        out_specs=pl.BlockSpec((tm, D_MODEL), lambda i, k: (i, 0)),
        out_shape=jax.ShapeDtypeStruct(x2d.shape, F32),
        scratch_shapes=[pltpu.VMEM((tm, D_MODEL), BF16)],
        compiler_params=pltpu.CompilerParams(
            dimension_semantics=("arbitrary", "arbitrary"),
            vmem_limit_bytes=VMEM_LIMIT_BYTES),
        name="ffn",
    )(x2d, g, w1_bf, w2_bf)


def _rope_coeffs(seq):
    half = ROPE_DIM // 2
    pos = jnp.arange(seq, dtype=F32)
    inv = ROPE_THETA ** (-(jnp.arange(half, dtype=F32) * 2.0) / ROPE_DIM)
    ang = pos[:, None] * inv[None, :]
    cos, sin = jnp.cos(ang), jnp.sin(ang)
    ones = jnp.ones((seq, DIFF_HEAD_DIM - ROPE_DIM), F32)
    zeros = jnp.zeros((seq, DIFF_HEAD_DIM - ROPE_DIM), F32)
    zh = jnp.zeros((seq, half), F32)
    c0 = jnp.concatenate([cos, cos, ones], axis=-1)
    c_up = jnp.concatenate([-sin, zh, zeros], axis=-1)
    c_dn = jnp.concatenate([zh, sin, zeros], axis=-1)
    rep = LANES // DIFF_HEAD_DIM
    return tuple(jnp.tile(c, (1, rep)) for c in (c0, c_up, c_dn))


def _head_norm_rope(t, gain, bd, c0, c_up, c_dn):
    ss = _dot((t * t).astype(BF16), bd)
    tn = t * lax.rsqrt(ss * (1.0 / DIFF_HEAD_DIM) + EPS) * gain
    outs = []
    half = ROPE_DIM // 2
    for p in range(MXU_DIM // LANES):
        tp = tn[:, p * LANES:(p + 1) * LANES]
        up = pltpu.roll(tp, LANES - half, axis=1)
        dn = pltpu.roll(tp, half, axis=1)
        outs.append(tp * c0 + up * c_up + dn * c_dn)
    return jnp.concatenate(outs, axis=-1)


def _layer_b_proj_kernel(x_ref, gmix_ref, gkv_ref, win_ref, wkv_ref, bd_ref, qg_ref, kg_ref,
                         c0_ref, cup_ref, cdn_ref, mk_ref, mv_ref, mqg_ref,
                         q1t_ref, q2t_ref, k1_ref, k2_ref, vt_ref, mo_ref):
    x = x_ref[...]
    tm = x.shape[0]
    xr = x * _rstd(x)
    xn_mix = (xr * gmix_ref[...]).astype(BF16)
    xn_kv = (xr * gkv_ref[...]).astype(BF16)
    bd = bd_ref[...]
    c0, cup, cdn = c0_ref[...], cup_ref[...], cdn_ref[...]
    qg = qg_ref[...] * (DIFF_HEAD_DIM ** -0.5 * math.log2(math.e))
    kg = kg_ref[...]
    pairs_per_chunk = MXU_DIM // LANES
    pad_rows = VT_ROWS - DIFF_V_DIM
    ones_row = (lax.broadcasted_iota(jnp.int32, (pad_rows, tm), 0) == 0).astype(BF16)
    mk, mv = mk_ref[0], mv_ref[0]

    def q_epilogue(qt_ref, c):
        def run(t):
            qt = _head_norm_rope(t, qg, bd, c0, cup, cdn).T.astype(BF16)
            for p in range(pairs_per_chunk):
                qt_ref[0, pairs_per_chunk * c + p, 0] = qt[p * LANES:(p + 1) * LANES]
        return run

    def k_epilogue(k_ref, c):
        def run(t):
            k_ref[:, c * MXU_DIM:(c + 1) * MXU_DIM] = _head_norm_rope(
                t, kg, bd, c0, cup, cdn).astype(BF16)
        return run

    def v_epilogue(c):
        def run(t):
            for p in range(pairs_per_chunk):
                h = pairs_per_chunk * c + p
                vt_ref[0, h, 0, :DIFF_V_DIM, :] = t[:, p * DIFF_V_DIM:(p + 1) * DIFF_V_DIM].T.astype(BF16)
                vt_ref[0, h, 0, DIFF_V_DIM:, :] = ones_row
        return run

    def mem_epilogue(c):
        def run(t):
            for p in range(pairs_per_chunk):
                h = pairs_per_chunk * c + p
                hs = slice(h * MEM_HEAD_DIM, (h + 1) * MEM_HEAD_DIM)
                o = _mem_attention_head(t[:, p * MEM_HEAD_DIM:(p + 1) * MEM_HEAD_DIM],
                                        mqg_ref[...], mk[:, hs], mv[:, hs])
                mo_ref[:, hs] = o.astype(BF16)
        return run

    chunks = []
    for c in range(DIFF_QK_WIDTH // MXU_DIM):
        chunks += [
            (xn_mix, win_ref, c * MXU_DIM, q_epilogue(q1t_ref, c)),
            (xn_kv, wkv_ref, c * MXU_DIM, k_epilogue(k1_ref, c)),
            (xn_kv, wkv_ref, 2 * DIFF_QK_WIDTH + 2 * c * MXU_DIM, v_epilogue(2 * c)),
            (xn_mix, win_ref, DIFF_QK_WIDTH + c * MXU_DIM, q_epilogue(q2t_ref, c)),
            (xn_kv, wkv_ref, DIFF_QK_WIDTH + c * MXU_DIM, k_epilogue(k2_ref, c)),
            (xn_kv, wkv_ref, 2 * DIFF_QK_WIDTH + (2 * c + 1) * MXU_DIM, v_epilogue(2 * c + 1)),
        ]
    for c in range(MEM_WIDTH // MXU_DIM):
        chunks.append((xn_mix, win_ref, 2 * DIFF_QK_WIDTH + c * MXU_DIM, mem_epilogue(c)))

    def project(chunk):
        xn, w_ref, col, _ = chunk
        return _dot(xn, w_ref[:, col:col + MXU_DIM])

    pending = project(chunks[0])
    for n, chunk in enumerate(chunks):
        ready = pending
        if n + 1 < len(chunks):
            pending = project(chunks[n + 1])
        chunk[3](ready)


def _layer_b_proj(x2d, gmix, gkv, win_bf, layer, wkv_bf, q_norm, k_norm, mk, mv, mqg,
                  batch, seq, tm, tq):
    nt = seq // tm
    nq = seq // tq
    sub = tq // tm
    npairs = DIFF_QK_WIDTH // LANES
    m = x2d.shape[0]
    c0, cup, cdn = _rope_coeffs(seq)
    lane_head = jnp.arange(MXU_DIM) // DIFF_HEAD_DIM
    bd = (lane_head[:, None] == lane_head[None, :]).astype(BF16)
    qg = jnp.tile(q_norm.reshape(1, DIFF_HEAD_DIM), (1, MXU_DIM // DIFF_HEAD_DIM))
    kg = jnp.tile(k_norm.reshape(1, DIFF_HEAD_DIM), (1, MXU_DIM // DIFF_HEAD_DIM))
    const2 = lambda b, i: (0, 0)
    rows = lambda b, i: (b * nt + i, 0)
    qk_sd = jax.ShapeDtypeStruct((m, DIFF_QK_WIDTH), BF16)
    qt_sd = jax.ShapeDtypeStruct((batch, npairs, nq, LANES, tq), BF16)
    qt_spec = pl.BlockSpec((1, npairs, 1, LANES, tm), lambda b, i: (b, 0, i // sub, 0, i % sub))
    return pl.pallas_call(
        _layer_b_proj_kernel,
        grid=(batch, nt),
        in_specs=[
            pl.BlockSpec((tm, D_MODEL), rows),
            _resident((1, D_MODEL), const2),
            _resident((1, D_MODEL), const2),
            _resident((None, D_MODEL, D_MODEL), lambda b, i: (layer, 0, 0)),
            _resident((D_MODEL, KV_WIDTH), const2),
            _resident((MXU_DIM, MXU_DIM), const2),
            _resident((1, MXU_DIM), const2),
            _resident((1, MXU_DIM), const2),
            pl.BlockSpec((tm, LANES), lambda b, i: (i, 0)),
            pl.BlockSpec((tm, LANES), lambda b, i: (i, 0)),
            pl.BlockSpec((tm, LANES), lambda b, i: (i, 0)),
            pl.BlockSpec((1, MEM_LEN, MEM_WIDTH), lambda b, i: (layer, b, 0)),
            pl.BlockSpec((1, MEM_LEN, MEM_WIDTH), lambda b, i: (layer, b, 0)),
            _resident((1, MEM_HEAD_DIM), const2),
        ],
        out_specs=[
            qt_spec,
            qt_spec,
            pl.BlockSpec((tm, DIFF_QK_WIDTH), rows),
            pl.BlockSpec((tm, DIFF_QK_WIDTH), rows),
            pl.BlockSpec((1, DIFF_HEADS, 1, VT_ROWS, tm),
                         lambda b, i: (b, 0, i // sub, 0, i % sub)),
            pl.BlockSpec((tm, MEM_WIDTH), rows),
        ],
        out_shape=[qt_sd, qt_sd, qk_sd, qk_sd,
                   jax.ShapeDtypeStruct((batch, DIFF_HEADS, nq, VT_ROWS, tq), BF16),
                   jax.ShapeDtypeStruct((m, MEM_WIDTH), BF16)],
        compiler_params=pltpu.CompilerParams(
            dimension_semantics=("arbitrary", "arbitrary"),
            vmem_limit_bytes=VMEM_LIMIT_BYTES),
        name="layer_b_proj",
    )(x2d, gmix, gkv, win_bf, wkv_bf, bd, qg, kg, c0, cup, cdn, mk, mv, mqg)


def _diff_attn_kernel(q1t_ref, q2t_ref, k1_ref, k2_ref, vt_ref, lam_ref, sg_ref, o_ref,
                      s_ref, m_ref, acc_ref, *, tq, nq, pairs, lam_init):
    feat = lax.broadcasted_iota(jnp.int32, (LANES, 1), 0)
    heads_per_pair = LANES // DIFF_HEAD_DIM
    combos = [(pr, hh, mp) for pr in range(pairs) for hh in range(heads_per_pair)
              for mp in range(2)]
    in_head = [(feat >= hh * DIFF_HEAD_DIM) & (feat < (hh + 1) * DIFF_HEAD_DIM)
               for hh in range(heads_per_pair)]
    key_pos = lax.broadcasted_iota(jnp.int32, (tq, tq), 0)
    q_pos = lax.broadcasted_iota(jnp.int32, (tq, tq), 1)

    def scores(i, blk):
        r0 = pl.multiple_of(blk * tq, tq)
        out = []
        for pr, hh, mp in combos:
            qt = (q1t_ref, q2t_ref)[mp][0, pr, i]
            qm = jnp.where(in_head[hh], qt, jnp.zeros_like(qt))
            kblk = (k1_ref, k2_ref)[mp][pl.ds(r0, tq), pr * LANES:(pr + 1) * LANES]
            out.append(_dot(kblk, qm))
        return out

    def store_scores(sts, diagonal):
        for c, st in enumerate(sts):
            s_ref[c] = jnp.where(key_pos <= q_pos, st, NEG_INF) if diagonal else st

    def consume(blk, first):
        for c, (pr, hh, mp) in enumerate(combos):
            st = s_ref[c]
            m_old = jnp.where(first, NEG_INF, m_ref[c])
            m_new = jnp.maximum(m_old, jnp.max(st, axis=0, keepdims=True))
            alpha = jnp.exp2(m_old - m_new)
            p = jnp.exp2(st - m_new).astype(BF16)
            vtblk = vt_ref[0, heads_per_pair * pr + hh, blk]
            acc_ref[c] = alpha * acc_ref[c] + _dot(vtblk, p)
            m_ref[c] = m_new

    def finalize(i):
        lq = lam_ref[...]
        lam = (jnp.exp(jnp.sum(lq[0:1] * lq[1:2], axis=-1, keepdims=True))
               - jnp.exp(jnp.sum(lq[2:3] * lq[3:4], axis=-1, keepdims=True)) + lam_init)
        r0 = pl.multiple_of(i * tq, tq)
        for hh in range(heads_per_pair * pairs):
            a1, a2 = acc_ref[2 * hh], acc_ref[2 * hh + 1]
            ot = (a1[:DIFF_V_DIM] * (1.0 / a1[DIFF_V_DIM:DIFF_V_DIM + 1])
                  - lam * (a2[:DIFF_V_DIM] * (1.0 / a2[DIFF_V_DIM:DIFF_V_DIM + 1])))
            ot = ot * lax.rsqrt(jnp.mean(ot * ot, axis=0, keepdims=True) + EPS)
            o = ot.T * sg_ref[...] * (1.0 - lam_init)
            o_ref[pl.ds(r0, tq), hh * DIFF_V_DIM:(hh + 1) * DIFF_V_DIM] = o.astype(BF16)

    acc_ref[...] = jnp.zeros(acc_ref.shape, F32)
    m_ref[...] = jnp.full(m_ref.shape, NEG_INF, F32)
    store_scores(scores(0, 0), True)

    def body(t, carry):
        i, pos = carry
        last = pos == i
        ni = jnp.where(last, i + 1, i)
        npos = jnp.where(last, 0, pos + 1)
        blk = jnp.where(pos == 0, i, pos - 1)
        nblk = jnp.where(npos == 0, ni, npos - 1)

        @pl.when(last)
        def _():
            nxt = scores(ni, nblk)
            consume(blk, pos == 0)
            store_scores(nxt, True)
            finalize(i)

        @pl.when(jnp.logical_not(last))
        def _():
            nxt = scores(ni, nblk)
            consume(blk, pos == 0)
            store_scores(nxt, False)

        return ni, npos

    ntasks = nq * (nq + 1) // 2
    lax.fori_loop(0, ntasks - 1, body, (jnp.int32(0), jnp.int32(0)))
    consume(max(nq - 2, 0), nq == 1)
    finalize(nq - 1)


def _diff_attention(q1t, q2t, k1, k2, vt, diff_lambda, subln, batch, seq, tq, pairs, lam_init):
    nq = seq // tq
    m = k1.shape[0]
    npairs = DIFF_QK_WIDTH // LANES
    heads_per_step = pairs * (LANES // DIFF_HEAD_DIM)
    vw = heads_per_step * DIFF_V_DIM
    ncombo = 2 * heads_per_step
    qspec = pl.BlockSpec((1, pairs, nq, LANES, tq), lambda b, hp: (b, hp, 0, 0, 0))
    kspec = pl.BlockSpec((seq, pairs * LANES), lambda b, hp: (b, hp))
    return pl.pallas_call(
        functools.partial(_diff_attn_kernel, tq=tq, nq=nq, pairs=pairs, lam_init=lam_init),
        grid=(batch, npairs // pairs),
        in_specs=[
            qspec, qspec, kspec, kspec,
            pl.BlockSpec((1, heads_per_step, nq, VT_ROWS, tq), lambda b, hp: (b, hp, 0, 0, 0)),
            _resident((4, DIFF_HEAD_DIM), lambda b, hp: (0, 0)),
            _resident((1, DIFF_V_DIM), lambda b, hp: (0, 0)),
        ],
        out_specs=pl.BlockSpec((seq, vw), lambda b, hp: (b, hp)),
        out_shape=jax.ShapeDtypeStruct((m, DIFF_V_WIDTH), BF16),
        scratch_shapes=[
            pltpu.VMEM((ncombo, tq, tq), F32),
            pltpu.VMEM((ncombo, 1, tq), F32),
            pltpu.VMEM((ncombo, VT_ROWS, tq), F32),
        ],
        compiler_params=pltpu.CompilerParams(
            dimension_semantics=("arbitrary", "arbitrary"),
            vmem_limit_bytes=VMEM_LIMIT_BYTES),
        name="diff_attention",
    )(q1t, q2t, k1, k2, vt, diff_lambda, subln)


def _out_proj_kernel(tok_ref, mo_ref, x_ref, w_ref, o_ref):
    o_ref[...] = (x_ref[...] + _dot(tok_ref[...], w_ref[:DIFF_V_WIDTH, :])
                  + _dot(mo_ref[...], w_ref[DIFF_V_WIDTH:, :]))


def _out_proj(tok, mo, x2d, wout_bf, layer, tm):
    m = x2d.shape[0]
    rows = lambda i: (i, 0)
    return pl.pallas_call(
        _out_proj_kernel,
        grid=(m // tm,),
        in_specs=[
            pl.BlockSpec((tm, DIFF_V_WIDTH), rows),
            pl.BlockSpec((tm, MEM_WIDTH), rows),
            pl.BlockSpec((tm, D_MODEL), rows),
            _resident((None, D_MODEL, D_MODEL), lambda i: (layer, 0, 0)),
        ],
        out_specs=pl.BlockSpec((tm, D_MODEL), rows),
        out_shape=jax.ShapeDtypeStruct(x2d.shape, F32),
        compiler_params=pltpu.CompilerParams(
            dimension_semantics=("arbitrary",),
            vmem_limit_bytes=VMEM_LIMIT_BYTES),
        name="out_proj",
    )(tok, mo, x2d, wout_bf)


def kernel(x, mem, mix_norm, w_in, w_out, mem_norm, w_mem_kv, mem_q_norm, mem_k_norm,
           ffn_norm, w_ff1, w_ff2, pool_w, pool_scale, kv_norm, w_kv, k_norm, q_norm,
           diff_lambda, subln_norm):
    batch, seq, _ = x.shape
    m = batch * seq
    x2d = x.reshape(m, D_MODEL)
    mem2d = mem.reshape(batch * MEM_LEN, D_MODEL)

    w_in_bf, w_out_bf = w_in.astype(BF16), w_out.astype(BF16)
    w_ff1_bf, w_ff2_bf = w_ff1.astype(BF16), w_ff2.astype(BF16)

    mk, mv = _memkv(mem2d, mem_norm, w_mem_kv.astype(BF16), mem_k_norm, batch)

    x2d = _layer_a(x2d, mix_norm[0:1], w_in_bf, w_out_bf, 0, pool_w[0].astype(BF16),
                   pool_scale[0:1], mk, mv, mem_q_norm[0:1], batch, seq, tm=512)
    x2d = _ffn(x2d, ffn_norm[0:1], w_ff1_bf, w_ff2_bf, 0, tm=512, tf=1024)

    lam_init = 0.8 - 0.6 * math.exp(-0.3 * 1)
    tq = 256
    q1t, q2t, k1, k2, vt, mo = _layer_b_proj(
        x2d, mix_norm[1:2], kv_norm.reshape(1, D_MODEL), w_in_bf, 1, w_kv.astype(BF16),
        q_norm[0], k_norm, mk, mv, mem_q_norm[1:2], batch, seq, tm=256, tq=tq)
    tok = _diff_attention(q1t, q2t, k1, k2, vt, diff_lambda[0], subln_norm[0:1],
                          batch, seq, tq=tq, pairs=2, lam_init=lam_init)
    x2d = _out_proj(tok, mo, x2d, w_out_bf, 1, tm=512)
    x2d = _ffn(x2d, ffn_norm[1:2], w_ff1_bf, w_ff2_bf, 1, tm=512, tf=1024)
    return x2d.reshape(batch, seq, D_MODEL)
```

```python
import functools
import math

import jax
import jax.numpy as jnp
from jax import lax
from jax.experimental import pallas as pl
from jax.experimental.pallas import tpu as pltpu

D_MODEL = 2048
MEM_LEN = 256
MEM_HEADS = 4
MEM_HEAD_DIM = 128
MEM_WIDTH = MEM_HEADS * MEM_HEAD_DIM
POOL_WIDTH = D_MODEL - MEM_WIDTH
POOL_WINDOWS = (2, 4, 8, 16)
POOL_GROUP = POOL_WIDTH // len(POOL_WINDOWS)
POOL_HALO = 16
DIFF_HEAD_DIM = 64
DIFF_V_DIM = 128
DIFF_HEADS = POOL_WIDTH // DIFF_V_DIM
DIFF_QK_WIDTH = DIFF_HEADS * DIFF_HEAD_DIM
DIFF_V_WIDTH = DIFF_HEADS * DIFF_V_DIM
VT_ROWS = DIFF_V_DIM + 16
KV_WIDTH = 2 * DIFF_QK_WIDTH + DIFF_V_WIDTH
ROPE_THETA = 500000.0
ROPE_DIM = DIFF_HEAD_DIM // 4
D_FF = 4 * D_MODEL
EPS = 1e-6
NEG_INF = -1e30

LANES = 128
MXU_DIM = 256
VMEM_LIMIT_BYTES = 56 * 1024 * 1024

BF16 = jnp.bfloat16
F32 = jnp.float32


def _resident(shape, index_map):
    return pl.BlockSpec(shape, index_map, pipeline_mode=pl.Buffered(1))


def _rstd(x):
    return lax.rsqrt(jnp.mean(x * x, axis=-1, keepdims=True) + EPS)


def _dot(a, b):
    return jnp.dot(a, b, preferred_element_type=F32)


def _dot_nt(a, b):
    return lax.dot_general(a, b, (((1,), (1,)), ((), ())), preferred_element_type=F32)


def _memkv_kernel(mem_ref, g_ref, w_ref, kg_ref, k_ref, v_ref):
    x = mem_ref[...]
    xn = (x * _rstd(x) * g_ref[0]).astype(BF16)
    kv = _dot(xn, w_ref[0])
    for h in range(MEM_HEADS):
        hs = slice(h * MEM_HEAD_DIM, (h + 1) * MEM_HEAD_DIM)
        kh = kv[:, hs]
        k_ref[0, :, hs] = (kh * _rstd(kh) * kg_ref[0]).astype(BF16)
    v_ref[0] = kv[:, MEM_WIDTH:].astype(BF16)


def _memkv(mem2d, mem_norm, w_mem_kv_bf, mem_k_norm, batch):
    depth = mem_norm.shape[0]
    out_sd = jax.ShapeDtypeStruct((depth, batch * MEM_LEN, MEM_WIDTH), BF16)
    return pl.pallas_call(
        _memkv_kernel,
        grid=(depth, batch),
        in_specs=[
            pl.BlockSpec((MEM_LEN, D_MODEL), lambda l, b: (b, 0)),
            pl.BlockSpec((1, 1, D_MODEL), lambda l, b: (l, 0, 0)),
            pl.BlockSpec((1, D_MODEL, 2 * MEM_WIDTH), lambda l, b: (l, 0, 0)),
            pl.BlockSpec((1, 1, MEM_HEAD_DIM), lambda l, b: (l, 0, 0)),
        ],
        out_specs=[
            pl.BlockSpec((1, MEM_LEN, MEM_WIDTH), lambda l, b: (l, b, 0)),
            pl.BlockSpec((1, MEM_LEN, MEM_WIDTH), lambda l, b: (l, b, 0)),
        ],
        out_shape=[out_sd, out_sd],
        compiler_params=pltpu.CompilerParams(
            dimension_semantics=("arbitrary", "arbitrary"),
            vmem_limit_bytes=VMEM_LIMIT_BYTES),
        name="mem_kv",
    )(mem2d, mem_norm.reshape(depth, 1, D_MODEL), w_mem_kv_bf,
      mem_k_norm.reshape(depth, 1, MEM_HEAD_DIM))


def _mem_attention_head(qh, qg, kh, vh):
    qn = (qh * _rstd(qh) * qg).astype(BF16)
    s = _dot_nt(qn, kh) * (MEM_HEAD_DIM ** -0.5)
    p = jnp.exp(s - jnp.max(s, axis=-1, keepdims=True))
    l = jnp.sum(p, axis=-1, keepdims=True)
    return _dot(p.astype(BF16), vh) / l


def _pool_bands():
    r = jnp.arange(MXU_DIM)[:, None]
    c = jnp.arange(MXU_DIM)[None, :]
    rh = jnp.arange(POOL_HALO)[:, None]
    ch = jnp.arange(POOL_HALO)[None, :]
    band = jnp.stack([((r - c >= 0) & (r - c < w)) for w in POOL_WINDOWS])
    bandh = jnp.stack([(rh + POOL_HALO - ch < w) for w in POOL_WINDOWS])
    return band.astype(BF16), bandh.astype(BF16)


def _layer_a_kernel(x_ref, xh_ref, g_ref, win_ref, band_ref, bandh_ref, poolw_ref, pscale_ref,
                    mk_ref, mv_ref, mqg_ref, wout_ref, o_ref, cat_ref, *, tm):
    i = pl.program_id(1)
    x = x_ref[...]
    rstd = _rstd(x)
    xn = (x * g_ref[...]).astype(BF16)

    xh = xh_ref[...]
    uh = _dot((xh * g_ref[...]).astype(BF16), win_ref[:, :POOL_WIDTH]) * _rstd(xh)
    uh = jnp.where(i == 0, 0.0, uh).astype(BF16)

    row = lax.broadcasted_iota(jnp.int32, (MXU_DIM, 1), 0)
    half_width = 2 * POOL_GROUP

    def pool_epilogue(u_half, half):
        ub = u_half.astype(BF16)
        for gl in range(2):
            g = 2 * half + gl
            w = POOL_WINDOWS[g]
            ls = slice(gl * POOL_GROUP, (gl + 1) * POOL_GROUP)
            cs = slice(g * POOL_GROUP, (g + 1) * POOL_GROUP)
            for sb in range(tm // MXU_DIM):
                r0 = sb * MXU_DIM
                rs = slice(r0, r0 + MXU_DIM)
                main = _dot(band_ref[g], ub[rs, ls])
                halo = uh[:, cs] if sb == 0 else ub[r0 - POOL_HALO:r0, ls]
                top = _dot(bandh_ref[g], halo)
                wsum = jnp.concatenate([main[:POOL_HALO] + top, main[POOL_HALO:]], axis=0)
                t = i * tm + r0 + row
                count = jnp.minimum(t + 1, w).astype(F32)
                pooled = wsum / count - u_half[rs, ls]
                tok = _dot(pooled.astype(BF16), poolw_ref[g]) * pscale_ref[:, cs]
                cat_ref[rs, cs] = tok.astype(BF16)

    def out_partial(c0, c1):
        return _dot(cat_ref[:, c0:c1], wout_ref[c0:c1, :])

    u0 = _dot(xn, win_ref[:, :half_width])
    u1 = _dot(xn, win_ref[:, half_width:POOL_WIDTH])
    pool_epilogue(u0 * rstd, 0)
    uq = _dot(xn, win_ref[:, POOL_WIDTH:]) * rstd
    pool_epilogue(u1 * rstd, 1)
    acc = x + out_partial(0, POOL_WIDTH)
    mk, mv = mk_ref[0], mv_ref[0]
    for h in range(MEM_HEADS):
        hs = slice(h * MEM_HEAD_DIM, (h + 1) * MEM_HEAD_DIM)
        o = _mem_attention_head(uq[:, hs], mqg_ref[...], mk[:, hs], mv[:, hs])
        cat_ref[:, POOL_WIDTH + h * MEM_HEAD_DIM:POOL_WIDTH + (h + 1) * MEM_HEAD_DIM] = o.astype(BF16)
    o_ref[...] = acc + out_partial(POOL_WIDTH, D_MODEL)


def _layer_a(x2d, g, win_bf, wout_bf, layer, poolw_bf, pscale, mk, mv, mqg, batch, seq, tm):
    nt = seq // tm
    wslab = lambda b, i: (layer, 0, 0)
    band, bandh = _pool_bands()
    ngrp = len(POOL_WINDOWS)
    const2 = lambda b, i: (0, 0)
    const3 = lambda b, i: (0, 0, 0)
    halo_blocks = tm // POOL_HALO
    return pl.pallas_call(
        functools.partial(_layer_a_kernel, tm=tm),
        grid=(batch, nt),
        in_specs=[
            pl.BlockSpec((tm, D_MODEL), lambda b, i: (b * nt + i, 0)),
            pl.BlockSpec((POOL_HALO, D_MODEL),
                         lambda b, i: (jnp.maximum((b * nt + i) * halo_blocks - 1, 0), 0)),
            _resident((1, D_MODEL), const2),
            _resident((None, D_MODEL, D_MODEL), wslab),
            _resident((ngrp, MXU_DIM, MXU_DIM), const3),
            _resident((ngrp, POOL_HALO, POOL_HALO), const3),
            _resident((ngrp, POOL_GROUP, POOL_GROUP), const3),
            _resident((1, POOL_WIDTH), const2),
            pl.BlockSpec((1, MEM_LEN, MEM_WIDTH), lambda b, i: (layer, b, 0)),
            pl.BlockSpec((1, MEM_LEN, MEM_WIDTH), lambda b, i: (layer, b, 0)),
            _resident((1, MEM_HEAD_DIM), const2),
            _resident((None, D_MODEL, D_MODEL), wslab),
        ],
        out_specs=pl.BlockSpec((tm, D_MODEL), lambda b, i: (b * nt + i, 0)),
        out_shape=jax.ShapeDtypeStruct(x2d.shape, F32),
        scratch_shapes=[pltpu.VMEM((tm, D_MODEL), BF16)],
        compiler_params=pltpu.CompilerParams(
            dimension_semantics=("arbitrary", "arbitrary"),
            vmem_limit_bytes=VMEM_LIMIT_BYTES),
        name="layer_a_mixer",
    )(x2d, x2d, g, win_bf, band, bandh, poolw_bf, pscale, mk, mv, mqg, wout_bf)


def _ffn_kernel(x_ref, g_ref, w1_ref, w2_ref, o_ref, xg_ref, r2_ref, *, nk):
    k = pl.program_id(1)

    def partial(xg):
        z = jnp.maximum(_dot(xg, w1_ref[...]), 0.0)
        return _dot((z * z).astype(BF16), w2_ref[...])

    @pl.when(k == 0)
    def _():
        x = x_ref[...]
        xg = (x * g_ref[...]).astype(BF16)
        xg_ref[...] = xg
        r2_ref[...] = 1.0 / (jnp.mean(x * x, axis=-1, keepdims=True) + EPS)
        o_ref[...] = partial(xg)

    @pl.when(jnp.logical_and(k > 0, k < nk - 1))
    def _():
        o_ref[...] += partial(xg_ref[...])

    @pl.when(k == nk - 1)
    def _():
        o_ref[...] = x_ref[...] + r2_ref[...] * (o_ref[...] + partial(xg_ref[...]))


def _ffn(x2d, g, w1_bf, w2_bf, layer, tm, tf):
    m = x2d.shape[0]
    nk = D_FF // tf
    assert nk >= 2
    return pl.pallas_call(
        functools.partial(_ffn_kernel, nk=nk),
        grid=(m // tm, nk),
        in_specs=[
            pl.BlockSpec((tm, D_MODEL), lambda i, k: (i, 0)),
            _resident((1, D_MODEL), lambda i, k: (0, 0)),
            pl.BlockSpec((None, D_MODEL, tf), lambda i, k: (layer, 0, k)),
            pl.BlockSpec((None, tf, D_MODEL), lambda i, k: (layer, k, 0)),
        ],
        out_specs=pl.BlockSpec((tm, D_MODEL), lambda i, k: (i, 0)),
        out_shape=jax.ShapeDtypeStruct(x2d.shape, F32),
        scratch_shapes=[pltpu.VMEM((tm, D_MODEL), BF16), pltpu.VMEM((tm, 1), F32)],
        compiler_params=pltpu.CompilerParams(
            dimension_semantics=("arbitrary", "arbitrary"),
            vmem_limit_bytes=VMEM_LIMIT_BYTES),
        name="ffn",
    )(x2d, g, w1_bf, w2_bf)


def _rope_coeffs(seq):
    half = ROPE_DIM // 2
    pos = jnp.arange(seq, dtype=F32)
    inv = ROPE_THETA ** (-(jnp.arange(half, dtype=F32) * 2.0) / ROPE_DIM)
    ang = pos[:, None] * inv[None, :]
    cos, sin = jnp.cos(ang), jnp.sin(ang)
    ones = jnp.ones((seq, DIFF_HEAD_DIM - ROPE_DIM), F32)
    zeros = jnp.zeros((seq, DIFF_HEAD_DIM - ROPE_DIM), F32)
    zh = jnp.zeros((seq, half), F32)
    c0 = jnp.concatenate([cos, cos, ones], axis=-1)
    c_up = jnp.concatenate([-sin, zh, zeros], axis=-1)
    c_dn = jnp.concatenate([zh, sin, zeros], axis=-1)
    rep = LANES // DIFF_HEAD_DIM
    return tuple(jnp.tile(c, (1, rep)) for c in (c0, c_up, c_dn))


def _head_norm_rope(t, gain, bd, c0, c_up, c_dn):
    ss = _dot((t * t).astype(BF16), bd)
    tn = t * lax.rsqrt(ss * (1.0 / DIFF_HEAD_DIM) + EPS) * gain
    outs = []
    half = ROPE_DIM // 2
    for p in range(MXU_DIM // LANES):
        tp = tn[:, p * LANES:(p + 1) * LANES]
        up = pltpu.roll(tp, LANES - half, axis=1)
        dn = pltpu.roll(tp, half, axis=1)
        outs.append(tp * c0 + up * c_up + dn * c_dn)
    return jnp.concatenate(outs, axis=-1)


def _layer_b_proj_kernel(x_ref, gmix_ref, gkv_ref, win_ref, wkv_ref, bd_ref, qg_ref, kg_ref,
                         c0_ref, cup_ref, cdn_ref, mk_ref, mv_ref, mqg_ref,
                         q1t_ref, q2t_ref, k1_ref, k2_ref, vt_ref, mo_ref):
    x = x_ref[...]
    tm = x.shape[0]
    rstd = _rstd(x)
    xn_mix = (x * gmix_ref[...]).astype(BF16)
    xn_kv = (x * gkv_ref[...]).astype(BF16)
    bd = bd_ref[...]
    c0, cup, cdn = c0_ref[...], cup_ref[...], cdn_ref[...]
    qg = qg_ref[...] * (DIFF_HEAD_DIM ** -0.5 * math.log2(math.e))
    kg = kg_ref[...]
    pairs_per_chunk = MXU_DIM // LANES
    pad_rows = VT_ROWS - DIFF_V_DIM
    ones_row = (lax.broadcasted_iota(jnp.int32, (pad_rows, tm), 0) == 0).astype(BF16)
    mk, mv = mk_ref[0], mv_ref[0]

    def q_epilogue(qt_ref, c):
        def run(t):
            qt = _head_norm_rope(t, qg, bd, c0, cup, cdn).T.astype(BF16)
            for p in range(pairs_per_chunk):
                qt_ref[0, pairs_per_chunk * c + p, 0] = qt[p * LANES:(p + 1) * LANES]
        return run

    def k_epilogue(k_ref, c):
        def run(t):
            k_ref[:, c * MXU_DIM:(c + 1) * MXU_DIM] = _head_norm_rope(
                t, kg, bd, c0, cup, cdn).astype(BF16)
        return run

    def v_epilogue(c):
        def run(t):
            for p in range(pairs_per_chunk):
                h = pairs_per_chunk * c + p
                vt_ref[0, h, 0, :DIFF_V_DIM, :] = t[:, p * DIFF_V_DIM:(p + 1) * DIFF_V_DIM].T.astype(BF16)
                vt_ref[0, h, 0, DIFF_V_DIM:, :] = ones_row
        return run

    def mem_epilogue(c):
        def run(t):
            for p in range(pairs_per_chunk):
                h = pairs_per_chunk * c + p
                hs = slice(h * MEM_HEAD_DIM, (h + 1) * MEM_HEAD_DIM)
                o = _mem_attention_head(t[:, p * MEM_HEAD_DIM:(p + 1) * MEM_HEAD_DIM],
                                        mqg_ref[...], mk[:, hs], mv[:, hs])
                mo_ref[:, hs] = o.astype(BF16)
        return run

    chunks = []
    for c in range(DIFF_QK_WIDTH // MXU_DIM):
        chunks += [
            (xn_mix, win_ref, c * MXU_DIM, q_epilogue(q1t_ref, c)),
            (xn_kv, wkv_ref, c * MXU_DIM, k_epilogue(k1_ref, c)),
            (xn_kv, wkv_ref, 2 * DIFF_QK_WIDTH + 2 * c * MXU_DIM, v_epilogue(2 * c)),
            (xn_mix, win_ref, DIFF_QK_WIDTH + c * MXU_DIM, q_epilogue(q2t_ref, c)),
            (xn_kv, wkv_ref, DIFF_QK_WIDTH + c * MXU_DIM, k_epilogue(k2_ref, c)),
            (xn_kv, wkv_ref, 2 * DIFF_QK_WIDTH + (2 * c + 1) * MXU_DIM, v_epilogue(2 * c + 1)),
        ]
    for c in range(MEM_WIDTH // MXU_DIM):
        chunks.append((xn_mix, win_ref, 2 * DIFF_QK_WIDTH + c * MXU_DIM, mem_epilogue(c)))

    def project(chunk):
        xn, w_ref, col, _ = chunk
        return _dot(xn, w_ref[:, col:col + MXU_DIM])

    pending = project(chunks[0])
    for n, chunk in enumerate(chunks):
        ready = pending
        if n + 1 < len(chunks):
            pending = project(chunks[n + 1])
        chunk[3](ready * rstd)


def _layer_b_proj(x2d, gmix, gkv, win_bf, layer, wkv_bf, q_norm, k_norm, mk, mv, mqg,
                  batch, seq, tm, tq):
    nt = seq // tm
    nq = seq // tq
    sub = tq // tm
    npairs = DIFF_QK_WIDTH // LANES
    m = x2d.shape[0]
    c0, cup, cdn = _rope_coeffs(seq)
    lane_head = jnp.arange(MXU_DIM) // DIFF_HEAD_DIM
    bd = (lane_head[:, None] == lane_head[None, :]).astype(BF16)
    qg = jnp.tile(q_norm.reshape(1, DIFF_HEAD_DIM), (1, MXU_DIM // DIFF_HEAD_DIM))
    kg = jnp.tile(k_norm.reshape(1, DIFF_HEAD_DIM), (1, MXU_DIM // DIFF_HEAD_DIM))
    const2 = lambda b, i: (0, 0)
    rows = lambda b, i: (b * nt + i, 0)
    qk_sd = jax.ShapeDtypeStruct((m, DIFF_QK_WIDTH), BF16)
    qt_sd = jax.ShapeDtypeStruct((batch, npairs, nq, LANES, tq), BF16)
    qt_spec = pl.BlockSpec((1, npairs, 1, LANES, tm), lambda b, i: (b, 0, i // sub, 0, i % sub))
    return pl.pallas_call(
        _layer_b_proj_kernel,
        grid=(batch, nt),
        in_specs=[
            pl.BlockSpec((tm, D_MODEL), rows),
            _resident((1, D_MODEL), const2),
            _resident((1, D_MODEL), const2),
            _resident((None, D_MODEL, D_MODEL), lambda b, i: (layer, 0, 0)),
            _resident((D_MODEL, KV_WIDTH), const2),
            _resident((MXU_DIM, MXU_DIM), const2),
            _resident((1, MXU_DIM), const2),
            _resident((1, MXU_DIM), const2),
            pl.BlockSpec((tm, LANES), lambda b, i: (i, 0)),
            pl.BlockSpec((tm, LANES), lambda b, i: (i, 0)),
            pl.BlockSpec((tm, LANES), lambda b, i: (i, 0)),
            pl.BlockSpec((1, MEM_LEN, MEM_WIDTH), lambda b, i: (layer, b, 0)),
            pl.BlockSpec((1, MEM_LEN, MEM_WIDTH), lambda b, i: (layer, b, 0)),
            _resident((1, MEM_HEAD_DIM), const2),
        ],
        out_specs=[
            qt_spec,
            qt_spec,
            pl.BlockSpec((tm, DIFF_QK_WIDTH), rows),
            pl.BlockSpec((tm, DIFF_QK_WIDTH), rows),
            pl.BlockSpec((1, DIFF_HEADS, 1, VT_ROWS, tm),
                         lambda b, i: (b, 0, i // sub, 0, i % sub)),
            pl.BlockSpec((tm, MEM_WIDTH), rows),
        ],
        out_shape=[qt_sd, qt_sd, qk_sd, qk_sd,
                   jax.ShapeDtypeStruct((batch, DIFF_HEADS, nq, VT_ROWS, tq), BF16),
                   jax.ShapeDtypeStruct((m, MEM_WIDTH), BF16)],
        compiler_params=pltpu.CompilerParams(
            dimension_semantics=("arbitrary", "arbitrary"),
            vmem_limit_bytes=VMEM_LIMIT_BYTES),
        name="layer_b_proj",
    )(x2d, gmix, gkv, win_bf, wkv_bf, bd, qg, kg, c0, cup, cdn, mk, mv, mqg)


def _diff_attn_kernel(q1t_ref, q2t_ref, k1_ref, k2_ref, vt_ref, lam_ref, sg_ref, o_ref,
                      s_ref, m_ref, acc_ref, *, tq, nq, pairs, lam_init):
    feat = lax.broadcasted_iota(jnp.int32, (LANES, 1), 0)
    heads_per_pair = LANES // DIFF_HEAD_DIM
    combos = [(pr, hh, mp) for pr in range(pairs) for hh in range(heads_per_pair)
              for mp in range(2)]
    in_head = [(feat >= hh * DIFF_HEAD_DIM) & (feat < (hh + 1) * DIFF_HEAD_DIM)
               for hh in range(heads_per_pair)]
    key_pos = lax.broadcasted_iota(jnp.int32, (tq, tq), 0)
    q_pos = lax.broadcasted_iota(jnp.int32, (tq, tq), 1)

    def scores(i, blk):
        r0 = pl.multiple_of(blk * tq, tq)
        out = []
        for pr, hh, mp in combos:
            qt = (q1t_ref, q2t_ref)[mp][0, pr, i]
            qm = jnp.where(in_head[hh], qt, jnp.zeros_like(qt))
            kblk = (k1_ref, k2_ref)[mp][pl.ds(r0, tq), pr * LANES:(pr + 1) * LANES]
            out.append(_dot(kblk, qm))
        return out

    def store_scores(sts, diagonal):
        for c, st in enumerate(sts):
            s_ref[c] = jnp.where(key_pos <= q_pos, st, NEG_INF) if diagonal else st

    def consume(blk, first):
        for c, (pr, hh, mp) in enumerate(combos):
            st = s_ref[c]
            m_old = jnp.where(first, NEG_INF, m_ref[c])
            m_new = jnp.maximum(m_old, jnp.max(st, axis=0, keepdims=True))
            alpha = jnp.exp2(m_old - m_new)
            p = jnp.exp2(st - m_new).astype(BF16)
            vtblk = vt_ref[0, heads_per_pair * pr + hh, blk]
            acc_ref[c] = alpha * acc_ref[c] + _dot(vtblk, p)
            m_ref[c] = m_new

    def finalize(i):
        lq = lam_ref[...]
        lam = (jnp.exp(jnp.sum(lq[0:1] * lq[1:2], axis=-1, keepdims=True))
               - jnp.exp(jnp.sum(lq[2:3] * lq[3:4], axis=-1, keepdims=True)) + lam_init)
        r0 = pl.multiple_of(i * tq, tq)
        for hh in range(heads_per_pair * pairs):
            a1, a2 = acc_ref[2 * hh], acc_ref[2 * hh + 1]
            ot = (a1[:DIFF_V_DIM] * (1.0 / a1[DIFF_V_DIM:DIFF_V_DIM + 1])
                  - lam * (a2[:DIFF_V_DIM] * (1.0 / a2[DIFF_V_DIM:DIFF_V_DIM + 1])))
            ot = ot * lax.rsqrt(jnp.mean(ot * ot, axis=0, keepdims=True) + EPS)
            o = ot.T * sg_ref[...] * (1.0 - lam_init)
            o_ref[pl.ds(r0, tq), hh * DIFF_V_DIM:(hh + 1) * DIFF_V_DIM] = o.astype(BF16)

    acc_ref[...] = jnp.zeros(acc_ref.shape, F32)
    m_ref[...] = jnp.full(m_ref.shape, NEG_INF, F32)
    store_scores(scores(0, 0), True)

    def body(t, carry):
        i, pos = carry
        last = pos == i
        ni = jnp.where(last, i + 1, i)
        npos = jnp.where(last, 0, pos + 1)
        blk = jnp.where(pos == 0, i, pos - 1)
        nblk = jnp.where(npos == 0, ni, npos - 1)

        @pl.when(last)
        def _():
            nxt = scores(ni, nblk)
            consume(blk, pos == 0)
            store_scores(nxt, True)
            finalize(i)

        @pl.when(jnp.logical_not(last))
        def _():
            nxt = scores(ni, nblk)
            consume(blk, pos == 0)
            store_scores(nxt, False)

        return ni, npos

    ntasks = nq * (nq + 1) // 2
    lax.fori_loop(0, ntasks - 1, body, (jnp.int32(0), jnp.int32(0)))
    consume(max(nq - 2, 0), nq == 1)
    finalize(nq - 1)


def _diff_attention(q1t, q2t, k1, k2, vt, diff_lambda, subln, batch, seq, tq, pairs, lam_init):
    nq = seq // tq
    m = k1.shape[0]
    npairs = DIFF_QK_WIDTH // LANES
    heads_per_step = pairs * (LANES // DIFF_HEAD_DIM)
    vw = heads_per_step * DIFF_V_DIM
    ncombo = 2 * heads_per_step
    qspec = pl.BlockSpec((1, pairs, nq, LANES, tq), lambda b, hp: (b, hp, 0, 0, 0))
    kspec = pl.BlockSpec((seq, pairs * LANES), lambda b, hp: (b, hp))
    return pl.pallas_call(
        functools.partial(_diff_attn_kernel, tq=tq, nq=nq, pairs=pairs, lam_init=lam_init),
        grid=(batch, npairs // pairs),
        in_specs=[
            qspec, qspec, kspec, kspec,
            pl.BlockSpec((1, heads_per_step, nq, VT_ROWS, tq), lambda b, hp: (b, hp, 0, 0, 0)),
            _resident((4, DIFF_HEAD_DIM), lambda b, hp: (0, 0)),
            _resident((1, DIFF_V_DIM), lambda b, hp: (0, 0)),
        ],
        out_specs=pl.BlockSpec((seq, vw), lambda b, hp: (b, hp)),
        out_shape=jax.ShapeDtypeStruct((m, DIFF_V_WIDTH), BF16),
        scratch_shapes=[
            pltpu.VMEM((ncombo, tq, tq), F32),
            pltpu.VMEM((ncombo, 1, tq), F32),
            pltpu.VMEM((ncombo, VT_ROWS, tq), F32),
        ],
        compiler_params=pltpu.CompilerParams(
            dimension_semantics=("arbitrary", "arbitrary"),
            vmem_limit_bytes=VMEM_LIMIT_BYTES),
        name="diff_attention",
    )(q1t, q2t, k1, k2, vt, diff_lambda, subln)


def _out_proj_kernel(tok_ref, mo_ref, x_ref, w_ref, o_ref):
    o_ref[...] = (x_ref[...] + _dot(tok_ref[...], w_ref[:DIFF_V_WIDTH, :])
                  + _dot(mo_ref[...], w_ref[DIFF_V_WIDTH:, :]))


def _out_proj(tok, mo, x2d, wout_bf, layer, tm):
    m = x2d.shape[0]
    rows = lambda i: (i, 0)
    return pl.pallas_call(
        _out_proj_kernel,
        grid=(m // tm,),
        in_specs=[
            pl.BlockSpec((tm, DIFF_V_WIDTH), rows),
            pl.BlockSpec((tm, MEM_WIDTH), rows),
            pl.BlockSpec((tm, D_MODEL), rows),
            _resident((None, D_MODEL, D_MODEL), lambda i: (layer, 0, 0)),
        ],
        out_specs=pl.BlockSpec((tm, D_MODEL), rows),
        out_shape=jax.ShapeDtypeStruct(x2d.shape, F32),
        compiler_params=pltpu.CompilerParams(
            dimension_semantics=("arbitrary",),
            vmem_limit_bytes=VMEM_LIMIT_BYTES),
        name="out_proj",
    )(tok, mo, x2d, wout_bf)


def kernel(x, mem, mix_norm, w_in, w_out, mem_norm, w_mem_kv, mem_q_norm, mem_k_norm,
           ffn_norm, w_ff1, w_ff2, pool_w, pool_scale, kv_norm, w_kv, k_norm, q_norm,
           diff_lambda, subln_norm):
    batch, seq, _ = x.shape
    m = batch * seq
    x2d = x.reshape(m, D_MODEL)
    mem2d = mem.reshape(batch * MEM_LEN, D_MODEL)

    w_in_bf, w_out_bf = w_in.astype(BF16), w_out.astype(BF16)
    w_ff1_bf, w_ff2_bf = w_ff1.astype(BF16), w_ff2.astype(BF16)

    mk, mv = _memkv(mem2d, mem_norm, w_mem_kv.astype(BF16), mem_k_norm, batch)

    x2d = _layer_a(x2d, mix_norm[0:1], w_in_bf, w_out_bf, 0, pool_w[0].astype(BF16),
                   pool_scale[0:1], mk, mv, mem_q_norm[0:1], batch, seq, tm=512)
    x2d = _ffn(x2d, ffn_norm[0:1], w_ff1_bf, w_ff2_bf, 0, tm=512, tf=1024)

    lam_init = 0.8 - 0.6 * math.exp(-0.3 * 1)
    tq = 256
    q1t, q2t, k1, k2, vt, mo = _layer_b_proj(
        x2d, mix_norm[1:2], kv_norm.reshape(1, D_MODEL), w_in_bf, 1, w_kv.astype(BF16),
        q_norm[0], k_norm, mk, mv, mem_q_norm[1:2], batch, seq, tm=256, tq=tq)
    tok = _diff_attention(q1t, q2t, k1, k2, vt, diff_lambda[0], subln_norm[0:1],
                          batch, seq, tq=tq, pairs=2, lam_init=lam_init)
    x2d = _out_proj(tok, mo, x2d, w_out_bf, 1, tm=512)
    x2d = _ffn(x2d, ffn_norm[1:2], w_ff1_bf, w_ff2_bf, 1, tm=512, tf=1024)
    return x2d.reshape(batch, seq, D_MODEL)
```

```python
import functools
import math

import jax
import jax.numpy as jnp
from jax import lax
from jax.experimental import pallas as pl
from jax.experimental.pallas import tpu as pltpu

D_MODEL = 2048
MEM_LEN = 256
MEM_HEADS = 4
MEM_HEAD_DIM = 128
MEM_WIDTH = MEM_HEADS * MEM_HEAD_DIM
POOL_WIDTH = D_MODEL - MEM_WIDTH
POOL_WINDOWS = (2, 4, 8, 16)
POOL_GROUP = POOL_WIDTH // len(POOL_WINDOWS)
POOL_HALO = 16
DIFF_HEAD_DIM = 64
DIFF_V_DIM = 128
DIFF_HEADS = POOL_WIDTH // DIFF_V_DIM
DIFF_QK_WIDTH = DIFF_HEADS * DIFF_HEAD_DIM
DIFF_V_WIDTH = DIFF_HEADS * DIFF_V_DIM
VT_ROWS = DIFF_V_DIM + 16
KV_WIDTH = 2 * DIFF_QK_WIDTH + DIFF_V_WIDTH
ROPE_THETA = 500000.0
ROPE_DIM = DIFF_HEAD_DIM // 4
D_FF = 4 * D_MODEL
EPS = 1e-6
NEG_INF = -1e30

LANES = 128
MXU_DIM = 256
VMEM_LIMIT_BYTES = 56 * 1024 * 1024

BF16 = jnp.bfloat16
F32 = jnp.float32


def _resident(shape, index_map):
    return pl.BlockSpec(shape, index_map, pipeline_mode=pl.Buffered(1))


def _rstd(x):
    return lax.rsqrt(jnp.mean(x * x, axis=-1, keepdims=True) + EPS)


def _dot(a, b):
    return jnp.dot(a, b, preferred_element_type=F32)


def _dot_nt(a, b):
    return lax.dot_general(a, b, (((1,), (1,)), ((), ())), preferred_element_type=F32)


def _memkv_kernel(mem_ref, g_ref, w_ref, kg_ref, k_ref, v_ref):
    x = mem_ref[...]
    xn = (x * _rstd(x) * g_ref[0]).astype(BF16)
    kv = _dot(xn, w_ref[0])
    for h in range(MEM_HEADS):
        hs = slice(h * MEM_HEAD_DIM, (h + 1) * MEM_HEAD_DIM)
        kh = kv[:, hs]
        k_ref[0, :, hs] = (kh * _rstd(kh) * kg_ref[0]).astype(BF16)
    v_ref[0] = kv[:, MEM_WIDTH:].astype(BF16)


def _memkv(mem2d, mem_norm, w_mem_kv_bf, mem_k_norm, batch):
    depth = mem_norm.shape[0]
    out_sd = jax.ShapeDtypeStruct((depth, batch * MEM_LEN, MEM_WIDTH), BF16)
    return pl.pallas_call(
        _memkv_kernel,
        grid=(depth, batch),
        in_specs=[
            pl.BlockSpec((MEM_LEN, D_MODEL), lambda l, b: (b, 0)),
            pl.BlockSpec((1, 1, D_MODEL), lambda l, b: (l, 0, 0)),
            pl.BlockSpec((1, D_MODEL, 2 * MEM_WIDTH), lambda l, b: (l, 0, 0)),
            pl.BlockSpec((1, 1, MEM_HEAD_DIM), lambda l, b: (l, 0, 0)),
        ],
        out_specs=[
            pl.BlockSpec((1, MEM_LEN, MEM_WIDTH), lambda l, b: (l, b, 0)),
            pl.BlockSpec((1, MEM_LEN, MEM_WIDTH), lambda l, b: (l, b, 0)),
        ],
        out_shape=[out_sd, out_sd],
        compiler_params=pltpu.CompilerParams(
            dimension_semantics=("arbitrary", "arbitrary"),
            vmem_limit_bytes=VMEM_LIMIT_BYTES),
        name="mem_kv",
    )(mem2d, mem_norm.reshape(depth, 1, D_MODEL), w_mem_kv_bf,
      mem_k_norm.reshape(depth, 1, MEM_HEAD_DIM))


def _mem_attention_head(qh, qg, kh, vh):
    qn = (qh * _rstd(qh) * qg).astype(BF16)
    s = _dot_nt(qn, kh) * (MEM_HEAD_DIM ** -0.5)
    p = jnp.exp(s - jnp.max(s, axis=-1, keepdims=True))
    l = jnp.sum(p, axis=-1, keepdims=True)
    return _dot(p.astype(BF16), vh) / l


def _pool_bands():
    r = jnp.arange(MXU_DIM)[:, None]
    c = jnp.arange(MXU_DIM)[None, :]
    rh = jnp.arange(POOL_HALO)[:, None]
    ch = jnp.arange(POOL_HALO)[None, :]
    band = jnp.stack([((r - c >= 0) & (r - c < w)) for w in POOL_WINDOWS])
    bandh = jnp.stack([(rh + POOL_HALO - ch < w) for w in POOL_WINDOWS])
    return band.astype(BF16), bandh.astype(BF16)


def _ffn_cast_specs(layer, nsteps, step):
    chunk = D_FF // nsteps
    in_specs = [pl.BlockSpec((None, D_MODEL, chunk), lambda *ix: (layer, 0, step(*ix))),
                pl.BlockSpec((None, chunk, D_MODEL), lambda *ix: (layer, step(*ix), 0))]
    out_specs = [pl.BlockSpec((D_MODEL, chunk), lambda *ix: (0, step(*ix))),
                 pl.BlockSpec((chunk, D_MODEL), lambda *ix: (step(*ix), 0))]
    out_shapes = [jax.ShapeDtypeStruct((D_MODEL, D_FF), BF16),
                  jax.ShapeDtypeStruct((D_FF, D_MODEL), BF16)]
    return in_specs, out_specs, out_shapes


def _layer_a_kernel(x_ref, xh_ref, g_ref, win_ref, band_ref, bandh_ref, poolw_ref, pscale_ref,
                    mk_ref, mv_ref, mqg_ref, wout_ref, w1_ref, w2_ref,
                    o_ref, w1bf_ref, w2bf_ref, cat_ref, *, tm):
    w1bf_ref[...] = w1_ref[...].astype(BF16)
    w2bf_ref[...] = w2_ref[...].astype(BF16)
    i = pl.program_id(1)
    x = x_ref[...]
    rstd = _rstd(x)
    xn = (x * g_ref[...]).astype(BF16)

    xh = xh_ref[...]
    uh = _dot((xh * g_ref[...]).astype(BF16), win_ref[:, :POOL_WIDTH]) * _rstd(xh)
    uh = jnp.where(i == 0, 0.0, uh).astype(BF16)

    row = lax.broadcasted_iota(jnp.int32, (MXU_DIM, 1), 0)
    half_width = 2 * POOL_GROUP

    def pool_epilogue(u_half, half):
        ub = u_half.astype(BF16)
        for gl in range(2):
            g = 2 * half + gl
            w = POOL_WINDOWS[g]
            ls = slice(gl * POOL_GROUP, (gl + 1) * POOL_GROUP)
            cs = slice(g * POOL_GROUP, (g + 1) * POOL_GROUP)
            for sb in range(tm // MXU_DIM):
                r0 = sb * MXU_DIM
                rs = slice(r0, r0 + MXU_DIM)
                main = _dot(band_ref[g], ub[rs, ls])
                halo = uh[:, cs] if sb == 0 else ub[r0 - POOL_HALO:r0, ls]
                top = _dot(bandh_ref[g], halo)
                wsum = jnp.concatenate([main[:POOL_HALO] + top, main[POOL_HALO:]], axis=0)
                t = i * tm + r0 + row
                count = jnp.minimum(t + 1, w).astype(F32)
                pooled = wsum / count - u_half[rs, ls]
                tok = _dot(pooled.astype(BF16), poolw_ref[g]) * pscale_ref[:, cs]
                cat_ref[rs, cs] = tok.astype(BF16)

    def out_partial(c0, c1):
        return _dot(cat_ref[:, c0:c1], wout_ref[c0:c1, :])

    u0 = _dot(xn, win_ref[:, :half_width])
    u1 = _dot(xn, win_ref[:, half_width:POOL_WIDTH])
    pool_epilogue(u0 * rstd, 0)
    uq = _dot(xn, win_ref[:, POOL_WIDTH:]) * rstd
    pool_epilogue(u1 * rstd, 1)
    acc = x + out_partial(0, POOL_WIDTH)
    mk, mv = mk_ref[0], mv_ref[0]
    for h in range(MEM_HEADS):
        hs = slice(h * MEM_HEAD_DIM, (h + 1) * MEM_HEAD_DIM)
        o = _mem_attention_head(uq[:, hs], mqg_ref[...], mk[:, hs], mv[:, hs])
        cat_ref[:, POOL_WIDTH + h * MEM_HEAD_DIM:POOL_WIDTH + (h + 1) * MEM_HEAD_DIM] = o.astype(BF16)
    o_ref[...] = acc + out_partial(POOL_WIDTH, D_MODEL)


def _layer_a(x2d, g, win_bf, wout_bf, layer, poolw_bf, pscale, mk, mv, mqg, w_ff1, w_ff2,
             batch, seq, tm):
    nt = seq // tm
    wslab = lambda b, i: (layer, 0, 0)
    cast_in, cast_out, cast_shapes = _ffn_cast_specs(layer, batch * nt, lambda b, i: b * nt + i)
    band, bandh = _pool_bands()
    ngrp = len(POOL_WINDOWS)
    const2 = lambda b, i: (0, 0)
    const3 = lambda b, i: (0, 0, 0)
    halo_blocks = tm // POOL_HALO
    return pl.pallas_call(
        functools.partial(_layer_a_kernel, tm=tm),
        grid=(batch, nt),
        in_specs=[
            pl.BlockSpec((tm, D_MODEL), lambda b, i: (b * nt + i, 0)),
            pl.BlockSpec((POOL_HALO, D_MODEL),
                         lambda b, i: (jnp.maximum((b * nt + i) * halo_blocks - 1, 0), 0)),
            _resident((1, D_MODEL), const2),
            _resident((None, D_MODEL, D_MODEL), wslab),
            _resident((ngrp, MXU_DIM, MXU_DIM), const3),
            _resident((ngrp, POOL_HALO, POOL_HALO), const3),
            _resident((ngrp, POOL_GROUP, POOL_GROUP), const3),
            _resident((1, POOL_WIDTH), const2),
            pl.BlockSpec((1, MEM_LEN, MEM_WIDTH), lambda b, i: (layer, b, 0)),
            pl.BlockSpec((1, MEM_LEN, MEM_WIDTH), lambda b, i: (layer, b, 0)),
            _resident((1, MEM_HEAD_DIM), const2),
            _resident((None, D_MODEL, D_MODEL), wslab),
        ] + cast_in,
        out_specs=[pl.BlockSpec((tm, D_MODEL), lambda b, i: (b * nt + i, 0))] + cast_out,
        out_shape=[jax.ShapeDtypeStruct(x2d.shape, F32)] + cast_shapes,
        scratch_shapes=[pltpu.VMEM((tm, D_MODEL), BF16)],
        compiler_params=pltpu.CompilerParams(
            dimension_semantics=("arbitrary", "arbitrary"),
            vmem_limit_bytes=VMEM_LIMIT_BYTES),
        name="layer_a_mixer",
    )(x2d, x2d, g, win_bf, band, bandh, poolw_bf, pscale, mk, mv, mqg, wout_bf, w_ff1, w_ff2)


def _ffn_kernel(x_ref, g_ref, w1_ref, w2_ref, o_ref, xg_ref, r2_ref, *, nk):
    k = pl.program_id(1)

    def partial(xg):
        z = jnp.maximum(_dot(xg, w1_ref[...]), 0.0)
        return _dot((z * z).astype(BF16), w2_ref[...])

    @pl.when(k == 0)
    def _():
        x = x_ref[...]
        xg = (x * g_ref[...]).astype(BF16)
        xg_ref[...] = xg
        r2_ref[...] = 1.0 / (jnp.mean(x * x, axis=-1, keepdims=True) + EPS)
        o_ref[...] = partial(xg)

    @pl.when(jnp.logical_and(k > 0, k < nk - 1))
    def _():
        o_ref[...] += partial(xg_ref[...])

    @pl.when(k == nk - 1)
    def _():
        o_ref[...] = x_ref[...] + r2_ref[...] * (o_ref[...] + partial(xg_ref[...]))


def _ffn(x2d, g, w1_bf, w2_bf, tm, tf):
    m = x2d.shape[0]
    nk = D_FF // tf
    assert nk >= 2
    return pl.pallas_call(
        functools.partial(_ffn_kernel, nk=nk),
        grid=(m // tm, nk),
        in_specs=[
            pl.BlockSpec((tm, D_MODEL), lambda i, k: (i, 0)),
            _resident((1, D_MODEL), lambda i, k: (0, 0)),
            pl.BlockSpec((D_MODEL, tf), lambda i, k: (0, k)),
            pl.BlockSpec((tf, D_MODEL), lambda i, k: (k, 0)),
        ],
        out_specs=pl.BlockSpec((tm, D_MODEL), lambda i, k: (i, 0)),
        out_shape=jax.ShapeDtypeStruct(x2d.shape, F32),
        scratch_shapes=[pltpu.VMEM((tm, D_MODEL), BF16), pltpu.VMEM((tm, 1), F32)],
        compiler_params=pltpu.CompilerParams(
            dimension_semantics=("arbitrary", "arbitrary"),
            vmem_limit_bytes=VMEM_LIMIT_BYTES),
        name="ffn",
    )(x2d, g, w1_bf, w2_bf)


def _rope_coeffs(seq):
    half = ROPE_DIM // 2
    pos = jnp.arange(seq, dtype=F32)
    inv = ROPE_THETA ** (-(jnp.arange(half, dtype=F32) * 2.0) / ROPE_DIM)
    ang = pos[:, None] * inv[None, :]
    cos, sin = jnp.cos(ang), jnp.sin(ang)
    ones = jnp.ones((seq, DIFF_HEAD_DIM - ROPE_DIM), F32)
    zeros = jnp.zeros((seq, DIFF_HEAD_DIM - ROPE_DIM), F32)
    zh = jnp.zeros((seq, half), F32)
    c0 = jnp.concatenate([cos, cos, ones], axis=-1)
    c_up = jnp.concatenate([-sin, zh, zeros], axis=-1)
    c_dn = jnp.concatenate([zh, sin, zeros], axis=-1)
    rep = LANES // DIFF_HEAD_DIM
    return tuple(jnp.tile(c, (1, rep)) for c in (c0, c_up, c_dn))


def _head_norm_rope(t, gain, bd, c0, c_up, c_dn):
    ss = _dot((t * t).astype(BF16), bd)
    tn = t * lax.rsqrt(ss * (1.0 / DIFF_HEAD_DIM) + EPS) * gain
    outs = []
    half = ROPE_DIM // 2
    for p in range(MXU_DIM // LANES):
        tp = tn[:, p * LANES:(p + 1) * LANES]
        up = pltpu.roll(tp, LANES - half, axis=1)
        dn = pltpu.roll(tp, half, axis=1)
        outs.append(tp * c0 + up * c_up + dn * c_dn)
    return jnp.concatenate(outs, axis=-1)


def _layer_b_proj_kernel(x_ref, gmix_ref, gkv_ref, win_ref, wkv_ref, bd_ref, qg_ref, kg_ref,
                         c0_ref, cup_ref, cdn_ref, mk_ref, mv_ref, mqg_ref,
                         q1t_ref, q2t_ref, k1_ref, k2_ref, vt_ref, mo_ref):
    x = x_ref[...]
    tm = x.shape[0]
    rstd = _rstd(x)
    xn_mix = (x * gmix_ref[...]).astype(BF16)
    xn_kv = (x * gkv_ref[...]).astype(BF16)
    bd = bd_ref[...]
    c0, cup, cdn = c0_ref[...], cup_ref[...], cdn_ref[...]
    qg = qg_ref[...] * (DIFF_HEAD_DIM ** -0.5 * math.log2(math.e))
    kg = kg_ref[...]
    pairs_per_chunk = MXU_DIM // LANES
    pad_rows = VT_ROWS - DIFF_V_DIM
    ones_row = (lax.broadcasted_iota(jnp.int32, (pad_rows, tm), 0) == 0).astype(BF16)
    mk, mv = mk_ref[0], mv_ref[0]

    def q_epilogue(qt_ref, c):
        def run(t):
            qt = _head_norm_rope(t, qg, bd, c0, cup, cdn).T.astype(BF16)
            for p in range(pairs_per_chunk):
                qt_ref[0, pairs_per_chunk * c + p, 0] = qt[p * LANES:(p + 1) * LANES]
        return run

    def k_epilogue(k_ref, c):
        def run(t):
            k_ref[:, c * MXU_DIM:(c + 1) * MXU_DIM] = _head_norm_rope(
                t, kg, bd, c0, cup, cdn).astype(BF16)
        return run

    def v_epilogue(c):
        def run(t):
            for p in range(pairs_per_chunk):
                h = pairs_per_chunk * c + p
                vt_ref[0, h, 0, :DIFF_V_DIM, :] = t[:, p * DIFF_V_DIM:(p + 1) * DIFF_V_DIM].T.astype(BF16)
                vt_ref[0, h, 0, DIFF_V_DIM:, :] = ones_row
        return run

    def mem_epilogue(c):
        def run(t):
            for p in range(pairs_per_chunk):
                h = pairs_per_chunk * c + p
                hs = slice(h * MEM_HEAD_DIM, (h + 1) * MEM_HEAD_DIM)
                o = _mem_attention_head(t[:, p * MEM_HEAD_DIM:(p + 1) * MEM_HEAD_DIM],
                                        mqg_ref[...], mk[:, hs], mv[:, hs])
                mo_ref[:, hs] = o.astype(BF16)
        return run

    chunks = []
    for c in range(DIFF_QK_WIDTH // MXU_DIM):
        chunks += [
            (xn_mix, win_ref, c * MXU_DIM, q_epilogue(q1t_ref, c)),
            (xn_kv, wkv_ref, c * MXU_DIM, k_epilogue(k1_ref, c)),
            (xn_kv, wkv_ref, 2 * DIFF_QK_WIDTH + 2 * c * MXU_DIM, v_epilogue(2 * c)),
            (xn_mix, win_ref, DIFF_QK_WIDTH + c * MXU_DIM, q_epilogue(q2t_ref, c)),
            (xn_kv, wkv_ref, DIFF_QK_WIDTH + c * MXU_DIM, k_epilogue(k2_ref, c)),
            (xn_kv, wkv_ref, 2 * DIFF_QK_WIDTH + (2 * c + 1) * MXU_DIM, v_epilogue(2 * c + 1)),
        ]
    for c in range(MEM_WIDTH // MXU_DIM):
        chunks.append((xn_mix, win_ref, 2 * DIFF_QK_WIDTH + c * MXU_DIM, mem_epilogue(c)))

    def project(chunk):
        xn, w_ref, col, _ = chunk
        return _dot(xn, w_ref[:, col:col + MXU_DIM])

    pending = project(chunks[0])
    for n, chunk in enumerate(chunks):
        ready = pending
        if n + 1 < len(chunks):
            pending = project(chunks[n + 1])
        chunk[3](ready * rstd)


def _layer_b_proj(x2d, gmix, gkv, win_bf, layer, wkv_bf, q_norm, k_norm, mk, mv, mqg,
                  batch, seq, tm, tq):
    nt = seq // tm
    nq = seq // tq
    sub = tq // tm
    npairs = DIFF_QK_WIDTH // LANES
    m = x2d.shape[0]
    c0, cup, cdn = _rope_coeffs(seq)
    lane_head = jnp.arange(MXU_DIM) // DIFF_HEAD_DIM
    bd = (lane_head[:, None] == lane_head[None, :]).astype(BF16)
    qg = jnp.tile(q_norm.reshape(1, DIFF_HEAD_DIM), (1, MXU_DIM // DIFF_HEAD_DIM))
    kg = jnp.tile(k_norm.reshape(1, DIFF_HEAD_DIM), (1, MXU_DIM // DIFF_HEAD_DIM))
    const2 = lambda b, i: (0, 0)
    rows = lambda b, i: (b * nt + i, 0)
    qk_sd = jax.ShapeDtypeStruct((m, DIFF_QK_WIDTH), BF16)
    qt_sd = jax.ShapeDtypeStruct((batch, npairs, nq, LANES, tq), BF16)
    qt_spec = pl.BlockSpec((1, npairs, 1, LANES, tm), lambda b, i: (b, 0, i // sub, 0, i % sub))
    return pl.pallas_call(
        _layer_b_proj_kernel,
        grid=(batch, nt),
        in_specs=[
            pl.BlockSpec((tm, D_MODEL), rows),
            _resident((1, D_MODEL), const2),
            _resident((1, D_MODEL), const2),
            _resident((None, D_MODEL, D_MODEL), lambda b, i: (layer, 0, 0)),
            _resident((D_MODEL, KV_WIDTH), const2),
            _resident((MXU_DIM, MXU_DIM), const2),
            _resident((1, MXU_DIM), const2),
            _resident((1, MXU_DIM), const2),
            pl.BlockSpec((tm, LANES), lambda b, i: (i, 0)),
            pl.BlockSpec((tm, LANES), lambda b, i: (i, 0)),
            pl.BlockSpec((tm, LANES), lambda b, i: (i, 0)),
            pl.BlockSpec((1, MEM_LEN, MEM_WIDTH), lambda b, i: (layer, b, 0)),
            pl.BlockSpec((1, MEM_LEN, MEM_WIDTH), lambda b, i: (layer, b, 0)),
            _resident((1, MEM_HEAD_DIM), const2),
        ],
        out_specs=[
            qt_spec,
            qt_spec,
            pl.BlockSpec((tm, DIFF_QK_WIDTH), rows),
            pl.BlockSpec((tm, DIFF_QK_WIDTH), rows),
            pl.BlockSpec((1, DIFF_HEADS, 1, VT_ROWS, tm),
                         lambda b, i: (b, 0, i // sub, 0, i % sub)),
            pl.BlockSpec((tm, MEM_WIDTH), rows),
        ],
        out_shape=[qt_sd, qt_sd, qk_sd, qk_sd,
                   jax.ShapeDtypeStruct((batch, DIFF_HEADS, nq, VT_ROWS, tq), BF16),
                   jax.ShapeDtypeStruct((m, MEM_WIDTH), BF16)],
        compiler_params=pltpu.CompilerParams(
            dimension_semantics=("arbitrary", "arbitrary"),
            vmem_limit_bytes=VMEM_LIMIT_BYTES),
        name="layer_b_proj",
    )(x2d, gmix, gkv, win_bf, wkv_bf, bd, qg, kg, c0, cup, cdn, mk, mv, mqg)


def _diff_attn_kernel(q1t_ref, q2t_ref, k1_ref, k2_ref, vt_ref, lam_ref, sg_ref, o_ref,
                      s_ref, m_ref, acc_ref, *, tq, nq, pairs, lam_init):
    feat = lax.broadcasted_iota(jnp.int32, (LANES, 1), 0)
    heads_per_pair = LANES // DIFF_HEAD_DIM
    combos = [(pr, hh, mp) for pr in range(pairs) for hh in range(heads_per_pair)
              for mp in range(2)]
    in_head = [(feat >= hh * DIFF_HEAD_DIM) & (feat < (hh + 1) * DIFF_HEAD_DIM)
               for hh in range(heads_per_pair)]
    key_pos = lax.broadcasted_iota(jnp.int32, (tq, tq), 0)
    q_pos = lax.broadcasted_iota(jnp.int32, (tq, tq), 1)

    def scores(i, blk):
        r0 = pl.multiple_of(blk * tq, tq)
        out = []
        for pr, hh, mp in combos:
            qt = (q1t_ref, q2t_ref)[mp][0, pr, i]
            qm = jnp.where(in_head[hh], qt, jnp.zeros_like(qt))
            kblk = (k1_ref, k2_ref)[mp][pl.ds(r0, tq), pr * LANES:(pr + 1) * LANES]
            out.append(_dot(kblk, qm))
        return out

    def store_scores(sts, diagonal):
        for c, st in enumerate(sts):
            s_ref[c] = jnp.where(key_pos <= q_pos, st, NEG_INF) if diagonal else st

    def consume(blk, first):
        for c, (pr, hh, mp) in enumerate(combos):
            st = s_ref[c]
            m_old = jnp.where(first, NEG_INF, m_ref[c])
            m_new = jnp.maximum(m_old, jnp.max(st, axis=0, keepdims=True))
            alpha = jnp.exp2(m_old - m_new)
            p = jnp.exp2(st - m_new).astype(BF16)
            vtblk = vt_ref[0, heads_per_pair * pr + hh, blk]
            acc_ref[c] = alpha * acc_ref[c] + _dot(vtblk, p)
            m_ref[c] = m_new

    def finalize(i):
        lq = lam_ref[...]
        lam = (jnp.exp(jnp.sum(lq[0:1] * lq[1:2], axis=-1, keepdims=True))
               - jnp.exp(jnp.sum(lq[2:3] * lq[3:4], axis=-1, keepdims=True)) + lam_init)
        r0 = pl.multiple_of(i * tq, tq)
        for hh in range(heads_per_pair * pairs):
            a1, a2 = acc_ref[2 * hh], acc_ref[2 * hh + 1]
            ot = (a1[:DIFF_V_DIM] * (1.0 / a1[DIFF_V_DIM:DIFF_V_DIM + 1])
                  - lam * (a2[:DIFF_V_DIM] * (1.0 / a2[DIFF_V_DIM:DIFF_V_DIM + 1])))
            ot = ot * lax.rsqrt(jnp.mean(ot * ot, axis=0, keepdims=True) + EPS)
            o = ot.T * sg_ref[...] * (1.0 - lam_init)
            o_ref[pl.ds(r0, tq), hh * DIFF_V_DIM:(hh + 1) * DIFF_V_DIM] = o.astype(BF16)

    acc_ref[...] = jnp.zeros(acc_ref.shape, F32)
    m_ref[...] = jnp.full(m_ref.shape, NEG_INF, F32)
    store_scores(scores(0, 0), True)

    def body(t, carry):
        i, pos = carry
        last = pos == i
        ni = jnp.where(last, i + 1, i)
        npos = jnp.where(last, 0, pos + 1)
        blk = jnp.where(pos == 0, i, pos - 1)
        nblk = jnp.where(npos == 0, ni, npos - 1)

        @pl.when(last)
        def _():
            nxt = scores(ni, nblk)
            consume(blk, pos == 0)
            store_scores(nxt, True)
            finalize(i)

        @pl.when(jnp.logical_not(last))
        def _():
            nxt = scores(ni, nblk)
            consume(blk, pos == 0)
            store_scores(nxt, False)

        return ni, npos

    ntasks = nq * (nq + 1) // 2
    lax.fori_loop(0, ntasks - 1, body, (jnp.int32(0), jnp.int32(0)))
    consume(max(nq - 2, 0), nq == 1)
    finalize(nq - 1)


def _diff_attention(q1t, q2t, k1, k2, vt, diff_lambda, subln, batch, seq, tq, pairs, lam_init):
    nq = seq // tq
    m = k1.shape[0]
    npairs = DIFF_QK_WIDTH // LANES
    heads_per_step = pairs * (LANES // DIFF_HEAD_DIM)
    vw = heads_per_step * DIFF_V_DIM
    ncombo = 2 * heads_per_step
    qspec = pl.BlockSpec((1, pairs, nq, LANES, tq), lambda b, hp: (b, hp, 0, 0, 0))
    kspec = pl.BlockSpec((seq, pairs * LANES), lambda b, hp: (b, hp))
    return pl.pallas_call(
        functools.partial(_diff_attn_kernel, tq=tq, nq=nq, pairs=pairs, lam_init=lam_init),
        grid=(batch, npairs // pairs),
        in_specs=[
            qspec, qspec, kspec, kspec,
            pl.BlockSpec((1, heads_per_step, nq, VT_ROWS, tq), lambda b, hp: (b, hp, 0, 0, 0)),
            _resident((4, DIFF_HEAD_DIM), lambda b, hp: (0, 0)),
            _resident((1, DIFF_V_DIM), lambda b, hp: (0, 0)),
        ],
        out_specs=pl.BlockSpec((seq, vw), lambda b, hp: (b, hp)),
        out_shape=jax.ShapeDtypeStruct((m, DIFF_V_WIDTH), BF16),
        scratch_shapes=[
            pltpu.VMEM((ncombo, tq, tq), F32),
            pltpu.VMEM((ncombo, 1, tq), F32),
            pltpu.VMEM((ncombo, VT_ROWS, tq), F32),
        ],
        compiler_params=pltpu.CompilerParams(
            dimension_semantics=("arbitrary", "arbitrary"),
            vmem_limit_bytes=VMEM_LIMIT_BYTES),
        name="diff_attention",
    )(q1t, q2t, k1, k2, vt, diff_lambda, subln)


def _out_proj_kernel(tok_ref, mo_ref, x_ref, w_ref, w1_ref, w2_ref, o_ref, w1bf_ref, w2bf_ref):
    w1bf_ref[...] = w1_ref[...].astype(BF16)
    w2bf_ref[...] = w2_ref[...].astype(BF16)
    o_ref[...] = (x_ref[...] + _dot(tok_ref[...], w_ref[:DIFF_V_WIDTH, :])
                  + _dot(mo_ref[...], w_ref[DIFF_V_WIDTH:, :]))


def _out_proj(tok, mo, x2d, wout_bf, layer, w_ff1, w_ff2, tm):
    m = x2d.shape[0]
    rows = lambda i: (i, 0)
    cast_in, cast_out, cast_shapes = _ffn_cast_specs(layer, m // tm, lambda i: i)
    return pl.pallas_call(
        _out_proj_kernel,
        grid=(m // tm,),
        in_specs=[
            pl.BlockSpec((tm, DIFF_V_WIDTH), rows),
            pl.BlockSpec((tm, MEM_WIDTH), rows),
            pl.BlockSpec((tm, D_MODEL), rows),
            _resident((None, D_MODEL, D_MODEL), lambda i: (layer, 0, 0)),
        ] + cast_in,
        out_specs=[pl.BlockSpec((tm, D_MODEL), rows)] + cast_out,
        out_shape=[jax.ShapeDtypeStruct(x2d.shape, F32)] + cast_shapes,
        compiler_params=pltpu.CompilerParams(
            dimension_semantics=("arbitrary",),
            vmem_limit_bytes=VMEM_LIMIT_BYTES),
        name="out_proj",
    )(tok, mo, x2d, wout_bf, w_ff1, w_ff2)


def kernel(x, mem, mix_norm, w_in, w_out, mem_norm, w_mem_kv, mem_q_norm, mem_k_norm,
           ffn_norm, w_ff1, w_ff2, pool_w, pool_scale, kv_norm, w_kv, k_norm, q_norm,
           diff_lambda, subln_norm):
    batch, seq, _ = x.shape
    m = batch * seq
    x2d = x.reshape(m, D_MODEL)
    mem2d = mem.reshape(batch * MEM_LEN, D_MODEL)

    w_in_bf, w_out_bf = w_in.astype(BF16), w_out.astype(BF16)

    mk, mv = _memkv(mem2d, mem_norm, w_mem_kv.astype(BF16), mem_k_norm, batch)

    x2d, w1_bf, w2_bf = _layer_a(x2d, mix_norm[0:1], w_in_bf, w_out_bf, 0,
                                 pool_w[0].astype(BF16), pool_scale[0:1], mk, mv,
                                 mem_q_norm[0:1], w_ff1, w_ff2, batch, seq, tm=512)
    x2d = _ffn(x2d, ffn_norm[0:1], w1_bf, w2_bf, tm=512, tf=1024)

    lam_init = 0.8 - 0.6 * math.exp(-0.3 * 1)
    tq = 256
    q1t, q2t, k1, k2, vt, mo = _layer_b_proj(
        x2d, mix_norm[1:2], kv_norm.reshape(1, D_MODEL), w_in_bf, 1, w_kv.astype(BF16),
        q_norm[0], k_norm, mk, mv, mem_q_norm[1:2], batch, seq, tm=256, tq=tq)
    tok = _diff_attention(q1t, q2t, k1, k2, vt, diff_lambda[0], subln_norm[0:1],
                          batch, seq, tq=tq, pairs=2, lam_init=lam_init)
    x2d, w1_bf, w2_bf = _out_proj(tok, mo, x2d, w_out_bf, 1, w_ff1, w_ff2, tm=512)
    x2d = _ffn(x2d, ffn_norm[1:2], w1_bf, w2_bf, tm=512, tf=1024)
    return x2d.reshape(batch, seq, D_MODEL)
```

```python
import functools
import math

import jax
import jax.numpy as jnp
from jax import lax
from jax.experimental import pallas as pl
from jax.experimental.pallas import tpu as pltpu

D_MODEL = 2048
MEM_LEN = 256
MEM_HEADS = 4
MEM_HEAD_DIM = 128
MEM_WIDTH = MEM_HEADS * MEM_HEAD_DIM
POOL_WIDTH = D_MODEL - MEM_WIDTH
POOL_WINDOWS = (2, 4, 8, 16)
POOL_GROUP = POOL_WIDTH // len(POOL_WINDOWS)
POOL_HALO = 16
DIFF_HEAD_DIM = 64
DIFF_V_DIM = 128
DIFF_HEADS = POOL_WIDTH // DIFF_V_DIM
DIFF_QK_WIDTH = DIFF_HEADS * DIFF_HEAD_DIM
DIFF_V_WIDTH = DIFF_HEADS * DIFF_V_DIM
VT_ROWS = DIFF_V_DIM + 16
KV_WIDTH = 2 * DIFF_QK_WIDTH + DIFF_V_WIDTH
ROPE_THETA = 500000.0
ROPE_DIM = DIFF_HEAD_DIM // 4
D_FF = 4 * D_MODEL
EPS = 1e-6
NEG_INF = -1e30

LANES = 128
MXU_DIM = 256
VMEM_LIMIT_BYTES = 56 * 1024 * 1024

BF16 = jnp.bfloat16
F32 = jnp.float32


def _resident(shape, index_map):
    return pl.BlockSpec(shape, index_map, pipeline_mode=pl.Buffered(1))


def _rstd(x):
    return lax.rsqrt(jnp.mean(x * x, axis=-1, keepdims=True) + EPS)


def _dot(a, b):
    return jnp.dot(a, b, preferred_element_type=F32)


def _dot_nt(a, b):
    return lax.dot_general(a, b, (((1,), (1,)), ((), ())), preferred_element_type=F32)


def _memkv_kernel(mem_ref, g_ref, w_ref, kg_ref, k_ref, v_ref):
    x = mem_ref[...]
    xn = (x * _rstd(x) * g_ref[0]).astype(BF16)
    kv = _dot(xn, w_ref[0])
    for h in range(MEM_HEADS):
        hs = slice(h * MEM_HEAD_DIM, (h + 1) * MEM_HEAD_DIM)
        kh = kv[:, hs]
        k_ref[0, :, hs] = (kh * _rstd(kh) * kg_ref[0]).astype(BF16)
    v_ref[0] = kv[:, MEM_WIDTH:].astype(BF16)


def _memkv(mem2d, mem_norm, w_mem_kv_bf, mem_k_norm, batch):
    depth = mem_norm.shape[0]
    out_sd = jax.ShapeDtypeStruct((depth, batch * MEM_LEN, MEM_WIDTH), BF16)
    return pl.pallas_call(
        _memkv_kernel,
        grid=(depth, batch),
        in_specs=[
            pl.BlockSpec((MEM_LEN, D_MODEL), lambda l, b: (b, 0)),
            pl.BlockSpec((1, 1, D_MODEL), lambda l, b: (l, 0, 0)),
            pl.BlockSpec((1, D_MODEL, 2 * MEM_WIDTH), lambda l, b: (l, 0, 0)),
            pl.BlockSpec((1, 1, MEM_HEAD_DIM), lambda l, b: (l, 0, 0)),
        ],
        out_specs=[
            pl.BlockSpec((1, MEM_LEN, MEM_WIDTH), lambda l, b: (l, b, 0)),
            pl.BlockSpec((1, MEM_LEN, MEM_WIDTH), lambda l, b: (l, b, 0)),
        ],
        out_shape=[out_sd, out_sd],
        compiler_params=pltpu.CompilerParams(
            dimension_semantics=("arbitrary", "arbitrary"),
            vmem_limit_bytes=VMEM_LIMIT_BYTES),
        name="mem_kv",
    )(mem2d, mem_norm.reshape(depth, 1, D_MODEL), w_mem_kv_bf,
      mem_k_norm.reshape(depth, 1, MEM_HEAD_DIM))


def _mem_attention_head(qh, qg, kh, vh):
    qn = (qh * _rstd(qh) * qg).astype(BF16)
    s = _dot_nt(qn, kh) * (MEM_HEAD_DIM ** -0.5)
    p = jnp.exp(s - jnp.max(s, axis=-1, keepdims=True))
    l = jnp.sum(p, axis=-1, keepdims=True)
    return _dot(p.astype(BF16), vh) / l


def _pool_bands():
    r = jnp.arange(MXU_DIM)[:, None]
    c = jnp.arange(MXU_DIM)[None, :]
    rh = jnp.arange(POOL_HALO)[:, None]
    ch = jnp.arange(POOL_HALO)[None, :]
    band = jnp.stack([((r - c >= 0) & (r - c < w)) for w in POOL_WINDOWS])
    bandh = jnp.stack([(rh + POOL_HALO - ch < w) for w in POOL_WINDOWS])
    return band.astype(BF16), bandh.astype(BF16)


def _ffn_cast_specs(layer, nsteps, step, tf):
    chunk = D_FF // nsteps
    per_block = tf // chunk
    in_specs = [pl.BlockSpec((None, D_MODEL, chunk), lambda *ix: (layer, 0, step(*ix))),
                pl.BlockSpec((None, chunk, D_MODEL), lambda *ix: (layer, step(*ix), 0))]
    out_specs = [pl.BlockSpec((None, D_MODEL, chunk),
                              lambda *ix: (step(*ix) // per_block, 0, step(*ix) % per_block)),
                 pl.BlockSpec((chunk, D_MODEL), lambda *ix: (step(*ix), 0))]
    out_shapes = [jax.ShapeDtypeStruct((D_FF // tf, D_MODEL, tf), BF16),
                  jax.ShapeDtypeStruct((D_FF, D_MODEL), BF16)]
    return in_specs, out_specs, out_shapes


def _layer_a_kernel(x_ref, xh_ref, g_ref, win_ref, band_ref, bandh_ref, poolw_ref, pscale_ref,
                    mk_ref, mv_ref, mqg_ref, wout_ref, w1_ref, w2_ref,
                    o_ref, w1bf_ref, w2bf_ref, cat_ref, *, tm):
    w1bf_ref[...] = w1_ref[...].astype(BF16)
    w2bf_ref[...] = w2_ref[...].astype(BF16)
    i = pl.program_id(1)
    x = x_ref[...]
    rstd = _rstd(x)
    xn = (x * g_ref[...]).astype(BF16)

    xh = xh_ref[...]
    uh = _dot((xh * g_ref[...]).astype(BF16), win_ref[:, :POOL_WIDTH]) * _rstd(xh)
    uh = jnp.where(i == 0, 0.0, uh).astype(BF16)

    row = lax.broadcasted_iota(jnp.int32, (MXU_DIM, 1), 0)
    half_width = 2 * POOL_GROUP

    def pool_epilogue(u_half, half):
        ub = u_half.astype(BF16)
        for gl in range(2):
            g = 2 * half + gl
            w = POOL_WINDOWS[g]
            ls = slice(gl * POOL_GROUP, (gl + 1) * POOL_GROUP)
            cs = slice(g * POOL_GROUP, (g + 1) * POOL_GROUP)
            for sb in range(tm // MXU_DIM):
                r0 = sb * MXU_DIM
                rs = slice(r0, r0 + MXU_DIM)
                main = _dot(band_ref[g], ub[rs, ls])
                halo = uh[:, cs] if sb == 0 else ub[r0 - POOL_HALO:r0, ls]
                top = _dot(bandh_ref[g], halo)
                wsum = jnp.concatenate([main[:POOL_HALO] + top, main[POOL_HALO:]], axis=0)
                t = i * tm + r0 + row
                count = jnp.minimum(t + 1, w).astype(F32)
                pooled = wsum / count - u_half[rs, ls]
                tok = _dot(pooled.astype(BF16), poolw_ref[g]) * pscale_ref[:, cs]
                cat_ref[rs, cs] = tok.astype(BF16)

    def out_partial(c0, c1):
        return _dot(cat_ref[:, c0:c1], wout_ref[c0:c1, :])

    u0 = _dot(xn, win_ref[:, :half_width])
    u1 = _dot(xn, win_ref[:, half_width:POOL_WIDTH])
    pool_epilogue(u0 * rstd, 0)
    uq = _dot(xn, win_ref[:, POOL_WIDTH:]) * rstd
    pool_epilogue(u1 * rstd, 1)
    acc = x + out_partial(0, POOL_WIDTH)
    mk, mv = mk_ref[0], mv_ref[0]
    for h in range(MEM_HEADS):
        hs = slice(h * MEM_HEAD_DIM, (h + 1) * MEM_HEAD_DIM)
        o = _mem_attention_head(uq[:, hs], mqg_ref[...], mk[:, hs], mv[:, hs])
        cat_ref[:, POOL_WIDTH + h * MEM_HEAD_DIM:POOL_WIDTH + (h + 1) * MEM_HEAD_DIM] = o.astype(BF16)
    o_ref[...] = acc + out_partial(POOL_WIDTH, D_MODEL)


def _layer_a(x2d, g, win_bf, wout_bf, layer, poolw_bf, pscale, mk, mv, mqg, w_ff1, w_ff2,
             batch, seq, tm, tf):
    nt = seq // tm
    wslab = lambda b, i: (layer, 0, 0)
    cast_in, cast_out, cast_shapes = _ffn_cast_specs(
        layer, batch * nt, lambda b, i: b * nt + i, tf)
    band, bandh = _pool_bands()
    ngrp = len(POOL_WINDOWS)
    const2 = lambda b, i: (0, 0)
    const3 = lambda b, i: (0, 0, 0)
    halo_blocks = tm // POOL_HALO
    return pl.pallas_call(
        functools.partial(_layer_a_kernel, tm=tm),
        grid=(batch, nt),
        in_specs=[
            pl.BlockSpec((tm, D_MODEL), lambda b, i: (b * nt + i, 0)),
            pl.BlockSpec((POOL_HALO, D_MODEL),
                         lambda b, i: (jnp.maximum((b * nt + i) * halo_blocks - 1, 0), 0)),
            _resident((1, D_MODEL), const2),
            _resident((None, D_MODEL, D_MODEL), wslab),
            _resident((ngrp, MXU_DIM, MXU_DIM), const3),
            _resident((ngrp, POOL_HALO, POOL_HALO), const3),
            _resident((ngrp, POOL_GROUP, POOL_GROUP), const3),
            _resident((1, POOL_WIDTH), const2),
            pl.BlockSpec((1, MEM_LEN, MEM_WIDTH), lambda b, i: (layer, b, 0)),
            pl.BlockSpec((1, MEM_LEN, MEM_WIDTH), lambda b, i: (layer, b, 0)),
            _resident((1, MEM_HEAD_DIM), const2),
            _resident((None, D_MODEL, D_MODEL), wslab),
        ] + cast_in,
        out_specs=[pl.BlockSpec((tm, D_MODEL), lambda b, i: (b * nt + i, 0))] + cast_out,
        out_shape=[jax.ShapeDtypeStruct(x2d.shape, F32)] + cast_shapes,
        scratch_shapes=[pltpu.VMEM((tm, D_MODEL), BF16)],
        compiler_params=pltpu.CompilerParams(
            dimension_semantics=("arbitrary", "arbitrary"),
            vmem_limit_bytes=VMEM_LIMIT_BYTES),
        name="layer_a_mixer",
    )(x2d, x2d, g, win_bf, band, bandh, poolw_bf, pscale, mk, mv, mqg, wout_bf, w_ff1, w_ff2)


def _ffn_kernel(x_ref, g_ref, w1_ref, w2_ref, o_ref, xg_ref, r2_ref, *, nk):
    k = pl.program_id(1)

    def partial(xg):
        z = jnp.maximum(_dot(xg, w1_ref[...]), 0.0)
        return _dot((z * z).astype(BF16), w2_ref[...])

    @pl.when(k == 0)
    def _():
        x = x_ref[...]
        xg = (x * g_ref[...]).astype(BF16)
        xg_ref[...] = xg
        r2_ref[...] = 1.0 / (jnp.mean(x * x, axis=-1, keepdims=True) + EPS)
        o_ref[...] = partial(xg)

    @pl.when(jnp.logical_and(k > 0, k < nk - 1))
    def _():
        o_ref[...] += partial(xg_ref[...])

    @pl.when(k == nk - 1)
    def _():
        o_ref[...] = x_ref[...] + r2_ref[...] * (o_ref[...] + partial(xg_ref[...]))


def _ffn(x2d, g, w1_bf, w2_bf, tm, tf):
    m = x2d.shape[0]
    assert w1_bf.shape == (D_FF // tf, D_MODEL, tf)
    nk = D_FF // tf
    assert nk >= 2
    return pl.pallas_call(
        functools.partial(_ffn_kernel, nk=nk),
        grid=(m // tm, nk),
        in_specs=[
            pl.BlockSpec((tm, D_MODEL), lambda i, k: (i, 0)),
            _resident((1, D_MODEL), lambda i, k: (0, 0)),
            pl.BlockSpec((None, D_MODEL, tf), lambda i, k: (k, 0, 0)),
            pl.BlockSpec((tf, D_MODEL), lambda i, k: (k, 0)),
        ],
        out_specs=pl.BlockSpec((tm, D_MODEL), lambda i, k: (i, 0)),
        out_shape=jax.ShapeDtypeStruct(x2d.shape, F32),
        scratch_shapes=[pltpu.VMEM((tm, D_MODEL), BF16), pltpu.VMEM((tm, 1), F32)],
        compiler_params=pltpu.CompilerParams(
            dimension_semantics=("arbitrary", "arbitrary"),
            vmem_limit_bytes=VMEM_LIMIT_BYTES),
        name="ffn",
    )(x2d, g, w1_bf, w2_bf)


def _rope_coeffs(seq):
    half = ROPE_DIM // 2
    pos = jnp.arange(seq, dtype=F32)
    inv = ROPE_THETA ** (-(jnp.arange(half, dtype=F32) * 2.0) / ROPE_DIM)
    ang = pos[:, None] * inv[None, :]
    cos, sin = jnp.cos(ang), jnp.sin(ang)
    ones = jnp.ones((seq, DIFF_HEAD_DIM - ROPE_DIM), F32)
    zeros = jnp.zeros((seq, DIFF_HEAD_DIM - ROPE_DIM), F32)
    zh = jnp.zeros((seq, half), F32)
    c0 = jnp.concatenate([cos, cos, ones], axis=-1)
    c_up = jnp.concatenate([-sin, zh, zeros], axis=-1)
    c_dn = jnp.concatenate([zh, sin, zeros], axis=-1)
    rep = LANES // DIFF_HEAD_DIM
    return tuple(jnp.tile(c, (1, rep)) for c in (c0, c_up, c_dn))


def _head_norm_rope(t, gain, bd, c0, c_up, c_dn):
    ss = _dot((t * t).astype(BF16), bd)
    tn = t * lax.rsqrt(ss * (1.0 / DIFF_HEAD_DIM) + EPS) * gain
    outs = []
    half = ROPE_DIM // 2
    for p in range(MXU_DIM // LANES):
        tp = tn[:, p * LANES:(p + 1) * LANES]
        up = pltpu.roll(tp, LANES - half, axis=1)
        dn = pltpu.roll(tp, half, axis=1)
        outs.append(tp * c0 + up * c_up + dn * c_dn)
    return jnp.concatenate(outs, axis=-1)


def _layer_b_proj_kernel(x_ref, gmix_ref, gkv_ref, win_ref, wkv_ref, bd_ref, qg_ref, kg_ref,
                         c0_ref, cup_ref, cdn_ref, mk_ref, mv_ref, mqg_ref, w1_ref, w2_ref,
                         q1t_ref, q2t_ref, k1_ref, k2_ref, vt_ref, mo_ref, w1bf_ref, w2bf_ref):
    w1bf_ref[...] = w1_ref[...].astype(BF16)
    w2bf_ref[...] = w2_ref[...].astype(BF16)
    x = x_ref[...]
    tm = x.shape[0]
    rstd = _rstd(x)
    xn_mix = (x * gmix_ref[...]).astype(BF16)
    xn_kv = (x * gkv_ref[...]).astype(BF16)
    bd = bd_ref[...]
    c0, cup, cdn = c0_ref[...], cup_ref[...], cdn_ref[...]
    qg = qg_ref[...] * (DIFF_HEAD_DIM ** -0.5 * math.log2(math.e))
    kg = kg_ref[...]
    pairs_per_chunk = MXU_DIM // LANES
    pad_rows = VT_ROWS - DIFF_V_DIM
    ones_row = (lax.broadcasted_iota(jnp.int32, (pad_rows, tm), 0) == 0).astype(BF16)
    mk, mv = mk_ref[0], mv_ref[0]

    def q_epilogue(qt_ref, c):
        def run(t):
            qt = _head_norm_rope(t, qg, bd, c0, cup, cdn).T.astype(BF16)
            for p in range(pairs_per_chunk):
                qt_ref[0, pairs_per_chunk * c + p, 0] = qt[p * LANES:(p + 1) * LANES]
        return run

    def k_epilogue(k_ref, c):
        def run(t):
            k_ref[:, c * MXU_DIM:(c + 1) * MXU_DIM] = _head_norm_rope(
                t, kg, bd, c0, cup, cdn).astype(BF16)
        return run

    def v_epilogue(c):
        def run(t):
            for p in range(pairs_per_chunk):
                h = pairs_per_chunk * c + p
                vt_ref[0, h, 0, :DIFF_V_DIM, :] = t[:, p * DIFF_V_DIM:(p + 1) * DIFF_V_DIM].T.astype(BF16)
                vt_ref[0, h, 0, DIFF_V_DIM:, :] = ones_row
        return run

    def mem_epilogue(c):
        def run(t):
            for p in range(pairs_per_chunk):
                h = pairs_per_chunk * c + p
                hs = slice(h * MEM_HEAD_DIM, (h + 1) * MEM_HEAD_DIM)
                o = _mem_attention_head(t[:, p * MEM_HEAD_DIM:(p + 1) * MEM_HEAD_DIM],
                                        mqg_ref[...], mk[:, hs], mv[:, hs])
                mo_ref[:, hs] = o.astype(BF16)
        return run

    chunks = []
    for c in range(DIFF_QK_WIDTH // MXU_DIM):
        chunks += [
            (xn_mix, win_ref, c * MXU_DIM, q_epilogue(q1t_ref, c)),
            (xn_kv, wkv_ref, c * MXU_DIM, k_epilogue(k1_ref, c)),
            (xn_kv, wkv_ref, 2 * DIFF_QK_WIDTH + 2 * c * MXU_DIM, v_epilogue(2 * c)),
            (xn_mix, win_ref, DIFF_QK_WIDTH + c * MXU_DIM, q_epilogue(q2t_ref, c)),
            (xn_kv, wkv_ref, DIFF_QK_WIDTH + c * MXU_DIM, k_epilogue(k2_ref, c)),
            (xn_kv, wkv_ref, 2 * DIFF_QK_WIDTH + (2 * c + 1) * MXU_DIM, v_epilogue(2 * c + 1)),
        ]
    for c in range(MEM_WIDTH // MXU_DIM):
        chunks.append((xn_mix, win_ref, 2 * DIFF_QK_WIDTH + c * MXU_DIM, mem_epilogue(c)))

    def project(chunk):
        xn, w_ref, col, _ = chunk
        return _dot(xn, w_ref[:, col:col + MXU_DIM])

    pending = project(chunks[0])
    for n, chunk in enumerate(chunks):
        ready = pending
        if n + 1 < len(chunks):
            pending = project(chunks[n + 1])
        chunk[3](ready * rstd)


def _layer_b_proj(x2d, gmix, gkv, win_bf, layer, wkv_bf, q_norm, k_norm, mk, mv, mqg,
                  w_ff1, w_ff2, batch, seq, tm, tq, tf):
    nt = seq // tm
    cast_in, cast_out, cast_shapes = _ffn_cast_specs(
        layer, batch * nt, lambda b, i: b * nt + i, tf)
    nq = seq // tq
    sub = tq // tm
    npairs = DIFF_QK_WIDTH // LANES
    m = x2d.shape[0]
    c0, cup, cdn = _rope_coeffs(seq)
    lane_head = jnp.arange(MXU_DIM) // DIFF_HEAD_DIM
    bd = (lane_head[:, None] == lane_head[None, :]).astype(BF16)
    qg = jnp.tile(q_norm.reshape(1, DIFF_HEAD_DIM), (1, MXU_DIM // DIFF_HEAD_DIM))
    kg = jnp.tile(k_norm.reshape(1, DIFF_HEAD_DIM), (1, MXU_DIM // DIFF_HEAD_DIM))
    const2 = lambda b, i: (0, 0)
    rows = lambda b, i: (b * nt + i, 0)
    qk_sd = jax.ShapeDtypeStruct((m, DIFF_QK_WIDTH), BF16)
    qt_sd = jax.ShapeDtypeStruct((batch, npairs, nq, LANES, tq), BF16)
    qt_spec = pl.BlockSpec((1, npairs, 1, LANES, tm), lambda b, i: (b, 0, i // sub, 0, i % sub))
    return pl.pallas_call(
        _layer_b_proj_kernel,
        grid=(batch, nt),
        in_specs=[
            pl.BlockSpec((tm, D_MODEL), rows),
            _resident((1, D_MODEL), const2),
            _resident((1, D_MODEL), const2),
            _resident((None, D_MODEL, D_MODEL), lambda b, i: (layer, 0, 0)),
            _resident((D_MODEL, KV_WIDTH), const2),
            _resident((MXU_DIM, MXU_DIM), const2),
            _resident((1, MXU_DIM), const2),
            _resident((1, MXU_DIM), const2),
            pl.BlockSpec((tm, LANES), lambda b, i: (i, 0)),
            pl.BlockSpec((tm, LANES), lambda b, i: (i, 0)),
            pl.BlockSpec((tm, LANES), lambda b, i: (i, 0)),
            pl.BlockSpec((1, MEM_LEN, MEM_WIDTH), lambda b, i: (layer, b, 0)),
            pl.BlockSpec((1, MEM_LEN, MEM_WIDTH), lambda b, i: (layer, b, 0)),
            _resident((1, MEM_HEAD_DIM), const2),
        ] + cast_in,
        out_specs=[
            qt_spec,
            qt_spec,
            pl.BlockSpec((tm, DIFF_QK_WIDTH), rows),
            pl.BlockSpec((tm, DIFF_QK_WIDTH), rows),
            pl.BlockSpec((1, DIFF_HEADS, 1, VT_ROWS, tm),
                         lambda b, i: (b, 0, i // sub, 0, i % sub)),
            pl.BlockSpec((tm, MEM_WIDTH), rows),
        ] + cast_out,
        out_shape=[qt_sd, qt_sd, qk_sd, qk_sd,
                   jax.ShapeDtypeStruct((batch, DIFF_HEADS, nq, VT_ROWS, tq), BF16),
                   jax.ShapeDtypeStruct((m, MEM_WIDTH), BF16)] + cast_shapes,
        compiler_params=pltpu.CompilerParams(
            dimension_semantics=("arbitrary", "arbitrary"),
            vmem_limit_bytes=VMEM_LIMIT_BYTES),
        name="layer_b_proj",
    )(x2d, gmix, gkv, win_bf, wkv_bf, bd, qg, kg, c0, cup, cdn, mk, mv, mqg, w_ff1, w_ff2)


def _diff_attn_kernel(q1t_ref, q2t_ref, k1_ref, k2_ref, vt_ref, lam_ref, sg_ref, o_ref,
                      s_ref, m_ref, acc_ref, *, tq, nq, pairs, lam_init):
    feat = lax.broadcasted_iota(jnp.int32, (LANES, 1), 0)
    heads_per_pair = LANES // DIFF_HEAD_DIM
    combos = [(pr, hh, mp) for pr in range(pairs) for hh in range(heads_per_pair)
              for mp in range(2)]
    in_head = [(feat >= hh * DIFF_HEAD_DIM) & (feat < (hh + 1) * DIFF_HEAD_DIM)
               for hh in range(heads_per_pair)]
    key_pos = lax.broadcasted_iota(jnp.int32, (tq, tq), 0)
    q_pos = lax.broadcasted_iota(jnp.int32, (tq, tq), 1)

    def scores(i, blk):
        r0 = pl.multiple_of(blk * tq, tq)
        out = []
        for pr, hh, mp in combos:
            qt = (q1t_ref, q2t_ref)[mp][0, pr, i]
            qm = jnp.where(in_head[hh], qt, jnp.zeros_like(qt))
            kblk = (k1_ref, k2_ref)[mp][pl.ds(r0, tq), pr * LANES:(pr + 1) * LANES]
            out.append(_dot(kblk, qm))
        return out

    def store_scores(sts, diagonal):
        for c, st in enumerate(sts):
            s_ref[c] = jnp.where(key_pos <= q_pos, st, NEG_INF) if diagonal else st

    def consume(blk, first):
        for c, (pr, hh, mp) in enumerate(combos):
            st = s_ref[c]
            m_old = jnp.where(first, NEG_INF, m_ref[c])
            m_new = jnp.maximum(m_old, jnp.max(st, axis=0, keepdims=True))
            alpha = jnp.exp2(m_old - m_new)
            p = jnp.exp2(st - m_new).astype(BF16)
            vtblk = vt_ref[0, heads_per_pair * pr + hh, blk]
            acc_ref[c] = alpha * acc_ref[c] + _dot(vtblk, p)
            m_ref[c] = m_new

    def finalize(i):
        lq = lam_ref[...]
        lam = (jnp.exp(jnp.sum(lq[0:1] * lq[1:2], axis=-1, keepdims=True))
               - jnp.exp(jnp.sum(lq[2:3] * lq[3:4], axis=-1, keepdims=True)) + lam_init)
        r0 = pl.multiple_of(i * tq, tq)
        for hh in range(heads_per_pair * pairs):
            a1, a2 = acc_ref[2 * hh], acc_ref[2 * hh + 1]
            ot = (a1[:DIFF_V_DIM] * (1.0 / a1[DIFF_V_DIM:DIFF_V_DIM + 1])
                  - lam * (a2[:DIFF_V_DIM] * (1.0 / a2[DIFF_V_DIM:DIFF_V_DIM + 1])))
            ot = ot * lax.rsqrt(jnp.mean(ot * ot, axis=0, keepdims=True) + EPS)
            o = ot.T * sg_ref[...] * (1.0 - lam_init)
            o_ref[pl.ds(r0, tq), hh * DIFF_V_DIM:(hh + 1) * DIFF_V_DIM] = o.astype(BF16)

    acc_ref[...] = jnp.zeros(acc_ref.shape, F32)
    m_ref[...] = jnp.full(m_ref.shape, NEG_INF, F32)
    store_scores(scores(0, 0), True)

    def body(t, carry):
        i, pos = carry
        last = pos == i
        ni = jnp.where(last, i + 1, i)
        npos = jnp.where(last, 0, pos + 1)
        blk = jnp.where(pos == 0, i, pos - 1)
        nblk = jnp.where(npos == 0, ni, npos - 1)

        @pl.when(last)
        def _():
            nxt = scores(ni, nblk)
            consume(blk, pos == 0)
            store_scores(nxt, True)
            finalize(i)

        @pl.when(jnp.logical_not(last))
        def _():
            nxt = scores(ni, nblk)
            consume(blk, pos == 0)
            store_scores(nxt, False)

        return ni, npos

    ntasks = nq * (nq + 1) // 2
    lax.fori_loop(0, ntasks - 1, body, (jnp.int32(0), jnp.int32(0)))
    consume(max(nq - 2, 0), nq == 1)
    finalize(nq - 1)


def _diff_attention(q1t, q2t, k1, k2, vt, diff_lambda, subln, batch, seq, tq, pairs, lam_init):
    nq = seq // tq
    m = k1.shape[0]
    npairs = DIFF_QK_WIDTH // LANES
    heads_per_step = pairs * (LANES // DIFF_HEAD_DIM)
    vw = heads_per_step * DIFF_V_DIM
    ncombo = 2 * heads_per_step
    qspec = pl.BlockSpec((1, pairs, nq, LANES, tq), lambda b, hp: (b, hp, 0, 0, 0))
    kspec = pl.BlockSpec((seq, pairs * LANES), lambda b, hp: (b, hp))
    return pl.pallas_call(
        functools.partial(_diff_attn_kernel, tq=tq, nq=nq, pairs=pairs, lam_init=lam_init),
        grid=(batch, npairs // pairs),
        in_specs=[
            qspec, qspec, kspec, kspec,
            pl.BlockSpec((1, heads_per_step, nq, VT_ROWS, tq), lambda b, hp: (b, hp, 0, 0, 0)),
            _resident((4, DIFF_HEAD_DIM), lambda b, hp: (0, 0)),
            _resident((1, DIFF_V_DIM), lambda b, hp: (0, 0)),
        ],
        out_specs=pl.BlockSpec((seq, vw), lambda b, hp: (b, hp)),
        out_shape=jax.ShapeDtypeStruct((m, DIFF_V_WIDTH), BF16),
        scratch_shapes=[
            pltpu.VMEM((ncombo, tq, tq), F32),
            pltpu.VMEM((ncombo, 1, tq), F32),
            pltpu.VMEM((ncombo, VT_ROWS, tq), F32),
        ],
        compiler_params=pltpu.CompilerParams(
            dimension_semantics=("arbitrary", "arbitrary"),
            vmem_limit_bytes=VMEM_LIMIT_BYTES),
        name="diff_attention",
    )(q1t, q2t, k1, k2, vt, diff_lambda, subln)


def _out_proj_kernel(tok_ref, mo_ref, x_ref, w_ref, o_ref):
    o_ref[...] = (x_ref[...] + _dot(tok_ref[...], w_ref[:DIFF_V_WIDTH, :])
                  + _dot(mo_ref[...], w_ref[DIFF_V_WIDTH:, :]))


def _out_proj(tok, mo, x2d, wout_bf, layer, tm):
    m = x2d.shape[0]
    rows = lambda i: (i, 0)
    return pl.pallas_call(
        _out_proj_kernel,
        grid=(m // tm,),
        in_specs=[
            pl.BlockSpec((tm, DIFF_V_WIDTH), rows),
            pl.BlockSpec((tm, MEM_WIDTH), rows),
            pl.BlockSpec((tm, D_MODEL), rows),
            _resident((None, D_MODEL, D_MODEL), lambda i: (layer, 0, 0)),
        ],
        out_specs=pl.BlockSpec((tm, D_MODEL), rows),
        out_shape=jax.ShapeDtypeStruct(x2d.shape, F32),
        compiler_params=pltpu.CompilerParams(
            dimension_semantics=("arbitrary",),
            vmem_limit_bytes=VMEM_LIMIT_BYTES),
        name="out_proj",
    )(tok, mo, x2d, wout_bf)


def kernel(x, mem, mix_norm, w_in, w_out, mem_norm, w_mem_kv, mem_q_norm, mem_k_norm,
           ffn_norm, w_ff1, w_ff2, pool_w, pool_scale, kv_norm, w_kv, k_norm, q_norm,
           diff_lambda, subln_norm):
    batch, seq, _ = x.shape
    m = batch * seq
    x2d = x.reshape(m, D_MODEL)
    mem2d = mem.reshape(batch * MEM_LEN, D_MODEL)

    w_in_bf, w_out_bf = w_in.astype(BF16), w_out.astype(BF16)

    mk, mv = _memkv(mem2d, mem_norm, w_mem_kv.astype(BF16), mem_k_norm, batch)

    tf = 1024

    x2d, w1_bf, w2_bf = _layer_a(x2d, mix_norm[0:1], w_in_bf, w_out_bf, 0,
                                 pool_w[0].astype(BF16), pool_scale[0:1], mk, mv,
                                 mem_q_norm[0:1], w_ff1, w_ff2, batch, seq, tm=512, tf=tf)
    x2d = _ffn(x2d, ffn_norm[0:1], w1_bf, w2_bf, tm=512, tf=tf)

    lam_init = 0.8 - 0.6 * math.exp(-0.3 * 1)
    tq = 256
    q1t, q2t, k1, k2, vt, mo, w1_bf, w2_bf = _layer_b_proj(
        x2d, mix_norm[1:2], kv_norm.reshape(1, D_MODEL), w_in_bf, 1, w_kv.astype(BF16),
        q_norm[0], k_norm, mk, mv, mem_q_norm[1:2], w_ff1, w_ff2, batch, seq,
        tm=256, tq=tq, tf=tf)
    tok = _diff_attention(q1t, q2t, k1, k2, vt, diff_lambda[0], subln_norm[0:1],
                          batch, seq, tq=tq, pairs=2, lam_init=lam_init)
    x2d = _out_proj(tok, mo, x2d, w_out_bf, 1, tm=1024)
    x2d = _ffn(x2d, ffn_norm[1:2], w1_bf, w2_bf, tm=512, tf=tf)
    return x2d.reshape(batch, seq, D_MODEL)
```

```python
import functools
import math

import jax
import jax.numpy as jnp
from jax import lax
from jax.experimental import pallas as pl
from jax.experimental.pallas import tpu as pltpu

D_MODEL = 2048
MEM_LEN = 256
MEM_HEADS = 4
MEM_HEAD_DIM = 128
MEM_WIDTH = MEM_HEADS * MEM_HEAD_DIM
POOL_WIDTH = D_MODEL - MEM_WIDTH
POOL_WINDOWS = (2, 4, 8, 16)
POOL_GROUP = POOL_WIDTH // len(POOL_WINDOWS)
POOL_HALO = 16
DIFF_HEAD_DIM = 64
DIFF_V_DIM = 128
DIFF_HEADS = POOL_WIDTH // DIFF_V_DIM
DIFF_QK_WIDTH = DIFF_HEADS * DIFF_HEAD_DIM
DIFF_V_WIDTH = DIFF_HEADS * DIFF_V_DIM
VT_ROWS = DIFF_V_DIM + 16
KV_WIDTH = 2 * DIFF_QK_WIDTH + DIFF_V_WIDTH
ROPE_THETA = 500000.0
ROPE_DIM = DIFF_HEAD_DIM // 4
D_FF = 4 * D_MODEL
EPS = 1e-6
NEG_INF = -1e30

LANES = 128
MXU_DIM = 256
VMEM_LIMIT_BYTES = 56 * 1024 * 1024

BF16 = jnp.bfloat16
F32 = jnp.float32


def _resident(shape, index_map):
    return pl.BlockSpec(shape, index_map, pipeline_mode=pl.Buffered(1))


def _rstd(x):
    return lax.rsqrt(jnp.mean(x * x, axis=-1, keepdims=True) + EPS)


def _dot(a, b):
    return jnp.dot(a, b, preferred_element_type=F32)


def _dot_nt(a, b):
    return lax.dot_general(a, b, (((1,), (1,)), ((), ())), preferred_element_type=F32)


def _memkv_kernel(mem_ref, g_ref, w_ref, kg_ref, win_ref, wout_ref, wkv_ref,
                  k_ref, v_ref, winbf_ref, woutbf_ref, wkvbf_ref):
    winbf_ref[...] = win_ref[...].astype(BF16)
    woutbf_ref[...] = wout_ref[...].astype(BF16)
    wkvbf_ref[...] = wkv_ref[...].astype(BF16)
    x = mem_ref[...]
    xn = (x * _rstd(x) * g_ref[0]).astype(BF16)
    kv = _dot(xn, w_ref[0])
    for h in range(MEM_HEADS):
        hs = slice(h * MEM_HEAD_DIM, (h + 1) * MEM_HEAD_DIM)
        kh = kv[:, hs]
        k_ref[0, :, hs] = (kh * _rstd(kh) * kg_ref[0]).astype(BF16)
    v_ref[0] = kv[:, MEM_WIDTH:].astype(BF16)


def _memkv(mem2d, mem_norm, w_mem_kv_bf, mem_k_norm, w_in, w_out, w_kv, batch):
    depth = mem_norm.shape[0]
    nsteps = depth * batch
    step = lambda l, b: (l * batch + b, 0)
    win2d = w_in.reshape(depth * D_MODEL, D_MODEL)
    wout2d = w_out.reshape(depth * D_MODEL, D_MODEL)
    rows_io = depth * D_MODEL // nsteps
    rows_kv = D_MODEL // nsteps
    out_sd = jax.ShapeDtypeStruct((depth, batch * MEM_LEN, MEM_WIDTH), BF16)
    mk, mv, win_bf, wout_bf, wkv_bf = pl.pallas_call(
        _memkv_kernel,
        grid=(depth, batch),
        in_specs=[
            pl.BlockSpec((MEM_LEN, D_MODEL), lambda l, b: (b, 0)),
            pl.BlockSpec((1, 1, D_MODEL), lambda l, b: (l, 0, 0)),
            pl.BlockSpec((1, D_MODEL, 2 * MEM_WIDTH), lambda l, b: (l, 0, 0)),
            pl.BlockSpec((1, 1, MEM_HEAD_DIM), lambda l, b: (l, 0, 0)),
            pl.BlockSpec((rows_io, D_MODEL), step),
            pl.BlockSpec((rows_io, D_MODEL), step),
            pl.BlockSpec((rows_kv, KV_WIDTH), step),
        ],
        out_specs=[
            pl.BlockSpec((1, MEM_LEN, MEM_WIDTH), lambda l, b: (l, b, 0)),
            pl.BlockSpec((1, MEM_LEN, MEM_WIDTH), lambda l, b: (l, b, 0)),
            pl.BlockSpec((rows_io, D_MODEL), step),
            pl.BlockSpec((rows_io, D_MODEL), step),
            pl.BlockSpec((rows_kv, KV_WIDTH), step),
        ],
        out_shape=[out_sd, out_sd,
                   jax.ShapeDtypeStruct(win2d.shape, BF16),
                   jax.ShapeDtypeStruct(wout2d.shape, BF16),
                   jax.ShapeDtypeStruct(w_kv.shape, BF16)],
        compiler_params=pltpu.CompilerParams(
            dimension_semantics=("arbitrary", "arbitrary"),
            vmem_limit_bytes=VMEM_LIMIT_BYTES),
        name="mem_kv",
    )(mem2d, mem_norm.reshape(depth, 1, D_MODEL), w_mem_kv_bf,
      mem_k_norm.reshape(depth, 1, MEM_HEAD_DIM), win2d, wout2d, w_kv)
    return (mk, mv, win_bf.reshape(w_in.shape), wout_bf.reshape(w_out.shape), wkv_bf)


def _mem_attention_head(qh, qg, kh, vh):
    qn = (qh * _rstd(qh) * qg).astype(BF16)
    s = _dot_nt(qn, kh) * (MEM_HEAD_DIM ** -0.5)
    p = jnp.exp(s - jnp.max(s, axis=-1, keepdims=True))
    l = jnp.sum(p, axis=-1, keepdims=True)
    return _dot(p.astype(BF16), vh) / l


def _pool_bands():
    r = jnp.arange(MXU_DIM)[:, None]
    c = jnp.arange(MXU_DIM)[None, :]
    rh = jnp.arange(POOL_HALO)[:, None]
    ch = jnp.arange(POOL_HALO)[None, :]
    band = jnp.stack([((r - c >= 0) & (r - c < w)) for w in POOL_WINDOWS])
    bandh = jnp.stack([(rh + POOL_HALO - ch < w) for w in POOL_WINDOWS])
    return band.astype(BF16), bandh.astype(BF16)


def _ffn_cast_specs(layer, nsteps, step, tf):
    chunk = D_FF // nsteps
    per_block = tf // chunk
    in_specs = [pl.BlockSpec((None, D_MODEL, chunk), lambda *ix: (layer, 0, step(*ix))),
                pl.BlockSpec((None, chunk, D_MODEL), lambda *ix: (layer, step(*ix), 0))]
    out_specs = [pl.BlockSpec((None, D_MODEL, chunk),
                              lambda *ix: (step(*ix) // per_block, 0, step(*ix) % per_block)),
                 pl.BlockSpec((chunk, D_MODEL), lambda *ix: (step(*ix), 0))]
    out_shapes = [jax.ShapeDtypeStruct((D_FF // tf, D_MODEL, tf), BF16),
                  jax.ShapeDtypeStruct((D_FF, D_MODEL), BF16)]
    return in_specs, out_specs, out_shapes


def _layer_a_kernel(x_ref, xh_ref, g_ref, win_ref, band_ref, bandh_ref, poolw_ref, pscale_ref,
                    mk_ref, mv_ref, mqg_ref, wout_ref, w1_ref, w2_ref,
                    o_ref, w1bf_ref, w2bf_ref, cat_ref, *, tm):
    w1bf_ref[...] = w1_ref[...].astype(BF16)
    w2bf_ref[...] = w2_ref[...].astype(BF16)
    i = pl.program_id(1)
    x = x_ref[...]
    rstd = _rstd(x)
    xn = (x * g_ref[...]).astype(BF16)

    xh = xh_ref[...]
    uh = _dot((xh * g_ref[...]).astype(BF16), win_ref[:, :POOL_WIDTH]) * _rstd(xh)
    uh = jnp.where(i == 0, 0.0, uh).astype(BF16)

    row = lax.broadcasted_iota(jnp.int32, (MXU_DIM, 1), 0)
    half_width = 2 * POOL_GROUP

    def pool_epilogue(u_half, half):
        ub = u_half.astype(BF16)
        for gl in range(2):
            g = 2 * half + gl
            w = POOL_WINDOWS[g]
            ls = slice(gl * POOL_GROUP, (gl + 1) * POOL_GROUP)
            cs = slice(g * POOL_GROUP, (g + 1) * POOL_GROUP)
            for sb in range(tm // MXU_DIM):
                r0 = sb * MXU_DIM
                rs = slice(r0, r0 + MXU_DIM)
                main = _dot(band_ref[g], ub[rs, ls])
                halo = uh[:, cs] if sb == 0 else ub[r0 - POOL_HALO:r0, ls]
                top = _dot(bandh_ref[g], halo)
                wsum = jnp.concatenate([main[:POOL_HALO] + top, main[POOL_HALO:]], axis=0)
                t = i * tm + r0 + row
                count = jnp.minimum(t + 1, w).astype(F32)
                pooled = wsum / count - u_half[rs, ls]
                tok = _dot(pooled.astype(BF16), poolw_ref[g]) * pscale_ref[:, cs]
                cat_ref[rs, cs] = tok.astype(BF16)

    def out_partial(c0, c1):
        return _dot(cat_ref[:, c0:c1], wout_ref[c0:c1, :])

    uq = _dot(xn, win_ref[:, POOL_WIDTH:]) * rstd
    u0 = _dot(xn, win_ref[:, :half_width])
    mk, mv = mk_ref[0], mv_ref[0]
    for h in range(MEM_HEADS):
        hs = slice(h * MEM_HEAD_DIM, (h + 1) * MEM_HEAD_DIM)
        o = _mem_attention_head(uq[:, hs], mqg_ref[...], mk[:, hs], mv[:, hs])
        cat_ref[:, POOL_WIDTH + h * MEM_HEAD_DIM:POOL_WIDTH + (h + 1) * MEM_HEAD_DIM] = o.astype(BF16)
    u1 = _dot(xn, win_ref[:, half_width:POOL_WIDTH])
    pool_epilogue(u0 * rstd, 0)
    acc = x + out_partial(POOL_WIDTH, D_MODEL)
    pool_epilogue(u1 * rstd, 1)
    o_ref[...] = acc + out_partial(0, POOL_WIDTH)


def _layer_a(x2d, g, win_bf, wout_bf, layer, poolw_bf, pscale, mk, mv, mqg, w_ff1, w_ff2,
             batch, seq, tm, tf):
    nt = seq // tm
    wslab = lambda b, i: (layer, 0, 0)
    cast_in, cast_out, cast_shapes = _ffn_cast_specs(
        layer, batch * nt, lambda b, i: b * nt + i, tf)
    band, bandh = _pool_bands()
    ngrp = len(POOL_WINDOWS)
    const2 = lambda b, i: (0, 0)
    const3 = lambda b, i: (0, 0, 0)
    halo_blocks = tm // POOL_HALO
    return pl.pallas_call(
        functools.partial(_layer_a_kernel, tm=tm),
        grid=(batch, nt),
        in_specs=[
            pl.BlockSpec((tm, D_MODEL), lambda b, i: (b * nt + i, 0)),
            pl.BlockSpec((POOL_HALO, D_MODEL),
                         lambda b, i: (jnp.maximum((b * nt + i) * halo_blocks - 1, 0), 0)),
            _resident((1, D_MODEL), const2),
            _resident((None, D_MODEL, D_MODEL), wslab),
            _resident((ngrp, MXU_DIM, MXU_DIM), const3),
            _resident((ngrp, POOL_HALO, POOL_HALO), const3),
            _resident((ngrp, POOL_GROUP, POOL_GROUP), const3),
            _resident((1, POOL_WIDTH), const2),
            pl.BlockSpec((1, MEM_LEN, MEM_WIDTH), lambda b, i: (layer, b, 0)),
            pl.BlockSpec((1, MEM_LEN, MEM_WIDTH), lambda b, i: (layer, b, 0)),
            _resident((1, MEM_HEAD_DIM), const2),
            _resident((None, D_MODEL, D_MODEL), wslab),
        ] + cast_in,
        out_specs=[pl.BlockSpec((tm, D_MODEL), lambda b, i: (b * nt + i, 0))] + cast_out,
        out_shape=[jax.ShapeDtypeStruct(x2d.shape, F32)] + cast_shapes,
        scratch_shapes=[pltpu.VMEM((tm, D_MODEL), BF16)],
        compiler_params=pltpu.CompilerParams(
            dimension_semantics=("arbitrary", "arbitrary"),
            vmem_limit_bytes=VMEM_LIMIT_BYTES),
        name="layer_a_mixer",
    )(x2d, x2d, g, win_bf, band, bandh, poolw_bf, pscale, mk, mv, mqg, wout_bf, w_ff1, w_ff2)


def _ffn_kernel(x_ref, g_ref, w1_ref, w2_ref, o_ref, xg_ref, r2_ref, *, nk):
    k = pl.program_id(1)

    def partial(xg):
        z = jnp.maximum(_dot(xg, w1_ref[...]), 0.0)
        return _dot((z * z).astype(BF16), w2_ref[...])

    @pl.when(k == 0)
    def _():
        x = x_ref[...]
        xg = (x * g_ref[...]).astype(BF16)
        xg_ref[...] = xg
        r2_ref[...] = 1.0 / (jnp.mean(x * x, axis=-1, keepdims=True) + EPS)
        o_ref[...] = partial(xg)

    @pl.when(jnp.logical_and(k > 0, k < nk - 1))
    def _():
        o_ref[...] += partial(xg_ref[...])

    @pl.when(k == nk - 1)
    def _():
        o_ref[...] = x_ref[...] + r2_ref[...] * (o_ref[...] + partial(xg_ref[...]))


def _ffn(x2d, g, w1_bf, w2_bf, tm, tf):
    m = x2d.shape[0]
    assert w1_bf.shape == (D_FF // tf, D_MODEL, tf)
    nk = D_FF // tf
    assert nk >= 2
    return pl.pallas_call(
        functools.partial(_ffn_kernel, nk=nk),
        grid=(m // tm, nk),
        in_specs=[
            pl.BlockSpec((tm, D_MODEL), lambda i, k: (i, 0)),
            _resident((1, D_MODEL), lambda i, k: (0, 0)),
            pl.BlockSpec((None, D_MODEL, tf), lambda i, k: (k, 0, 0)),
            pl.BlockSpec((tf, D_MODEL), lambda i, k: (k, 0)),
        ],
        out_specs=pl.BlockSpec((tm, D_MODEL), lambda i, k: (i, 0)),
        out_shape=jax.ShapeDtypeStruct(x2d.shape, F32),
        scratch_shapes=[pltpu.VMEM((tm, D_MODEL), BF16), pltpu.VMEM((tm, 1), F32)],
        compiler_params=pltpu.CompilerParams(
            dimension_semantics=("arbitrary", "arbitrary"),
            vmem_limit_bytes=VMEM_LIMIT_BYTES),
        name="ffn",
    )(x2d, g, w1_bf, w2_bf)


def _rope_coeffs(seq):
    half = ROPE_DIM // 2
    pos = jnp.arange(seq, dtype=F32)
    inv = ROPE_THETA ** (-(jnp.arange(half, dtype=F32) * 2.0) / ROPE_DIM)
    ang = pos[:, None] * inv[None, :]
    cos, sin = jnp.cos(ang), jnp.sin(ang)
    ones = jnp.ones((seq, DIFF_HEAD_DIM - ROPE_DIM), F32)
    zeros = jnp.zeros((seq, DIFF_HEAD_DIM - ROPE_DIM), F32)
    zh = jnp.zeros((seq, half), F32)
    c0 = jnp.concatenate([cos, cos, ones], axis=-1)
    c_up = jnp.concatenate([-sin, zh, zeros], axis=-1)
    c_dn = jnp.concatenate([zh, sin, zeros], axis=-1)
    rep = LANES // DIFF_HEAD_DIM
    return tuple(jnp.tile(c, (1, rep)) for c in (c0, c_up, c_dn))


def _head_norm_rope(t, gain, bd, c0, c_up, c_dn):
    ss = _dot((t * t).astype(BF16), bd)
    tn = t * lax.rsqrt(ss * (1.0 / DIFF_HEAD_DIM) + EPS) * gain
    outs = []
    half = ROPE_DIM // 2
    for p in range(MXU_DIM // LANES):
        tp = tn[:, p * LANES:(p + 1) * LANES]
        up = pltpu.roll(tp, LANES - half, axis=1)
        dn = pltpu.roll(tp, half, axis=1)
        outs.append(tp * c0 + up * c_up + dn * c_dn)
    return jnp.concatenate(outs, axis=-1)


def _layer_b_proj_kernel(x_ref, gmix_ref, gkv_ref, win_ref, wkv_ref, bd_ref, qg_ref, kg_ref,
                         c0_ref, cup_ref, cdn_ref, mk_ref, mv_ref, mqg_ref, w1_ref, w2_ref,
                         q1t_ref, q2t_ref, k1_ref, k2_ref, vt_ref, mo_ref, w1bf_ref, w2bf_ref):
    w1bf_ref[...] = w1_ref[...].astype(BF16)
    w2bf_ref[...] = w2_ref[...].astype(BF16)
    x = x_ref[...]
    tm = x.shape[0]
    rstd = _rstd(x)
    xn_mix = (x * gmix_ref[...]).astype(BF16)
    xn_kv = (x * gkv_ref[...]).astype(BF16)
    bd = bd_ref[...]
    c0, cup, cdn = c0_ref[...], cup_ref[...], cdn_ref[...]
    qg = qg_ref[...] * (DIFF_HEAD_DIM ** -0.5 * math.log2(math.e))
    kg = kg_ref[...]
    pairs_per_chunk = MXU_DIM // LANES
    pad_rows = VT_ROWS - DIFF_V_DIM
    ones_row = (lax.broadcasted_iota(jnp.int32, (pad_rows, tm), 0) == 0).astype(BF16)
    mk, mv = mk_ref[0], mv_ref[0]

    def q_epilogue(qt_ref, c):
        def run(t):
            qt = _head_norm_rope(t, qg, bd, c0, cup, cdn).T.astype(BF16)
            for p in range(pairs_per_chunk):
                qt_ref[0, pairs_per_chunk * c + p, 0] = qt[p * LANES:(p + 1) * LANES]
        return run

    def k_epilogue(k_ref, c):
        def run(t):
            k_ref[:, c * MXU_DIM:(c + 1) * MXU_DIM] = _head_norm_rope(
                t, kg, bd, c0, cup, cdn).astype(BF16)
        return run

    def v_epilogue(c):
        def run(t):
            for p in range(pairs_per_chunk):
                h = pairs_per_chunk * c + p
                vt_ref[0, h, 0, :DIFF_V_DIM, :] = t[:, p * DIFF_V_DIM:(p + 1) * DIFF_V_DIM].T.astype(BF16)
                vt_ref[0, h, 0, DIFF_V_DIM:, :] = ones_row
        return run

    def mem_epilogue(c):
        def run(t):
            for p in range(pairs_per_chunk):
                h = pairs_per_chunk * c + p
                hs = slice(h * MEM_HEAD_DIM, (h + 1) * MEM_HEAD_DIM)
                o = _mem_attention_head(t[:, p * MEM_HEAD_DIM:(p + 1) * MEM_HEAD_DIM],
                                        mqg_ref[...], mk[:, hs], mv[:, hs])
                mo_ref[:, hs] = o.astype(BF16)
        return run

    chunks = []
    for c in range(DIFF_QK_WIDTH // MXU_DIM):
        chunks += [
            (xn_mix, win_ref, c * MXU_DIM, q_epilogue(q1t_ref, c)),
            (xn_kv, wkv_ref, c * MXU_DIM, k_epilogue(k1_ref, c)),
            (xn_kv, wkv_ref, 2 * DIFF_QK_WIDTH + 2 * c * MXU_DIM, v_epilogue(2 * c)),
            (xn_mix, win_ref, DIFF_QK_WIDTH + c * MXU_DIM, q_epilogue(q2t_ref, c)),
            (xn_kv, wkv_ref, DIFF_QK_WIDTH + c * MXU_DIM, k_epilogue(k2_ref, c)),
            (xn_kv, wkv_ref, 2 * DIFF_QK_WIDTH + (2 * c + 1) * MXU_DIM, v_epilogue(2 * c + 1)),
        ]
    for c in range(MEM_WIDTH // MXU_DIM):
        chunks.append((xn_mix, win_ref, 2 * DIFF_QK_WIDTH + c * MXU_DIM, mem_epilogue(c)))

    def project(chunk):
        xn, w_ref, col, _ = chunk
        return _dot(xn, w_ref[:, col:col + MXU_DIM])

    pending = project(chunks[0])
    for n, chunk in enumerate(chunks):
        ready = pending
        if n + 1 < len(chunks):
            pending = project(chunks[n + 1])
        chunk[3](ready * rstd)


def _layer_b_proj(x2d, gmix, gkv, win_bf, layer, wkv_bf, q_norm, k_norm, mk, mv, mqg,
                  w_ff1, w_ff2, batch, seq, tm, tq, tf):
    nt = seq // tm
    cast_in, cast_out, cast_shapes = _ffn_cast_specs(
        layer, batch * nt, lambda b, i: b * nt + i, tf)
    nq = seq // tq
    sub = tq // tm
    npairs = DIFF_QK_WIDTH // LANES
    m = x2d.shape[0]
    c0, cup, cdn = _rope_coeffs(seq)
    lane_head = jnp.arange(MXU_DIM) // DIFF_HEAD_DIM
    bd = (lane_head[:, None] == lane_head[None, :]).astype(BF16)
    qg = jnp.tile(q_norm.reshape(1, DIFF_HEAD_DIM), (1, MXU_DIM // DIFF_HEAD_DIM))
    kg = jnp.tile(k_norm.reshape(1, DIFF_HEAD_DIM), (1, MXU_DIM // DIFF_HEAD_DIM))
    const2 = lambda b, i: (0, 0)
    rows = lambda b, i: (b * nt + i, 0)
    qk_sd = jax.ShapeDtypeStruct((m, DIFF_QK_WIDTH), BF16)
    qt_sd = jax.ShapeDtypeStruct((batch, npairs, nq, LANES, tq), BF16)
    qt_spec = pl.BlockSpec((1, npairs, 1, LANES, tm), lambda b, i: (b, 0, i // sub, 0, i % sub))
    return pl.pallas_call(
        _layer_b_proj_kernel,
        grid=(batch, nt),
        in_specs=[
            pl.BlockSpec((tm, D_MODEL), rows),
            _resident((1, D_MODEL), const2),
            _resident((1, D_MODEL), const2),
            _resident((None, D_MODEL, D_MODEL), lambda b, i: (layer, 0, 0)),
            _resident((D_MODEL, KV_WIDTH), const2),
            _resident((MXU_DIM, MXU_DIM), const2),
            _resident((1, MXU_DIM), const2),
            _resident((1, MXU_DIM), const2),
            pl.BlockSpec((tm, LANES), lambda b, i: (i, 0)),
            pl.BlockSpec((tm, LANES), lambda b, i: (i, 0)),
            pl.BlockSpec((tm, LANES), lambda b, i: (i, 0)),
            pl.BlockSpec((1, MEM_LEN, MEM_WIDTH), lambda b, i: (layer, b, 0)),
            pl.BlockSpec((1, MEM_LEN, MEM_WIDTH), lambda b, i: (layer, b, 0)),
            _resident((1, MEM_HEAD_DIM), const2),
        ] + cast_in,
        out_specs=[
            qt_spec,
            qt_spec,
            pl.BlockSpec((tm, DIFF_QK_WIDTH), rows),
            pl.BlockSpec((tm, DIFF_QK_WIDTH), rows),
            pl.BlockSpec((1, DIFF_HEADS, 1, VT_ROWS, tm),
                         lambda b, i: (b, 0, i // sub, 0, i % sub)),
            pl.BlockSpec((tm, MEM_WIDTH), rows),
        ] + cast_out,
        out_shape=[qt_sd, qt_sd, qk_sd, qk_sd,
                   jax.ShapeDtypeStruct((batch, DIFF_HEADS, nq, VT_ROWS, tq), BF16),
                   jax.ShapeDtypeStruct((m, MEM_WIDTH), BF16)] + cast_shapes,
        compiler_params=pltpu.CompilerParams(
            dimension_semantics=("arbitrary", "arbitrary"),
            vmem_limit_bytes=VMEM_LIMIT_BYTES),
        name="layer_b_proj",
    )(x2d, gmix, gkv, win_bf, wkv_bf, bd, qg, kg, c0, cup, cdn, mk, mv, mqg, w_ff1, w_ff2)


def _diff_attn_kernel(q1t_ref, q2t_ref, k1_ref, k2_ref, vt_ref, lam_ref, sg_ref, o_ref,
                      s_ref, m_ref, acc_ref, *, tq, nq, pairs, lam_init):
    feat = lax.broadcasted_iota(jnp.int32, (LANES, 1), 0)
    heads_per_pair = LANES // DIFF_HEAD_DIM
    combos = [(pr, hh, mp) for pr in range(pairs) for hh in range(heads_per_pair)
              for mp in range(2)]
    in_head = [(feat >= hh * DIFF_HEAD_DIM) & (feat < (hh + 1) * DIFF_HEAD_DIM)
               for hh in range(heads_per_pair)]
    key_pos = lax.broadcasted_iota(jnp.int32, (tq, tq), 0)
    q_pos = lax.broadcasted_iota(jnp.int32, (tq, tq), 1)

    def scores(i, blk):
        r0 = pl.multiple_of(blk * tq, tq)
        out = []
        for pr, hh, mp in combos:
            qt = (q1t_ref, q2t_ref)[mp][0, pr, i]
            qm = jnp.where(in_head[hh], qt, jnp.zeros_like(qt))
            kblk = (k1_ref, k2_ref)[mp][pl.ds(r0, tq), pr * LANES:(pr + 1) * LANES]
            out.append(_dot(kblk, qm))
        return out

    def store_scores(sts, diagonal):
        for c, st in enumerate(sts):
            s_ref[c] = jnp.where(key_pos <= q_pos, st, NEG_INF) if diagonal else st

    def consume(blk, first):
        for c, (pr, hh, mp) in enumerate(combos):
            st = s_ref[c]
            m_old = jnp.where(first, NEG_INF, m_ref[c])
            m_new = jnp.maximum(m_old, jnp.max(st, axis=0, keepdims=True))
            alpha = jnp.exp2(m_old - m_new)
            p = jnp.exp2(st - m_new).astype(BF16)
            vtblk = vt_ref[0, heads_per_pair * pr + hh, blk]
            acc_ref[c] = alpha * acc_ref[c] + _dot(vtblk, p)
            m_ref[c] = m_new

    def finalize(i):
        lq = lam_ref[...]
        lam = (jnp.exp(jnp.sum(lq[0:1] * lq[1:2], axis=-1, keepdims=True))
               - jnp.exp(jnp.sum(lq[2:3] * lq[3:4], axis=-1, keepdims=True)) + lam_init)
        r0 = pl.multiple_of(i * tq, tq)
        for hh in range(heads_per_pair * pairs):
            a1, a2 = acc_ref[2 * hh], acc_ref[2 * hh + 1]
            ot = (a1[:DIFF_V_DIM] * (1.0 / a1[DIFF_V_DIM:DIFF_V_DIM + 1])
                  - lam * (a2[:DIFF_V_DIM] * (1.0 / a2[DIFF_V_DIM:DIFF_V_DIM + 1])))
            ot = ot * lax.rsqrt(jnp.mean(ot * ot, axis=0, keepdims=True) + EPS)
            o = ot.T * sg_ref[...] * (1.0 - lam_init)
            o_ref[pl.ds(r0, tq), hh * DIFF_V_DIM:(hh + 1) * DIFF_V_DIM] = o.astype(BF16)

    acc_ref[...] = jnp.zeros(acc_ref.shape, F32)
    m_ref[...] = jnp.full(m_ref.shape, NEG_INF, F32)
    store_scores(scores(0, 0), True)

    def body(t, carry):
        i, pos = carry
        last = pos == i
        ni = jnp.where(last, i + 1, i)
        npos = jnp.where(last, 0, pos + 1)
        blk = jnp.where(pos == 0, i, pos - 1)
        nblk = jnp.where(npos == 0, ni, npos - 1)

        @pl.when(last)
        def _():
            nxt = scores(ni, nblk)
            consume(blk, pos == 0)
            store_scores(nxt, True)
            finalize(i)

        @pl.when(jnp.logical_not(last))
        def _():
            nxt = scores(ni, nblk)
            consume(blk, pos == 0)
            store_scores(nxt, False)

        return ni, npos

    ntasks = nq * (nq + 1) // 2
    lax.fori_loop(0, ntasks - 1, body, (jnp.int32(0), jnp.int32(0)))
    consume(max(nq - 2, 0), nq == 1)
    finalize(nq - 1)


def _diff_attention(q1t, q2t, k1, k2, vt, diff_lambda, subln, batch, seq, tq, pairs, lam_init):
    nq = seq // tq
    m = k1.shape[0]
    npairs = DIFF_QK_WIDTH // LANES
    heads_per_step = pairs * (LANES // DIFF_HEAD_DIM)
    vw = heads_per_step * DIFF_V_DIM
    ncombo = 2 * heads_per_step
    qspec = pl.BlockSpec((1, pairs, nq, LANES, tq), lambda b, hp: (b, hp, 0, 0, 0))
    kspec = pl.BlockSpec((seq, pairs * LANES), lambda b, hp: (b, hp))
    return pl.pallas_call(
        functools.partial(_diff_attn_kernel, tq=tq, nq=nq, pairs=pairs, lam_init=lam_init),
        grid=(batch, npairs // pairs),
        in_specs=[
            qspec, qspec, kspec, kspec,
            pl.BlockSpec((1, heads_per_step, nq, VT_ROWS, tq), lambda b, hp: (b, hp, 0, 0, 0)),
            _resident((4, DIFF_HEAD_DIM), lambda b, hp: (0, 0)),
            _resident((1, DIFF_V_DIM), lambda b, hp: (0, 0)),
        ],
        out_specs=pl.BlockSpec((seq, vw), lambda b, hp: (b, hp)),
        out_shape=jax.ShapeDtypeStruct((m, DIFF_V_WIDTH), BF16),
        scratch_shapes=[
            pltpu.VMEM((ncombo, tq, tq), F32),
            pltpu.VMEM((ncombo, 1, tq), F32),
            pltpu.VMEM((ncombo, VT_ROWS, tq), F32),
        ],
        compiler_params=pltpu.CompilerParams(
            dimension_semantics=("arbitrary", "arbitrary"),
            vmem_limit_bytes=VMEM_LIMIT_BYTES),
        name="diff_attention",
    )(q1t, q2t, k1, k2, vt, diff_lambda, subln)


def _out_proj_kernel(tok_ref, mo_ref, x_ref, w_ref, o_ref):
    o_ref[...] = (x_ref[...] + _dot(tok_ref[...], w_ref[:DIFF_V_WIDTH, :])
                  + _dot(mo_ref[...], w_ref[DIFF_V_WIDTH:, :]))


def _out_proj(tok, mo, x2d, wout_bf, layer, tm):
    m = x2d.shape[0]
    rows = lambda i: (i, 0)
    return pl.pallas_call(
        _out_proj_kernel,
        grid=(m // tm,),
        in_specs=[
            pl.BlockSpec((tm, DIFF_V_WIDTH), rows),
            pl.BlockSpec((tm, MEM_WIDTH), rows),
            pl.BlockSpec((tm, D_MODEL), rows),
            _resident((None, D_MODEL, D_MODEL), lambda i: (layer, 0, 0)),
        ],
        out_specs=pl.BlockSpec((tm, D_MODEL), rows),
        out_shape=jax.ShapeDtypeStruct(x2d.shape, F32),
        compiler_params=pltpu.CompilerParams(
            dimension_semantics=("arbitrary",),
            vmem_limit_bytes=VMEM_LIMIT_BYTES),
        name="out_proj",
    )(tok, mo, x2d, wout_bf)


def kernel(x, mem, mix_norm, w_in, w_out, mem_norm, w_mem_kv, mem_q_norm, mem_k_norm,
           ffn_norm, w_ff1, w_ff2, pool_w, pool_scale, kv_norm, w_kv, k_norm, q_norm,
           diff_lambda, subln_norm):
    batch, seq, _ = x.shape
    m = batch * seq
    x2d = x.reshape(m, D_MODEL)
    mem2d = mem.reshape(batch * MEM_LEN, D_MODEL)

    mk, mv, w_in_bf, w_out_bf, w_kv_bf = _memkv(
        mem2d, mem_norm, w_mem_kv.astype(BF16), mem_k_norm, w_in, w_out, w_kv, batch)

    tf = 1024

    x2d, w1_bf, w2_bf = _layer_a(x2d, mix_norm[0:1], w_in_bf, w_out_bf, 0,
                                 pool_w[0].astype(BF16), pool_scale[0:1], mk, mv,
                                 mem_q_norm[0:1], w_ff1, w_ff2, batch, seq, tm=512, tf=tf)
    x2d = _ffn(x2d, ffn_norm[0:1], w1_bf, w2_bf, tm=512, tf=tf)

    lam_init = 0.8 - 0.6 * math.exp(-0.3 * 1)
    tq = 256
    q1t, q2t, k1, k2, vt, mo, w1_bf, w2_bf = _layer_b_proj(
        x2d, mix_norm[1:2], kv_norm.reshape(1, D_MODEL), w_in_bf, 1, w_kv_bf,
        q_norm[0], k_norm, mk, mv, mem_q_norm[1:2], w_ff1, w_ff2, batch, seq,
        tm=256, tq=tq, tf=tf)
    tok = _diff_attention(q1t, q2t, k1, k2, vt, diff_lambda[0], subln_norm[0:1],
                          batch, seq, tq=tq, pairs=2, lam_init=lam_init)
    x2d = _out_proj(tok, mo, x2d, w_out_bf, 1, tm=1024)
    x2d = _ffn(x2d, ffn_norm[1:2], w1_bf, w2_bf, tm=512, tf=tf)
    return x2d.reshape(batch, seq, D_MODEL)
```

```python
import functools
import math

import jax
import jax.numpy as jnp
from jax import lax
from jax.experimental import pallas as pl
from jax.experimental.pallas import tpu as pltpu

D_MODEL = 2048
MEM_LEN = 256
MEM_HEADS = 4
MEM_HEAD_DIM = 128
MEM_WIDTH = MEM_HEADS * MEM_HEAD_DIM
POOL_WIDTH = D_MODEL - MEM_WIDTH
POOL_WINDOWS = (2, 4, 8, 16)
POOL_GROUP = POOL_WIDTH // len(POOL_WINDOWS)
POOL_HALO = 16
DIFF_HEAD_DIM = 64
DIFF_V_DIM = 128
DIFF_HEADS = POOL_WIDTH // DIFF_V_DIM
DIFF_QK_WIDTH = DIFF_HEADS * DIFF_HEAD_DIM
DIFF_V_WIDTH = DIFF_HEADS * DIFF_V_DIM
VT_ROWS = DIFF_V_DIM + 16
KV_WIDTH = 2 * DIFF_QK_WIDTH + DIFF_V_WIDTH
ROPE_THETA = 500000.0
ROPE_DIM = DIFF_HEAD_DIM // 4
D_FF = 4 * D_MODEL
EPS = 1e-6
NEG_INF = -1e30

MAX_SAFE_SCORE_BOUND = 40.0

LANES = 128
SUBLANES = 8
MXU_DIM = 256
VMEM_LIMIT_BYTES = 56 * 1024 * 1024

BF16 = jnp.bfloat16
F32 = jnp.float32


def _resident(shape, index_map):
    return pl.BlockSpec(shape, index_map, pipeline_mode=pl.Buffered(1))


def _rstd(x):
    return lax.rsqrt(jnp.mean(x * x, axis=-1, keepdims=True) + EPS)


def _dot(a, b):
    return jnp.dot(a, b, preferred_element_type=F32)


def _dot_nt(a, b):
    return lax.dot_general(a, b, (((1,), (1,)), ((), ())), preferred_element_type=F32)


def _memkv_kernel(mem_ref, g_ref, w_ref, kg_ref, win_ref, wout_ref, wkv_ref,
                  k_ref, v_ref, winbf_ref, woutbf_ref, wkvbf_ref):
    winbf_ref[...] = win_ref[...].astype(BF16)
    woutbf_ref[...] = wout_ref[...].astype(BF16)
    wkvbf_ref[...] = wkv_ref[...].astype(BF16)
    x = mem_ref[...]
    xn = (x * _rstd(x) * g_ref[0]).astype(BF16)
    kv = _dot(xn, w_ref[0])
    for h in range(MEM_HEADS):
        hs = slice(h * MEM_HEAD_DIM, (h + 1) * MEM_HEAD_DIM)
        kh = kv[:, hs]
        k_ref[0, :, hs] = (kh * _rstd(kh) * kg_ref[0]).astype(BF16)
    v_ref[0] = kv[:, MEM_WIDTH:].astype(BF16)


def _memkv(mem2d, mem_norm, w_mem_kv_bf, mem_k_norm, w_in, w_out, w_kv, batch):
    depth = mem_norm.shape[0]
    nsteps = depth * batch
    step = lambda l, b: (l * batch + b, 0)
    win2d = w_in.reshape(depth * D_MODEL, D_MODEL)
    wout2d = w_out.reshape(depth * D_MODEL, D_MODEL)
    rows_io = depth * D_MODEL // nsteps
    rows_kv = D_MODEL // nsteps
    out_sd = jax.ShapeDtypeStruct((depth, batch * MEM_LEN, MEM_WIDTH), BF16)
    mk, mv, win_bf, wout_bf, wkv_bf = pl.pallas_call(
        _memkv_kernel,
        grid=(depth, batch),
        in_specs=[
            pl.BlockSpec((MEM_LEN, D_MODEL), lambda l, b: (b, 0)),
            pl.BlockSpec((1, 1, D_MODEL), lambda l, b: (l, 0, 0)),
            pl.BlockSpec((1, D_MODEL, 2 * MEM_WIDTH), lambda l, b: (l, 0, 0)),
            pl.BlockSpec((1, 1, MEM_HEAD_DIM), lambda l, b: (l, 0, 0)),
            pl.BlockSpec((rows_io, D_MODEL), step),
            pl.BlockSpec((rows_io, D_MODEL), step),
            pl.BlockSpec((rows_kv, KV_WIDTH), step),
        ],
        out_specs=[
            pl.BlockSpec((1, MEM_LEN, MEM_WIDTH), lambda l, b: (l, b, 0)),
            pl.BlockSpec((1, MEM_LEN, MEM_WIDTH), lambda l, b: (l, b, 0)),
            pl.BlockSpec((rows_io, D_MODEL), step),
            pl.BlockSpec((rows_io, D_MODEL), step),
            pl.BlockSpec((rows_kv, KV_WIDTH), step),
        ],
        out_shape=[out_sd, out_sd,
                   jax.ShapeDtypeStruct(win2d.shape, BF16),
                   jax.ShapeDtypeStruct(wout2d.shape, BF16),
                   jax.ShapeDtypeStruct(w_kv.shape, BF16)],
        compiler_params=pltpu.CompilerParams(
            dimension_semantics=("arbitrary", "arbitrary"),
            vmem_limit_bytes=VMEM_LIMIT_BYTES),
        name="mem_kv",
    )(mem2d, mem_norm.reshape(depth, 1, D_MODEL), w_mem_kv_bf,
      mem_k_norm.reshape(depth, 1, MEM_HEAD_DIM), win2d, wout2d, w_kv)
    return (mk, mv, win_bf.reshape(w_in.shape), wout_bf.reshape(w_out.shape), wkv_bf)


def _mem_attention_head(qh, qg, kh, vh):
    qn = (qh * _rstd(qh) * qg).astype(BF16)
    s = _dot_nt(qn, kh) * (MEM_HEAD_DIM ** -0.5)
    p = jnp.exp(s - jnp.max(s, axis=-1, keepdims=True))
    l = jnp.sum(p, axis=-1, keepdims=True)
    return _dot(p.astype(BF16), vh) / l


def _pool_bands():
    r = jnp.arange(MXU_DIM)[:, None]
    c = jnp.arange(MXU_DIM)[None, :]
    rh = jnp.arange(POOL_HALO)[:, None]
    ch = jnp.arange(POOL_HALO)[None, :]
    band = jnp.stack([((r - c >= 0) & (r - c < w)) for w in POOL_WINDOWS])
    bandh = jnp.stack([(rh + POOL_HALO - ch < w) for w in POOL_WINDOWS])
    return band.astype(BF16), bandh.astype(BF16)


def _ffn_cast_specs(layer, nsteps, step, tf):
    chunk = D_FF // nsteps
    per_block = tf // chunk
    in_specs = [pl.BlockSpec((None, D_MODEL, chunk), lambda *ix: (layer, 0, step(*ix))),
                pl.BlockSpec((None, chunk, D_MODEL), lambda *ix: (layer, step(*ix), 0))]
    out_specs = [pl.BlockSpec((None, D_MODEL, chunk),
                              lambda *ix: (step(*ix) // per_block, 0, step(*ix) % per_block)),
                 pl.BlockSpec((chunk, D_MODEL), lambda *ix: (step(*ix), 0))]
    out_shapes = [jax.ShapeDtypeStruct((D_FF // tf, D_MODEL, tf), BF16),
                  jax.ShapeDtypeStruct((D_FF, D_MODEL), BF16)]
    return in_specs, out_specs, out_shapes


def _layer_a_kernel(x_ref, xh_ref, g_ref, win_ref, band_ref, bandh_ref, poolw_ref, pscale_ref,
                    mk_ref, mv_ref, mqg_ref, wout_ref, w1_ref, w2_ref,
                    o_ref, w1bf_ref, w2bf_ref, cat_ref, *, tm):
    w1bf_ref[...] = w1_ref[...].astype(BF16)
    w2bf_ref[...] = w2_ref[...].astype(BF16)
    i = pl.program_id(1)
    x = x_ref[...]
    rstd = _rstd(x)
    xn = (x * g_ref[...]).astype(BF16)

    xh = xh_ref[...]
    uh = _dot((xh * g_ref[...]).astype(BF16), win_ref[:, :POOL_WIDTH]) * _rstd(xh)
    uh = jnp.where(i == 0, 0.0, uh).astype(BF16)

    row = lax.broadcasted_iota(jnp.int32, (MXU_DIM, 1), 0)
    half_width = 2 * POOL_GROUP

    def pool_epilogue(u_half, half):
        ub = u_half.astype(BF16)
        for gl in range(2):
            g = 2 * half + gl
            w = POOL_WINDOWS[g]
            ls = slice(gl * POOL_GROUP, (gl + 1) * POOL_GROUP)
            cs = slice(g * POOL_GROUP, (g + 1) * POOL_GROUP)
            for sb in range(tm // MXU_DIM):
                r0 = sb * MXU_DIM
                rs = slice(r0, r0 + MXU_DIM)
                main = _dot(band_ref[g], ub[rs, ls])
                halo = uh[:, cs] if sb == 0 else ub[r0 - POOL_HALO:r0, ls]
                top = _dot(bandh_ref[g], halo)
                wsum = jnp.concatenate([main[:POOL_HALO] + top, main[POOL_HALO:]], axis=0)
                t = i * tm + r0 + row
                count = jnp.minimum(t + 1, w).astype(F32)
                pooled = wsum / count - u_half[rs, ls]
                tok = _dot(pooled.astype(BF16), poolw_ref[g]) * pscale_ref[:, cs]
                cat_ref[rs, cs] = tok.astype(BF16)

    def out_partial(c0, c1):
        return _dot(cat_ref[:, c0:c1], wout_ref[c0:c1, :])

    uq = _dot(xn, win_ref[:, POOL_WIDTH:]) * rstd
    u0 = _dot(xn, win_ref[:, :half_width])
    mk, mv = mk_ref[0], mv_ref[0]
    for h in range(MEM_HEADS):
        hs = slice(h * MEM_HEAD_DIM, (h + 1) * MEM_HEAD_DIM)
        o = _mem_attention_head(uq[:, hs], mqg_ref[...], mk[:, hs], mv[:, hs])
        cat_ref[:, POOL_WIDTH + h * MEM_HEAD_DIM:POOL_WIDTH + (h + 1) * MEM_HEAD_DIM] = o.astype(BF16)
    u1 = _dot(xn, win_ref[:, half_width:POOL_WIDTH])
    pool_epilogue(u0 * rstd, 0)
    acc = x + out_partial(POOL_WIDTH, D_MODEL)
    pool_epilogue(u1 * rstd, 1)
    o_ref[...] = acc + out_partial(0, POOL_WIDTH)


def _layer_a(x2d, g, win_bf, wout_bf, layer, poolw_bf, pscale, mk, mv, mqg, w_ff1, w_ff2,
             batch, seq, tm, tf):
    nt = seq // tm
    wslab = lambda b, i: (layer, 0, 0)
    cast_in, cast_out, cast_shapes = _ffn_cast_specs(
        layer, batch * nt, lambda b, i: b * nt + i, tf)
    band, bandh = _pool_bands()
    ngrp = len(POOL_WINDOWS)
    const2 = lambda b, i: (0, 0)
    const3 = lambda b, i: (0, 0, 0)
    halo_blocks = tm // POOL_HALO
    return pl.pallas_call(
        functools.partial(_layer_a_kernel, tm=tm),
        grid=(batch, nt),
        in_specs=[
            pl.BlockSpec((tm, D_MODEL), lambda b, i: (b * nt + i, 0)),
            pl.BlockSpec((POOL_HALO, D_MODEL),
                         lambda b, i: (jnp.maximum((b * nt + i) * halo_blocks - 1, 0), 0)),
            _resident((1, D_MODEL), const2),
            _resident((None, D_MODEL, D_MODEL), wslab),
            _resident((ngrp, MXU_DIM, MXU_DIM), const3),
            _resident((ngrp, POOL_HALO, POOL_HALO), const3),
            _resident((ngrp, POOL_GROUP, POOL_GROUP), const3),
            _resident((1, POOL_WIDTH), const2),
            pl.BlockSpec((1, MEM_LEN, MEM_WIDTH), lambda b, i: (layer, b, 0)),
            pl.BlockSpec((1, MEM_LEN, MEM_WIDTH), lambda b, i: (layer, b, 0)),
            _resident((1, MEM_HEAD_DIM), const2),
            _resident((None, D_MODEL, D_MODEL), wslab),
        ] + cast_in,
        out_specs=[pl.BlockSpec((tm, D_MODEL), lambda b, i: (b * nt + i, 0))] + cast_out,
        out_shape=[jax.ShapeDtypeStruct(x2d.shape, F32)] + cast_shapes,
        scratch_shapes=[pltpu.VMEM((tm, D_MODEL), BF16)],
        compiler_params=pltpu.CompilerParams(
            dimension_semantics=("arbitrary", "arbitrary"),
            vmem_limit_bytes=VMEM_LIMIT_BYTES),
        name="layer_a_mixer",
    )(x2d, x2d, g, win_bf, band, bandh, poolw_bf, pscale, mk, mv, mqg, wout_bf, w_ff1, w_ff2)


def _ffn_kernel(x_ref, g_ref, w1_ref, w2_ref, o_ref, xg_ref, r2_ref, *, nk):
    k = pl.program_id(1)

    def partial(xg):
        z = jnp.maximum(_dot(xg, w1_ref[...]), 0.0)
        return _dot((z * z).astype(BF16), w2_ref[...])

    @pl.when(k == 0)
    def _():
        x = x_ref[...]
        xg = (x * g_ref[...]).astype(BF16)
        xg_ref[...] = xg
        r2_ref[...] = 1.0 / (jnp.mean(x * x, axis=-1, keepdims=True) + EPS)
        o_ref[...] = partial(xg)

    @pl.when(jnp.logical_and(k > 0, k < nk - 1))
    def _():
        o_ref[...] += partial(xg_ref[...])

    @pl.when(k == nk - 1)
    def _():
        o_ref[...] = x_ref[...] + r2_ref[...] * (o_ref[...] + partial(xg_ref[...]))


def _ffn(x2d, g, w1_bf, w2_bf, tm, tf):
    m = x2d.shape[0]
    assert w1_bf.shape == (D_FF // tf, D_MODEL, tf)
    nk = D_FF // tf
    assert nk >= 2
    return pl.pallas_call(
        functools.partial(_ffn_kernel, nk=nk),
        grid=(m // tm, nk),
        in_specs=[
            pl.BlockSpec((tm, D_MODEL), lambda i, k: (i, 0)),
            _resident((1, D_MODEL), lambda i, k: (0, 0)),
            pl.BlockSpec((None, D_MODEL, tf), lambda i, k: (k, 0, 0)),
            pl.BlockSpec((tf, D_MODEL), lambda i, k: (k, 0)),
        ],
        out_specs=pl.BlockSpec((tm, D_MODEL), lambda i, k: (i, 0)),
        out_shape=jax.ShapeDtypeStruct(x2d.shape, F32),
        scratch_shapes=[pltpu.VMEM((tm, D_MODEL), BF16), pltpu.VMEM((tm, 1), F32)],
        compiler_params=pltpu.CompilerParams(
            dimension_semantics=("arbitrary", "arbitrary"),
            vmem_limit_bytes=VMEM_LIMIT_BYTES),
        name="ffn",
    )(x2d, g, w1_bf, w2_bf)


def _rope_coeffs(seq):
    half = ROPE_DIM // 2
    pos = jnp.arange(seq, dtype=F32)
    inv = ROPE_THETA ** (-(jnp.arange(half, dtype=F32) * 2.0) / ROPE_DIM)
    ang = pos[:, None] * inv[None, :]
    cos, sin = jnp.cos(ang), jnp.sin(ang)
    ones = jnp.ones((seq, DIFF_HEAD_DIM - ROPE_DIM), F32)
    zeros = jnp.zeros((seq, DIFF_HEAD_DIM - ROPE_DIM), F32)
    zh = jnp.zeros((seq, half), F32)
    c0 = jnp.concatenate([cos, cos, ones], axis=-1)
    c_up = jnp.concatenate([-sin, zh, zeros], axis=-1)
    c_dn = jnp.concatenate([zh, sin, zeros], axis=-1)
    rep = LANES // DIFF_HEAD_DIM
    return tuple(jnp.tile(c, (1, rep)) for c in (c0, c_up, c_dn))


def _head_norm_rope(t, gain, bd, c0, c_up, c_dn):
    ss = _dot((t * t).astype(BF16), bd)
    tn = t * lax.rsqrt(ss * (1.0 / DIFF_HEAD_DIM) + EPS) * gain
    outs = []
    half = ROPE_DIM // 2
    for p in range(MXU_DIM // LANES):
        tp = tn[:, p * LANES:(p + 1) * LANES]
        up = pltpu.roll(tp, LANES - half, axis=1)
        dn = pltpu.roll(tp, half, axis=1)
        outs.append(tp * c0 + up * c_up + dn * c_dn)
    return jnp.concatenate(outs, axis=-1)


def _layer_b_proj_kernel(x_ref, gmix_ref, gkv_ref, win_ref, wkv_ref, bd_ref, qg_ref, kg_ref,
                         c0_ref, cup_ref, cdn_ref, mk_ref, mv_ref, mqg_ref, w1_ref, w2_ref,
                         q1t_ref, q2t_ref, k1_ref, k2_ref, vt_ref, mo_ref, w1bf_ref, w2bf_ref):
    w1bf_ref[...] = w1_ref[...].astype(BF16)
    w2bf_ref[...] = w2_ref[...].astype(BF16)
    x = x_ref[...]
    tm = x.shape[0]
    rstd = _rstd(x)
    xn_mix = (x * gmix_ref[...]).astype(BF16)
    xn_kv = (x * gkv_ref[...]).astype(BF16)
    bd = bd_ref[...]
    c0, cup, cdn = c0_ref[...], cup_ref[...], cdn_ref[...]
    qg = qg_ref[...] * (DIFF_HEAD_DIM ** -0.5 * math.log2(math.e))
    kg = kg_ref[...]
    pairs_per_chunk = MXU_DIM // LANES
    pad_rows = VT_ROWS - DIFF_V_DIM
    ones_row = (lax.broadcasted_iota(jnp.int32, (pad_rows, tm), 0) == 0).astype(BF16)
    mk, mv = mk_ref[0], mv_ref[0]

    def q_epilogue(qt_ref, c):
        def run(t):
            qt = _head_norm_rope(t, qg, bd, c0, cup, cdn).T.astype(BF16)
            for p in range(pairs_per_chunk):
                qt_ref[0, pairs_per_chunk * c + p, 0] = qt[p * LANES:(p + 1) * LANES]
        return run

    def k_epilogue(k_ref, c):
        def run(t):
            k_ref[:, c * MXU_DIM:(c + 1) * MXU_DIM] = _head_norm_rope(
                t, kg, bd, c0, cup, cdn).astype(BF16)
        return run

    def v_epilogue(c):
        def run(t):
            for p in range(pairs_per_chunk):
                h = pairs_per_chunk * c + p
                vt_ref[0, h, 0, :DIFF_V_DIM, :] = t[:, p * DIFF_V_DIM:(p + 1) * DIFF_V_DIM].T.astype(BF16)
                vt_ref[0, h, 0, DIFF_V_DIM:, :] = ones_row
        return run

    def mem_epilogue(c):
        def run(t):
            for p in range(pairs_per_chunk):
                h = pairs_per_chunk * c + p
                hs = slice(h * MEM_HEAD_DIM, (h + 1) * MEM_HEAD_DIM)
                o = _mem_attention_head(t[:, p * MEM_HEAD_DIM:(p + 1) * MEM_HEAD_DIM],
                                        mqg_ref[...], mk[:, hs], mv[:, hs])
                mo_ref[:, hs] = o.astype(BF16)
        return run

    chunks = []
    for c in range(DIFF_QK_WIDTH // MXU_DIM):
        chunks += [
            (xn_mix, win_ref, c * MXU_DIM, q_epilogue(q1t_ref, c)),
            (xn_kv, wkv_ref, c * MXU_DIM, k_epilogue(k1_ref, c)),
            (xn_kv, wkv_ref, 2 * DIFF_QK_WIDTH + 2 * c * MXU_DIM, v_epilogue(2 * c)),
            (xn_mix, win_ref, DIFF_QK_WIDTH + c * MXU_DIM, q_epilogue(q2t_ref, c)),
            (xn_kv, wkv_ref, DIFF_QK_WIDTH + c * MXU_DIM, k_epilogue(k2_ref, c)),
            (xn_kv, wkv_ref, 2 * DIFF_QK_WIDTH + (2 * c + 1) * MXU_DIM, v_epilogue(2 * c + 1)),
        ]
    for c in range(MEM_WIDTH // MXU_DIM):
        chunks.append((xn_mix, win_ref, 2 * DIFF_QK_WIDTH + c * MXU_DIM, mem_epilogue(c)))

    def project(chunk):
        xn, w_ref, col, _ = chunk
        return _dot(xn, w_ref[:, col:col + MXU_DIM])

    pending = project(chunks[0])
    for n, chunk in enumerate(chunks):
        ready = pending
        if n + 1 < len(chunks):
            pending = project(chunks[n + 1])
        chunk[3](ready * rstd)


def _layer_b_proj(x2d, gmix, gkv, win_bf, layer, wkv_bf, q_norm, k_norm, mk, mv, mqg,
                  w_ff1, w_ff2, batch, seq, tm, tq, tf):
    nt = seq // tm
    cast_in, cast_out, cast_shapes = _ffn_cast_specs(
        layer, batch * nt, lambda b, i: b * nt + i, tf)
    nq = seq // tq
    sub = tq // tm
    npairs = DIFF_QK_WIDTH // LANES
    m = x2d.shape[0]
    c0, cup, cdn = _rope_coeffs(seq)
    lane_head = jnp.arange(MXU_DIM) // DIFF_HEAD_DIM
    bd = (lane_head[:, None] == lane_head[None, :]).astype(BF16)
    qg = jnp.tile(q_norm.reshape(1, DIFF_HEAD_DIM), (1, MXU_DIM // DIFF_HEAD_DIM))
    kg = jnp.tile(k_norm.reshape(1, DIFF_HEAD_DIM), (1, MXU_DIM // DIFF_HEAD_DIM))
    const2 = lambda b, i: (0, 0)
    rows = lambda b, i: (b * nt + i, 0)
    qk_sd = jax.ShapeDtypeStruct((m, DIFF_QK_WIDTH), BF16)
    qt_sd = jax.ShapeDtypeStruct((batch, npairs, nq, LANES, tq), BF16)
    qt_spec = pl.BlockSpec((1, npairs, 1, LANES, tm), lambda b, i: (b, 0, i // sub, 0, i % sub))
    return pl.pallas_call(
        _layer_b_proj_kernel,
        grid=(batch, nt),
        in_specs=[
            pl.BlockSpec((tm, D_MODEL), rows),
            _resident((1, D_MODEL), const2),
            _resident((1, D_MODEL), const2),
            _resident((None, D_MODEL, D_MODEL), lambda b, i: (layer, 0, 0)),
            _resident((D_MODEL, KV_WIDTH), const2),
            _resident((MXU_DIM, MXU_DIM), const2),
            _resident((1, MXU_DIM), const2),
            _resident((1, MXU_DIM), const2),
            pl.BlockSpec((tm, LANES), lambda b, i: (i, 0)),
            pl.BlockSpec((tm, LANES), lambda b, i: (i, 0)),
            pl.BlockSpec((tm, LANES), lambda b, i: (i, 0)),
            pl.BlockSpec((1, MEM_LEN, MEM_WIDTH), lambda b, i: (layer, b, 0)),
            pl.BlockSpec((1, MEM_LEN, MEM_WIDTH), lambda b, i: (layer, b, 0)),
            _resident((1, MEM_HEAD_DIM), const2),
        ] + cast_in,
        out_specs=[
            qt_spec,
            qt_spec,
            pl.BlockSpec((tm, DIFF_QK_WIDTH), rows),
            pl.BlockSpec((tm, DIFF_QK_WIDTH), rows),
            pl.BlockSpec((1, DIFF_HEADS, 1, VT_ROWS, tm),
                         lambda b, i: (b, 0, i // sub, 0, i % sub)),
            pl.BlockSpec((tm, MEM_WIDTH), rows),
        ] + cast_out,
        out_shape=[qt_sd, qt_sd, qk_sd, qk_sd,
                   jax.ShapeDtypeStruct((batch, DIFF_HEADS, nq, VT_ROWS, tq), BF16),
                   jax.ShapeDtypeStruct((m, MEM_WIDTH), BF16)] + cast_shapes,
        compiler_params=pltpu.CompilerParams(
            dimension_semantics=("arbitrary", "arbitrary"),
            vmem_limit_bytes=VMEM_LIMIT_BYTES),
        name="layer_b_proj",
    )(x2d, gmix, gkv, win_bf, wkv_bf, bd, qg, kg, c0, cup, cdn, mk, mv, mqg, w_ff1, w_ff2)


def _diff_attn_kernel(q1t_ref, q2t_ref, k1_ref, k2_ref, vt_ref, lam_ref, sg_ref, o_ref,
                      s_ref, m_ref, acc_ref, b_ref, *, tq, nq, pairs, lam_init):
    feat = lax.broadcasted_iota(jnp.int32, (LANES, 1), 0)
    heads_per_pair = LANES // DIFF_HEAD_DIM
    combos = [(pr, hh, mp) for pr in range(pairs) for hh in range(heads_per_pair)
              for mp in range(2)]
    in_head = [(feat >= hh * DIFF_HEAD_DIM) & (feat < (hh + 1) * DIFF_HEAD_DIM)
               for hh in range(heads_per_pair)]
    key_pos = lax.broadcasted_iota(jnp.int32, (tq, tq), 0)
    q_pos = lax.broadcasted_iota(jnp.int32, (tq, tq), 1)

    def scores(i, blk):
        r0 = pl.multiple_of(blk * tq, tq)
        out = []
        for pr, hh, mp in combos:
            qt = (q1t_ref, q2t_ref)[mp][0, pr, i]
            qm = jnp.where(in_head[hh], qt, jnp.zeros_like(qt))
            kblk = (k1_ref, k2_ref)[mp][pl.ds(r0, tq), pr * LANES:(pr + 1) * LANES]
            out.append(_dot(kblk, qm))
        return out

    def store_scores(sts, diagonal):
        for c, st in enumerate(sts):
            s_ref[c] = jnp.where(key_pos <= q_pos, st, NEG_INF) if diagonal else st

    def score_bounds():
        lane = lax.broadcasted_iota(jnp.int32, (1, LANES), 1)
        sub = lax.broadcasted_iota(jnp.int32, (2 * SUBLANES, LANES), 0)
        largest = jnp.zeros((1, tq), F32)
        for pr in range(pairs):
            for mp in range(2):
                kall = (k1_ref, k2_ref)[mp][:, pr * LANES:(pr + 1) * LANES]
                kmax = jnp.max(jnp.abs(kall.astype(F32)), axis=0, keepdims=True)
                lhs = jnp.zeros((2 * SUBLANES, LANES), F32)
                for hh in range(heads_per_pair):
                    head_lanes = (lane >= hh * DIFF_HEAD_DIM) & (lane < (hh + 1) * DIFF_HEAD_DIM)
                    lhs = jnp.where((sub == hh) & head_lanes, kmax, lhs)
                lhs = lhs.astype(BF16)
                for i in range(nq):
                    qabs = jnp.abs((q1t_ref, q2t_ref)[mp][0, pr, i])
                    bound = _dot(lhs, qabs)
                    for hh in range(heads_per_pair):
                        c = 2 * (heads_per_pair * pr + hh) + mp
                        b_ref[i, c] = bound[hh:hh + 1]
                        largest = jnp.maximum(largest, bound[hh:hh + 1])
        return jnp.max(largest)

    def run(fast):
        def consume(i, blk, first):
            for c, (pr, hh, mp) in enumerate(combos):
                st = s_ref[c]
                vtblk = vt_ref[0, heads_per_pair * pr + hh, blk]
                if fast:
                    p = jnp.exp2(st - b_ref[i, c]).astype(BF16)
                    acc_ref[c] += _dot(vtblk, p)
                else:
                    m_old = jnp.where(first, NEG_INF, m_ref[c])
                    m_new = jnp.maximum(m_old, jnp.max(st, axis=0, keepdims=True))
                    alpha = jnp.exp2(m_old - m_new)
                    p = jnp.exp2(st - m_new).astype(BF16)
                    acc_ref[c] = alpha * acc_ref[c] + _dot(vtblk, p)
                    m_ref[c] = m_new

        def finalize(i):
            lq = lam_ref[...]
            lam = (jnp.exp(jnp.sum(lq[0:1] * lq[1:2], axis=-1, keepdims=True))
                   - jnp.exp(jnp.sum(lq[2:3] * lq[3:4], axis=-1, keepdims=True)) + lam_init)
            r0 = pl.multiple_of(i * tq, tq)
            for hh in range(heads_per_pair * pairs):
                a1, a2 = acc_ref[2 * hh], acc_ref[2 * hh + 1]
                ot = (a1[:DIFF_V_DIM] * (1.0 / a1[DIFF_V_DIM:DIFF_V_DIM + 1])
                      - lam * (a2[:DIFF_V_DIM] * (1.0 / a2[DIFF_V_DIM:DIFF_V_DIM + 1])))
                ot = ot * lax.rsqrt(jnp.mean(ot * ot, axis=0, keepdims=True) + EPS)
                o = ot.T * sg_ref[...] * (1.0 - lam_init)
                o_ref[pl.ds(r0, tq), hh * DIFF_V_DIM:(hh + 1) * DIFF_V_DIM] = o.astype(BF16)
            if fast:
                acc_ref[...] = jnp.zeros(acc_ref.shape, F32)

        store_scores(scores(0, 0), True)

        def body(t, carry):
            i, pos = carry
            last = pos == i
            ni = jnp.where(last, i + 1, i)
            npos = jnp.where(last, 0, pos + 1)
            blk = jnp.where(pos == 0, i, pos - 1)
            nblk = jnp.where(npos == 0, ni, npos - 1)

            @pl.when(last)
            def _():
                nxt = scores(ni, nblk)
                consume(i, blk, pos == 0)
                store_scores(nxt, True)
                finalize(i)

            @pl.when(jnp.logical_not(last))
            def _():
                nxt = scores(ni, nblk)
                consume(i, blk, pos == 0)
                store_scores(nxt, False)

            return ni, npos

        ntasks = nq * (nq + 1) // 2
        lax.fori_loop(0, ntasks - 1, body, (jnp.int32(0), jnp.int32(0)))
        consume(nq - 1, max(nq - 2, 0), nq == 1)
        finalize(nq - 1)

    acc_ref[...] = jnp.zeros(acc_ref.shape, F32)
    m_ref[...] = jnp.full(m_ref.shape, NEG_INF, F32)
    use_bounds = score_bounds() <= MAX_SAFE_SCORE_BOUND

    @pl.when(use_bounds)
    def _():
        run(True)

    @pl.when(jnp.logical_not(use_bounds))
    def _():
        run(False)


def _diff_attention(q1t, q2t, k1, k2, vt, diff_lambda, subln, batch, seq, tq, pairs, lam_init):
    nq = seq // tq
    m = k1.shape[0]
    npairs = DIFF_QK_WIDTH // LANES
    heads_per_step = pairs * (LANES // DIFF_HEAD_DIM)
    vw = heads_per_step * DIFF_V_DIM
    ncombo = 2 * heads_per_step
    qspec = pl.BlockSpec((1, pairs, nq, LANES, tq), lambda b, hp: (b, hp, 0, 0, 0))
    kspec = pl.BlockSpec((seq, pairs * LANES), lambda b, hp: (b, hp))
    return pl.pallas_call(
        functools.partial(_diff_attn_kernel, tq=tq, nq=nq, pairs=pairs, lam_init=lam_init),
        grid=(batch, npairs // pairs),
        in_specs=[
            qspec, qspec, kspec, kspec,
            pl.BlockSpec((1, heads_per_step, nq, VT_ROWS, tq), lambda b, hp: (b, hp, 0, 0, 0)),
            _resident((4, DIFF_HEAD_DIM), lambda b, hp: (0, 0)),
            _resident((1, DIFF_V_DIM), lambda b, hp: (0, 0)),
        ],
        out_specs=pl.BlockSpec((seq, vw), lambda b, hp: (b, hp)),
        out_shape=jax.ShapeDtypeStruct((m, DIFF_V_WIDTH), BF16),
        scratch_shapes=[
            pltpu.VMEM((ncombo, tq, tq), F32),
            pltpu.VMEM((ncombo, 1, tq), F32),
            pltpu.VMEM((ncombo, VT_ROWS, tq), F32),
            pltpu.VMEM((nq, ncombo, 1, tq), F32),
        ],
        compiler_params=pltpu.CompilerParams(
            dimension_semantics=("arbitrary", "arbitrary"),
            vmem_limit_bytes=VMEM_LIMIT_BYTES),
        name="diff_attention",
    )(q1t, q2t, k1, k2, vt, diff_lambda, subln)


def _out_proj_kernel(tok_ref, mo_ref, x_ref, w_ref, o_ref):
    o_ref[...] = (x_ref[...] + _dot(tok_ref[...], w_ref[:DIFF_V_WIDTH, :])
                  + _dot(mo_ref[...], w_ref[DIFF_V_WIDTH:, :]))


def _out_proj(tok, mo, x2d, wout_bf, layer, tm):
    m = x2d.shape[0]
    rows = lambda i: (i, 0)
    return pl.pallas_call(
        _out_proj_kernel,
        grid=(m // tm,),
        in_specs=[
            pl.BlockSpec((tm, DIFF_V_WIDTH), rows),
            pl.BlockSpec((tm, MEM_WIDTH), rows),
            pl.BlockSpec((tm, D_MODEL), rows),
            _resident((None, D_MODEL, D_MODEL), lambda i: (layer, 0, 0)),
        ],
        out_specs=pl.BlockSpec((tm, D_MODEL), rows),
        out_shape=jax.ShapeDtypeStruct(x2d.shape, F32),
        compiler_params=pltpu.CompilerParams(
            dimension_semantics=("arbitrary",),
            vmem_limit_bytes=VMEM_LIMIT_BYTES),
        name="out_proj",
    )(tok, mo, x2d, wout_bf)


def kernel(x, mem, mix_norm, w_in, w_out, mem_norm, w_mem_kv, mem_q_norm, mem_k_norm,
           ffn_norm, w_ff1, w_ff2, pool_w, pool_scale, kv_norm, w_kv, k_norm, q_norm,
           diff_lambda, subln_norm):
    batch, seq, _ = x.shape
    m = batch * seq
    x2d = x.reshape(m, D_MODEL)
    mem2d = mem.reshape(batch * MEM_LEN, D_MODEL)

    mk, mv, w_in_bf, w_out_bf, w_kv_bf = _memkv(
        mem2d, mem_norm, w_mem_kv.astype(BF16), mem_k_norm, w_in, w_out, w_kv, batch)

    tf = 1024

    x2d, w1_bf, w2_bf = _layer_a(x2d, mix_norm[0:1], w_in_bf, w_out_bf, 0,
                                 pool_w[0].astype(BF16), pool_scale[0:1], mk, mv,
                                 mem_q_norm[0:1], w_ff1, w_ff2, batch, seq, tm=512, tf=tf)
    x2d = _ffn(x2d, ffn_norm[0:1], w1_bf, w2_bf, tm=512, tf=tf)

    lam_init = 0.8 - 0.6 * math.exp(-0.3 * 1)
    tq = 256
    q1t, q2t, k1, k2, vt, mo, w1_bf, w2_bf = _layer_b_proj(
        x2d, mix_norm[1:2], kv_norm.reshape(1, D_MODEL), w_in_bf, 1, w_kv_bf,
        q_norm[0], k_norm, mk, mv, mem_q_norm[1:2], w_ff1, w_ff2, batch, seq,
        tm=256, tq=tq, tf=tf)
    tok = _diff_attention(q1t, q2t, k1, k2, vt, diff_lambda[0], subln_norm[0:1],
                          batch, seq, tq=tq, pairs=2, lam_init=lam_init)
    x2d = _out_proj(tok, mo, x2d, w_out_bf, 1, tm=1024)
    x2d = _ffn(x2d, ffn_norm[1:2], w1_bf, w2_bf, tm=512, tf=tf)
    return x2d.reshape(batch, seq, D_MODEL)
```

```python
import functools
import math

import jax
import jax.numpy as jnp
from jax import lax
from jax.experimental import pallas as pl
from jax.experimental.pallas import tpu as pltpu

D_MODEL = 2048
MEM_LEN = 256
MEM_HEADS = 4
MEM_HEAD_DIM = 128
MEM_WIDTH = MEM_HEADS * MEM_HEAD_DIM
POOL_WIDTH = D_MODEL - MEM_WIDTH
POOL_WINDOWS = (2, 4, 8, 16)
POOL_GROUP = POOL_WIDTH // len(POOL_WINDOWS)
POOL_HALO = 16
DIFF_HEAD_DIM = 64
DIFF_V_DIM = 128
DIFF_HEADS = POOL_WIDTH // DIFF_V_DIM
DIFF_QK_WIDTH = DIFF_HEADS * DIFF_HEAD_DIM
DIFF_V_WIDTH = DIFF_HEADS * DIFF_V_DIM
VT_ROWS = DIFF_V_DIM + 16
KV_WIDTH = 2 * DIFF_QK_WIDTH + DIFF_V_WIDTH
ROPE_THETA = 500000.0
ROPE_DIM = DIFF_HEAD_DIM // 4
D_FF = 4 * D_MODEL
EPS = 1e-6
NEG_INF = -1e30

MAX_SAFE_SCORE_BOUND = 40.0

LANES = 128
SUBLANES = 8
MXU_DIM = 256
VMEM_LIMIT_BYTES = 56 * 1024 * 1024

BF16 = jnp.bfloat16
F32 = jnp.float32


def _resident(shape, index_map):
    return pl.BlockSpec(shape, index_map, pipeline_mode=pl.Buffered(1))


def _rstd(x):
    return lax.rsqrt(jnp.mean(x * x, axis=-1, keepdims=True) + EPS)


def _dot(a, b):
    return jnp.dot(a, b, preferred_element_type=F32)


def _dot_nt(a, b):
    return lax.dot_general(a, b, (((1,), (1,)), ((), ())), preferred_element_type=F32)


def _memkv_kernel(mem_ref, g_ref, w_ref, kg_ref, win_ref, wout_ref, wkv_ref,
                  k_ref, v_ref, winbf_ref, woutbf_ref, wkvbf_ref):
    winbf_ref[...] = win_ref[...].astype(BF16)
    woutbf_ref[...] = wout_ref[...].astype(BF16)
    wkvbf_ref[...] = wkv_ref[...].astype(BF16)
    x = mem_ref[...]
    xn = (x * _rstd(x) * g_ref[0]).astype(BF16)
    kv = _dot(xn, w_ref[0])
    for h in range(MEM_HEADS):
        hs = slice(h * MEM_HEAD_DIM, (h + 1) * MEM_HEAD_DIM)
        kh = kv[:, hs]
        k_ref[0, :, hs] = (kh * _rstd(kh) * kg_ref[0]).astype(BF16)
    v_ref[0] = kv[:, MEM_WIDTH:].astype(BF16)


def _memkv(mem2d, mem_norm, w_mem_kv_bf, mem_k_norm, w_in, w_out, w_kv, batch):
    depth = mem_norm.shape[0]
    nsteps = depth * batch
    step = lambda l, b: (l * batch + b, 0)
    win2d = w_in.reshape(depth * D_MODEL, D_MODEL)
    wout2d = w_out.reshape(depth * D_MODEL, D_MODEL)
    rows_io = depth * D_MODEL // nsteps
    rows_kv = D_MODEL // nsteps
    out_sd = jax.ShapeDtypeStruct((depth, batch * MEM_LEN, MEM_WIDTH), BF16)
    mk, mv, win_bf, wout_bf, wkv_bf = pl.pallas_call(
        _memkv_kernel,
        grid=(depth, batch),
        in_specs=[
            pl.BlockSpec((MEM_LEN, D_MODEL), lambda l, b: (b, 0)),
            pl.BlockSpec((1, 1, D_MODEL), lambda l, b: (l, 0, 0)),
            pl.BlockSpec((1, D_MODEL, 2 * MEM_WIDTH), lambda l, b: (l, 0, 0)),
            pl.BlockSpec((1, 1, MEM_HEAD_DIM), lambda l, b: (l, 0, 0)),
            pl.BlockSpec((rows_io, D_MODEL), step),
            pl.BlockSpec((rows_io, D_MODEL), step),
            pl.BlockSpec((rows_kv, KV_WIDTH), step),
        ],
        out_specs=[
            pl.BlockSpec((1, MEM_LEN, MEM_WIDTH), lambda l, b: (l, b, 0)),
            pl.BlockSpec((1, MEM_LEN, MEM_WIDTH), lambda l, b: (l, b, 0)),
            pl.BlockSpec((rows_io, D_MODEL), step),
            pl.BlockSpec((rows_io, D_MODEL), step),
            pl.BlockSpec((rows_kv, KV_WIDTH), step),
        ],
        out_shape=[out_sd, out_sd,
                   jax.ShapeDtypeStruct(win2d.shape, BF16),
                   jax.ShapeDtypeStruct(wout2d.shape, BF16),
                   jax.ShapeDtypeStruct(w_kv.shape, BF16)],
        compiler_params=pltpu.CompilerParams(
            dimension_semantics=("arbitrary", "arbitrary"),
            vmem_limit_bytes=VMEM_LIMIT_BYTES),
        name="mem_kv",
    )(mem2d, mem_norm.reshape(depth, 1, D_MODEL), w_mem_kv_bf,
      mem_k_norm.reshape(depth, 1, MEM_HEAD_DIM), win2d, wout2d, w_kv)
    return (mk, mv, win_bf.reshape(w_in.shape), wout_bf.reshape(w_out.shape), wkv_bf)


def _mem_attention_head(qh, qg, kh, vh):
    qn = (qh * _rstd(qh) * qg).astype(BF16)
    s = _dot_nt(qn, kh) * (MEM_HEAD_DIM ** -0.5)
    p = jnp.exp(s - jnp.max(s, axis=-1, keepdims=True))
    l = jnp.sum(p, axis=-1, keepdims=True)
    return _dot(p.astype(BF16), vh) / l


def _pool_bands():
    r = jnp.arange(MXU_DIM)[:, None]
    c = jnp.arange(MXU_DIM)[None, :]
    rh = jnp.arange(POOL_HALO)[:, None]
    ch = jnp.arange(POOL_HALO)[None, :]
    band = jnp.stack([((r - c >= 0) & (r - c < w)) for w in POOL_WINDOWS])
    bandh = jnp.stack([(rh + POOL_HALO - ch < w) for w in POOL_WINDOWS])
    return band.astype(BF16), bandh.astype(BF16)


def _ffn_cast_specs(layer, nsteps, step, tf):
    chunk = D_FF // nsteps
    per_block = tf // chunk
    in_specs = [pl.BlockSpec((None, D_MODEL, chunk), lambda *ix: (layer, 0, step(*ix))),
                pl.BlockSpec((None, chunk, D_MODEL), lambda *ix: (layer, step(*ix), 0))]
    out_specs = [pl.BlockSpec((None, D_MODEL, chunk),
                              lambda *ix: (step(*ix) // per_block, 0, step(*ix) % per_block)),
                 pl.BlockSpec((chunk, D_MODEL), lambda *ix: (step(*ix), 0))]
    out_shapes = [jax.ShapeDtypeStruct((D_FF // tf, D_MODEL, tf), BF16),
                  jax.ShapeDtypeStruct((D_FF, D_MODEL), BF16)]
    return in_specs, out_specs, out_shapes


def _layer_a_kernel(x_ref, xh_ref, g_ref, win_ref, band_ref, bandh_ref, poolw_ref, pscale_ref,
                    mk_ref, mv_ref, mqg_ref, wout_ref, w1_ref, w2_ref,
                    o_ref, w1bf_ref, w2bf_ref, cat_ref, *, tm):
    w1bf_ref[...] = w1_ref[...].astype(BF16)
    w2bf_ref[...] = w2_ref[...].astype(BF16)
    i = pl.program_id(1)
    x = x_ref[...]
    rstd = _rstd(x)
    xn = (x * g_ref[...]).astype(BF16)

    xh = xh_ref[...]
    uh = _dot((xh * g_ref[...]).astype(BF16), win_ref[:, :POOL_WIDTH]) * _rstd(xh)
    uh = jnp.where(i == 0, 0.0, uh).astype(BF16)

    row = lax.broadcasted_iota(jnp.int32, (MXU_DIM, 1), 0)
    half_width = 2 * POOL_GROUP

    def pool_epilogue(u_half, half):
        ub = u_half.astype(BF16)
        for gl in range(2):
            g = 2 * half + gl
            w = POOL_WINDOWS[g]
            ls = slice(gl * POOL_GROUP, (gl + 1) * POOL_GROUP)
            cs = slice(g * POOL_GROUP, (g + 1) * POOL_GROUP)
            for sb in range(tm // MXU_DIM):
                r0 = sb * MXU_DIM
                rs = slice(r0, r0 + MXU_DIM)
                main = _dot(band_ref[g], ub[rs, ls])
                halo = uh[:, cs] if sb == 0 else ub[r0 - POOL_HALO:r0, ls]
                top = _dot(bandh_ref[g], halo)
                wsum = jnp.concatenate([main[:POOL_HALO] + top, main[POOL_HALO:]], axis=0)
                t = i * tm + r0 + row
                count = jnp.minimum(t + 1, w).astype(F32)
                pooled = wsum / count - u_half[rs, ls]
                tok = _dot(pooled.astype(BF16), poolw_ref[g]) * pscale_ref[:, cs]
                cat_ref[rs, cs] = tok.astype(BF16)

    def out_partial(c0, c1):
        return _dot(cat_ref[:, c0:c1], wout_ref[c0:c1, :])

    uq = _dot(xn, win_ref[:, POOL_WIDTH:]) * rstd
    u0 = _dot(xn, win_ref[:, :half_width])
    mk, mv = mk_ref[0], mv_ref[0]
    for h in range(MEM_HEADS):
        hs = slice(h * MEM_HEAD_DIM, (h + 1) * MEM_HEAD_DIM)
        o = _mem_attention_head(uq[:, hs], mqg_ref[...], mk[:, hs], mv[:, hs])
        cat_ref[:, POOL_WIDTH + h * MEM_HEAD_DIM:POOL_WIDTH + (h + 1) * MEM_HEAD_DIM] = o.astype(BF16)
    u1 = _dot(xn, win_ref[:, half_width:POOL_WIDTH])
    pool_epilogue(u0 * rstd, 0)
    acc = x + out_partial(POOL_WIDTH, D_MODEL)
    pool_epilogue(u1 * rstd, 1)
    o_ref[...] = acc + out_partial(0, POOL_WIDTH)


def _layer_a(x2d, g, win_bf, wout_bf, layer, poolw_bf, pscale, mk, mv, mqg, w_ff1, w_ff2,
             batch, seq, tm, tf):
    nt = seq // tm
    wslab = lambda b, i: (layer, 0, 0)
    cast_in, cast_out, cast_shapes = _ffn_cast_specs(
        layer, batch * nt, lambda b, i: b * nt + i, tf)
    band, bandh = _pool_bands()
    ngrp = len(POOL_WINDOWS)
    const2 = lambda b, i: (0, 0)
    const3 = lambda b, i: (0, 0, 0)
    halo_blocks = tm // POOL_HALO
    return pl.pallas_call(
        functools.partial(_layer_a_kernel, tm=tm),
        grid=(batch, nt),
        in_specs=[
            pl.BlockSpec((tm, D_MODEL), lambda b, i: (b * nt + i, 0)),
            pl.BlockSpec((POOL_HALO, D_MODEL),
                         lambda b, i: (jnp.maximum((b * nt + i) * halo_blocks - 1, 0), 0)),
            _resident((1, D_MODEL), const2),
            _resident((None, D_MODEL, D_MODEL), wslab),
            _resident((ngrp, MXU_DIM, MXU_DIM), const3),
            _resident((ngrp, POOL_HALO, POOL_HALO), const3),
            _resident((ngrp, POOL_GROUP, POOL_GROUP), const3),
            _resident((1, POOL_WIDTH), const2),
            pl.BlockSpec((1, MEM_LEN, MEM_WIDTH), lambda b, i: (layer, b, 0)),
            pl.BlockSpec((1, MEM_LEN, MEM_WIDTH), lambda b, i: (layer, b, 0)),
            _resident((1, MEM_HEAD_DIM), const2),
            _resident((None, D_MODEL, D_MODEL), wslab),
        ] + cast_in,
        out_specs=[pl.BlockSpec((tm, D_MODEL), lambda b, i: (b * nt + i, 0))] + cast_out,
        out_shape=[jax.ShapeDtypeStruct(x2d.shape, F32)] + cast_shapes,
        scratch_shapes=[pltpu.VMEM((tm, D_MODEL), BF16)],
        compiler_params=pltpu.CompilerParams(
            dimension_semantics=("arbitrary", "arbitrary"),
            vmem_limit_bytes=VMEM_LIMIT_BYTES),
        name="layer_a_mixer",
    )(x2d, x2d, g, win_bf, band, bandh, poolw_bf, pscale, mk, mv, mqg, wout_bf, w_ff1, w_ff2)


def _ffn_kernel(x_ref, g_ref, w1_ref, w2_ref, o_ref, xg_ref, r2_ref, *, nk):
    k = pl.program_id(1)

    def partial(xg):
        z = jnp.maximum(_dot(xg, w1_ref[...]), 0.0)
        return _dot((z * z).astype(BF16), w2_ref[...])

    @pl.when(k == 0)
    def _():
        x = x_ref[...]
        xg = (x * g_ref[...]).astype(BF16)
        xg_ref[...] = xg
        r2_ref[...] = 1.0 / (jnp.mean(x * x, axis=-1, keepdims=True) + EPS)
        o_ref[...] = partial(xg)

    @pl.when(jnp.logical_and(k > 0, k < nk - 1))
    def _():
        o_ref[...] += partial(xg_ref[...])

    @pl.when(k == nk - 1)
    def _():
        o_ref[...] = x_ref[...] + r2_ref[...] * (o_ref[...] + partial(xg_ref[...]))


def _ffn(x2d, g, w1_bf, w2_bf, tm, tf):
    m = x2d.shape[0]
    assert w1_bf.shape == (D_FF // tf, D_MODEL, tf)
    nk = D_FF // tf
    assert nk >= 2
    return pl.pallas_call(
        functools.partial(_ffn_kernel, nk=nk),
        grid=(m // tm, nk),
        in_specs=[
            pl.BlockSpec((tm, D_MODEL), lambda i, k: (i, 0)),
            _resident((1, D_MODEL), lambda i, k: (0, 0)),
            pl.BlockSpec((None, D_MODEL, tf), lambda i, k: (k, 0, 0)),
            pl.BlockSpec((tf, D_MODEL), lambda i, k: (k, 0)),
        ],
        out_specs=pl.BlockSpec((tm, D_MODEL), lambda i, k: (i, 0)),
        out_shape=jax.ShapeDtypeStruct(x2d.shape, F32),
        scratch_shapes=[pltpu.VMEM((tm, D_MODEL), BF16), pltpu.VMEM((tm, 1), F32)],
        compiler_params=pltpu.CompilerParams(
            dimension_semantics=("arbitrary", "arbitrary"),
            vmem_limit_bytes=VMEM_LIMIT_BYTES),
        name="ffn",
    )(x2d, g, w1_bf, w2_bf)


def _rope_coeffs(seq):
    half = ROPE_DIM // 2
    pos = jnp.arange(seq, dtype=F32)
    inv = ROPE_THETA ** (-(jnp.arange(half, dtype=F32) * 2.0) / ROPE_DIM)
    ang = pos[:, None] * inv[None, :]
    cos, sin = jnp.cos(ang), jnp.sin(ang)
    ones = jnp.ones((seq, DIFF_HEAD_DIM - ROPE_DIM), F32)
    zeros = jnp.zeros((seq, DIFF_HEAD_DIM - ROPE_DIM), F32)
    zh = jnp.zeros((seq, half), F32)
    c0 = jnp.concatenate([cos, cos, ones], axis=-1)
    c_up = jnp.concatenate([-sin, zh, zeros], axis=-1)
    c_dn = jnp.concatenate([zh, sin, zeros], axis=-1)
    rep = LANES // DIFF_HEAD_DIM
    return tuple(jnp.tile(c, (1, rep)) for c in (c0, c_up, c_dn))


def _head_norm_rope(t, gain, bd, c0, c_up, c_dn):
    ss = _dot((t * t).astype(BF16), bd)
    tn = t * lax.rsqrt(ss * (1.0 / DIFF_HEAD_DIM) + EPS) * gain
    outs = []
    half = ROPE_DIM // 2
    for p in range(MXU_DIM // LANES):
        tp = tn[:, p * LANES:(p + 1) * LANES]
        up = pltpu.roll(tp, LANES - half, axis=1)
        dn = pltpu.roll(tp, half, axis=1)
        outs.append(tp * c0 + up * c_up + dn * c_dn)
    return jnp.concatenate(outs, axis=-1)


def _layer_b_proj_kernel(x_ref, gmix_ref, gkv_ref, win_ref, wkv_ref, bd_ref, qg_ref, kg_ref,
                         c0_ref, cup_ref, cdn_ref, mk_ref, mv_ref, mqg_ref, w1_ref, w2_ref,
                         q1t_ref, q2t_ref, k1_ref, k2_ref, vt_ref, mo_ref, w1bf_ref, w2bf_ref):
    w1bf_ref[...] = w1_ref[...].astype(BF16)
    w2bf_ref[...] = w2_ref[...].astype(BF16)
    x = x_ref[...]
    tm = x.shape[0]
    rstd = _rstd(x)
    xn_mix = (x * gmix_ref[...]).astype(BF16)
    xn_kv = (x * gkv_ref[...]).astype(BF16)
    bd = bd_ref[...]
    c0, cup, cdn = c0_ref[...], cup_ref[...], cdn_ref[...]
    qg = qg_ref[...] * (DIFF_HEAD_DIM ** -0.5 * math.log2(math.e))
    kg = kg_ref[...]
    pairs_per_chunk = MXU_DIM // LANES
    pad_rows = VT_ROWS - DIFF_V_DIM
    ones_row = (lax.broadcasted_iota(jnp.int32, (pad_rows, tm), 0) == 0).astype(BF16)
    mk, mv = mk_ref[0], mv_ref[0]

    def q_epilogue(qt_ref, c):
        def run(t):
            qt = _head_norm_rope(t, qg, bd, c0, cup, cdn).T.astype(BF16)
            for p in range(pairs_per_chunk):
                qt_ref[0, pairs_per_chunk * c + p, 0] = qt[p * LANES:(p + 1) * LANES]
        return run

    def k_epilogue(k_ref, c):
        def run(t):
            k_ref[:, c * MXU_DIM:(c + 1) * MXU_DIM] = _head_norm_rope(
                t, kg, bd, c0, cup, cdn).astype(BF16)
        return run

    def v_epilogue(c):
        def run(t):
            for p in range(pairs_per_chunk):
                h = pairs_per_chunk * c + p
                vt_ref[0, h, 0, :DIFF_V_DIM, :] = t[:, p * DIFF_V_DIM:(p + 1) * DIFF_V_DIM].T.astype(BF16)
                vt_ref[0, h, 0, DIFF_V_DIM:, :] = ones_row
        return run

    def mem_epilogue(c):
        def run(t):
            for p in range(pairs_per_chunk):
                h = pairs_per_chunk * c + p
                hs = slice(h * MEM_HEAD_DIM, (h + 1) * MEM_HEAD_DIM)
                o = _mem_attention_head(t[:, p * MEM_HEAD_DIM:(p + 1) * MEM_HEAD_DIM],
                                        mqg_ref[...], mk[:, hs], mv[:, hs])
                mo_ref[:, hs] = o.astype(BF16)
        return run

    chunks = []
    for c in range(DIFF_QK_WIDTH // MXU_DIM):
        chunks += [
            (xn_mix, win_ref, c * MXU_DIM, q_epilogue(q1t_ref, c)),
            (xn_kv, wkv_ref, c * MXU_DIM, k_epilogue(k1_ref, c)),
            (xn_kv, wkv_ref, 2 * DIFF_QK_WIDTH + 2 * c * MXU_DIM, v_epilogue(2 * c)),
            (xn_mix, win_ref, DIFF_QK_WIDTH + c * MXU_DIM, q_epilogue(q2t_ref, c)),
            (xn_kv, wkv_ref, DIFF_QK_WIDTH + c * MXU_DIM, k_epilogue(k2_ref, c)),
            (xn_kv, wkv_ref, 2 * DIFF_QK_WIDTH + (2 * c + 1) * MXU_DIM, v_epilogue(2 * c + 1)),
        ]
    for c in range(MEM_WIDTH // MXU_DIM):
        chunks.append((xn_mix, win_ref, 2 * DIFF_QK_WIDTH + c * MXU_DIM, mem_epilogue(c)))

    def project(chunk):
        xn, w_ref, col, _ = chunk
        return _dot(xn, w_ref[:, col:col + MXU_DIM])

    pending = project(chunks[0])
    for n, chunk in enumerate(chunks):
        ready = pending
        if n + 1 < len(chunks):
            pending = project(chunks[n + 1])
        chunk[3](ready * rstd)


def _layer_b_proj(x2d, gmix, gkv, win_bf, layer, wkv_bf, q_norm, k_norm, mk, mv, mqg,
                  w_ff1, w_ff2, batch, seq, tm, tq, tf):
    nt = seq // tm
    cast_in, cast_out, cast_shapes = _ffn_cast_specs(
        layer, batch * nt, lambda b, i: b * nt + i, tf)
    nq = seq // tq
    sub = tq // tm
    npairs = DIFF_QK_WIDTH // LANES
    m = x2d.shape[0]
    c0, cup, cdn = _rope_coeffs(seq)
    lane_head = jnp.arange(MXU_DIM) // DIFF_HEAD_DIM
    bd = (lane_head[:, None] == lane_head[None, :]).astype(BF16)
    qg = jnp.tile(q_norm.reshape(1, DIFF_HEAD_DIM), (1, MXU_DIM // DIFF_HEAD_DIM))
    kg = jnp.tile(k_norm.reshape(1, DIFF_HEAD_DIM), (1, MXU_DIM // DIFF_HEAD_DIM))
    const2 = lambda b, i: (0, 0)
    rows = lambda b, i: (b * nt + i, 0)
    qk_sd = jax.ShapeDtypeStruct((m, DIFF_QK_WIDTH), BF16)
    qt_sd = jax.ShapeDtypeStruct((batch, npairs, nq, LANES, tq), BF16)
    qt_spec = pl.BlockSpec((1, npairs, 1, LANES, tm), lambda b, i: (b, 0, i // sub, 0, i % sub))
    return pl.pallas_call(
        _layer_b_proj_kernel,
        grid=(batch, nt),
        in_specs=[
            pl.BlockSpec((tm, D_MODEL), rows),
            _resident((1, D_MODEL), const2),
            _resident((1, D_MODEL), const2),
            _resident((None, D_MODEL, D_MODEL), lambda b, i: (layer, 0, 0)),
            _resident((D_MODEL, KV_WIDTH), const2),
            _resident((MXU_DIM, MXU_DIM), const2),
            _resident((1, MXU_DIM), const2),
            _resident((1, MXU_DIM), const2),
            pl.BlockSpec((tm, LANES), lambda b, i: (i, 0)),
            pl.BlockSpec((tm, LANES), lambda b, i: (i, 0)),
            pl.BlockSpec((tm, LANES), lambda b, i: (i, 0)),
            pl.BlockSpec((1, MEM_LEN, MEM_WIDTH), lambda b, i: (layer, b, 0)),
            pl.BlockSpec((1, MEM_LEN, MEM_WIDTH), lambda b, i: (layer, b, 0)),
            _resident((1, MEM_HEAD_DIM), const2),
        ] + cast_in,
        out_specs=[
            qt_spec,
            qt_spec,
            pl.BlockSpec((tm, DIFF_QK_WIDTH), rows),
            pl.BlockSpec((tm, DIFF_QK_WIDTH), rows),
            pl.BlockSpec((1, DIFF_HEADS, 1, VT_ROWS, tm),
                         lambda b, i: (b, 0, i // sub, 0, i % sub)),
            pl.BlockSpec((tm, MEM_WIDTH), rows),
        ] + cast_out,
        out_shape=[qt_sd, qt_sd, qk_sd, qk_sd,
                   jax.ShapeDtypeStruct((batch, DIFF_HEADS, nq, VT_ROWS, tq), BF16),
                   jax.ShapeDtypeStruct((m, MEM_WIDTH), BF16)] + cast_shapes,
        compiler_params=pltpu.CompilerParams(
            dimension_semantics=("arbitrary", "arbitrary"),
            vmem_limit_bytes=VMEM_LIMIT_BYTES),
        name="layer_b_proj",
    )(x2d, gmix, gkv, win_bf, wkv_bf, bd, qg, kg, c0, cup, cdn, mk, mv, mqg, w_ff1, w_ff2)


def _diff_attn_kernel(q1t_ref, q2t_ref, k1_ref, k2_ref, vt_ref, lam_ref, sg_ref, o_ref,
                      s_ref, m_ref, acc_ref, b_ref, *, tq, nq, pairs, lam_init):
    feat = lax.broadcasted_iota(jnp.int32, (LANES, 1), 0)
    heads_per_pair = LANES // DIFF_HEAD_DIM
    combos = [(pr, hh, mp) for pr in range(pairs) for hh in range(heads_per_pair)
              for mp in range(2)]
    in_head = [(feat >= hh * DIFF_HEAD_DIM) & (feat < (hh + 1) * DIFF_HEAD_DIM)
               for hh in range(heads_per_pair)]
    key_pos = lax.broadcasted_iota(jnp.int32, (tq, tq), 0)
    q_pos = lax.broadcasted_iota(jnp.int32, (tq, tq), 1)

    def scores(i, blk):
        r0 = pl.multiple_of(blk * tq, tq)
        out = []
        for pr, hh, mp in combos:
            qt = (q1t_ref, q2t_ref)[mp][0, pr, i]
            qm = jnp.where(in_head[hh], qt, jnp.zeros_like(qt))
            kblk = (k1_ref, k2_ref)[mp][pl.ds(r0, tq), pr * LANES:(pr + 1) * LANES]
            out.append(_dot(kblk, qm))
        return out

    def store_scores(sts, diagonal):
        for c, st in enumerate(sts):
            s_ref[c] = jnp.where(key_pos <= q_pos, st, NEG_INF) if diagonal else st

    def score_bounds():
        lane = lax.broadcasted_iota(jnp.int32, (1, LANES), 1)
        sub = lax.broadcasted_iota(jnp.int32, (2 * SUBLANES, LANES), 0)
        largest = jnp.zeros((1, tq), F32)
        for pr in range(pairs):
            for mp in range(2):
                kall = (k1_ref, k2_ref)[mp][:, pr * LANES:(pr + 1) * LANES]
                kmax = jnp.max(jnp.abs(kall.astype(F32)), axis=0, keepdims=True)
                lhs = jnp.zeros((2 * SUBLANES, LANES), F32)
                for hh in range(heads_per_pair):
                    head_lanes = (lane >= hh * DIFF_HEAD_DIM) & (lane < (hh + 1) * DIFF_HEAD_DIM)
                    lhs = jnp.where((sub == hh) & head_lanes, kmax, lhs)
                lhs = lhs.astype(BF16)
                for i in range(nq):
                    qabs = jnp.abs((q1t_ref, q2t_ref)[mp][0, pr, i])
                    bound = _dot(lhs, qabs)
                    for hh in range(heads_per_pair):
                        c = 2 * (heads_per_pair * pr + hh) + mp
                        b_ref[i, c] = bound[hh:hh + 1]
                        largest = jnp.maximum(largest, bound[hh:hh + 1])
        return jnp.max(largest)

    def run(fast):
        def consume(i, blk, first):
            for c, (pr, hh, mp) in enumerate(combos):
                st = s_ref[c]
                vtblk = vt_ref[0, heads_per_pair * pr + hh, blk]
                if fast:
                    p = jnp.exp2(st - b_ref[i, c]).astype(BF16)
                    acc_ref[c] += _dot(vtblk, p)
                else:
                    m_old = jnp.where(first, NEG_INF, m_ref[c])
                    m_new = jnp.maximum(m_old, jnp.max(st, axis=0, keepdims=True))
                    alpha = jnp.exp2(m_old - m_new)
                    p = jnp.exp2(st - m_new).astype(BF16)
                    acc_ref[c] = alpha * acc_ref[c] + _dot(vtblk, p)
                    m_ref[c] = m_new

        def finalize(i):
            lq = lam_ref[...]
            lam = (jnp.exp(jnp.sum(lq[0:1] * lq[1:2], axis=-1, keepdims=True))
                   - jnp.exp(jnp.sum(lq[2:3] * lq[3:4], axis=-1, keepdims=True)) + lam_init)
            r0 = pl.multiple_of(i * tq, tq)
            for hh in range(heads_per_pair * pairs):
                a1, a2 = acc_ref[2 * hh], acc_ref[2 * hh + 1]
                ot = (a1[:DIFF_V_DIM] * (1.0 / a1[DIFF_V_DIM:DIFF_V_DIM + 1])
                      - lam * (a2[:DIFF_V_DIM] * (1.0 / a2[DIFF_V_DIM:DIFF_V_DIM + 1])))
                ot = ot * lax.rsqrt(jnp.mean(ot * ot, axis=0, keepdims=True) + EPS)
                o = ot.T * sg_ref[...] * (1.0 - lam_init)
                o_ref[pl.ds(r0, tq), hh * DIFF_V_DIM:(hh + 1) * DIFF_V_DIM] = o.astype(BF16)
            if fast:
                acc_ref[...] = jnp.zeros(acc_ref.shape, F32)

        store_scores(scores(0, 0), True)

        def body(t, carry):
            i, pos = carry
            last = pos == i
            ni = jnp.where(last, i + 1, i)
            npos = jnp.where(last, 0, pos + 1)
            blk = jnp.where(pos == 0, i, pos - 1)
            nblk = jnp.where(npos == 0, ni, npos - 1)

            @pl.when(last)
            def _():
                nxt = scores(ni, nblk)
                consume(i, blk, pos == 0)
                store_scores(nxt, True)
                finalize(i)

            @pl.when(jnp.logical_not(last))
            def _():
                nxt = scores(ni, nblk)
                consume(i, blk, pos == 0)
                store_scores(nxt, False)

            return ni, npos

        ntasks = nq * (nq + 1) // 2
        lax.fori_loop(0, ntasks - 1, body, (jnp.int32(0), jnp.int32(0)))
        consume(nq - 1, max(nq - 2, 0), nq == 1)
        finalize(nq - 1)

    acc_ref[...] = jnp.zeros(acc_ref.shape, F32)
    m_ref[...] = jnp.full(m_ref.shape, NEG_INF, F32)
    use_bounds = score_bounds() <= MAX_SAFE_SCORE_BOUND

    @pl.when(use_bounds)
    def _():
        run(True)

    @pl.when(jnp.logical_not(use_bounds))
    def _():
        run(False)


def _diff_attention(q1t, q2t, k1, k2, vt, diff_lambda, subln, batch, seq, tq, pairs, lam_init):
    nq = seq // tq
    m = k1.shape[0]
    npairs = DIFF_QK_WIDTH // LANES
    heads_per_step = pairs * (LANES // DIFF_HEAD_DIM)
    vw = heads_per_step * DIFF_V_DIM
    ncombo = 2 * heads_per_step
    qspec = pl.BlockSpec((1, pairs, nq, LANES, tq), lambda b, hp: (b, hp, 0, 0, 0))
    kspec = pl.BlockSpec((seq, pairs * LANES), lambda b, hp: (b, hp))
    return pl.pallas_call(
        functools.partial(_diff_attn_kernel, tq=tq, nq=nq, pairs=pairs, lam_init=lam_init),
        grid=(batch, npairs // pairs),
        in_specs=[
            qspec, qspec, kspec, kspec,
            pl.BlockSpec((1, heads_per_step, nq, VT_ROWS, tq), lambda b, hp: (b, hp, 0, 0, 0)),
            _resident((4, DIFF_HEAD_DIM), lambda b, hp: (0, 0)),
            _resident((1, DIFF_V_DIM), lambda b, hp: (0, 0)),
        ],
        out_specs=pl.BlockSpec((seq, vw), lambda b, hp: (b, hp)),
        out_shape=jax.ShapeDtypeStruct((m, DIFF_V_WIDTH), BF16),
        scratch_shapes=[
            pltpu.VMEM((ncombo, tq, tq), F32),
            pltpu.VMEM((ncombo, 1, tq), F32),
            pltpu.VMEM((ncombo, VT_ROWS, tq), F32),
            pltpu.VMEM((nq, ncombo, 1, tq), F32),
        ],
        compiler_params=pltpu.CompilerParams(
            dimension_semantics=("arbitrary", "arbitrary"),
            vmem_limit_bytes=VMEM_LIMIT_BYTES),
        name="diff_attention",
    )(q1t, q2t, k1, k2, vt, diff_lambda, subln)


def _out_proj_kernel(tok_ref, mo_ref, x_ref, w_ref, o_ref):
    o_ref[...] = (x_ref[...] + _dot(tok_ref[...], w_ref[:DIFF_V_WIDTH, :])
                  + _dot(mo_ref[...], w_ref[DIFF_V_WIDTH:, :]))


def _out_proj(tok, mo, x2d, wout_bf, layer, tm):
    m = x2d.shape[0]
    rows = lambda i: (i, 0)
    return pl.pallas_call(
        _out_proj_kernel,
        grid=(m // tm,),
        in_specs=[
            pl.BlockSpec((tm, DIFF_V_WIDTH), rows),
            pl.BlockSpec((tm, MEM_WIDTH), rows),
            pl.BlockSpec((tm, D_MODEL), rows),
            _resident((None, D_MODEL, D_MODEL), lambda i: (layer, 0, 0)),
        ],
        out_specs=pl.BlockSpec((tm, D_MODEL), rows),
        out_shape=jax.ShapeDtypeStruct(x2d.shape, F32),
        compiler_params=pltpu.CompilerParams(
            dimension_semantics=("arbitrary",),
            vmem_limit_bytes=VMEM_LIMIT_BYTES),
        name="out_proj",
    )(tok, mo, x2d, wout_bf)


def kernel(x, mem, mix_norm, w_in, w_out, mem_norm, w_mem_kv, mem_q_norm, mem_k_norm,
           ffn_norm, w_ff1, w_ff2, pool_w, pool_scale, kv_norm, w_kv, k_norm, q_norm,
           diff_lambda, subln_norm):
    batch, seq, _ = x.shape
    m = batch * seq
    x2d = x.reshape(m, D_MODEL)
    mem2d = mem.reshape(batch * MEM_LEN, D_MODEL)

    mk, mv, w_in_bf, w_out_bf, w_kv_bf = _memkv(
        mem2d, mem_norm, w_mem_kv.astype(BF16), mem_k_norm, w_in, w_out, w_kv, batch)

    tf = 512

    x2d, w1_bf, w2_bf = _layer_a(x2d, mix_norm[0:1], w_in_bf, w_out_bf, 0,
                                 pool_w[0].astype(BF16), pool_scale[0:1], mk, mv,
                                 mem_q_norm[0:1], w_ff1, w_ff2, batch, seq, tm=512, tf=tf)
    x2d = _ffn(x2d, ffn_norm[0:1], w1_bf, w2_bf, tm=1024, tf=tf)

    lam_init = 0.8 - 0.6 * math.exp(-0.3 * 1)
    tq = 256
    q1t, q2t, k1, k2, vt, mo, w1_bf, w2_bf = _layer_b_proj(
        x2d, mix_norm[1:2], kv_norm.reshape(1, D_MODEL), w_in_bf, 1, w_kv_bf,
        q_norm[0], k_norm, mk, mv, mem_q_norm[1:2], w_ff1, w_ff2, batch, seq,
        tm=256, tq=tq, tf=tf)
    tok = _diff_attention(q1t, q2t, k1, k2, vt, diff_lambda[0], subln_norm[0:1],
                          batch, seq, tq=tq, pairs=2, lam_init=lam_init)
    x2d = _out_proj(tok, mo, x2d, w_out_bf, 1, tm=1024)
    x2d = _ffn(x2d, ffn_norm[1:2], w1_bf, w2_bf, tm=1024, tf=tf)
    return x2d.reshape(batch, seq, D_MODEL)
```

```python
import functools
import math

import jax
import jax.numpy as jnp
from jax import lax
from jax.experimental import pallas as pl
from jax.experimental.pallas import tpu as pltpu

D_MODEL = 2048
MEM_LEN = 256
MEM_HEADS = 4
MEM_HEAD_DIM = 128
MEM_WIDTH = MEM_HEADS * MEM_HEAD_DIM
POOL_WIDTH = D_MODEL - MEM_WIDTH
POOL_WINDOWS = (2, 4, 8, 16)
POOL_GROUP = POOL_WIDTH // len(POOL_WINDOWS)
POOL_HALO = 16
DIFF_HEAD_DIM = 64
DIFF_V_DIM = 128
DIFF_HEADS = POOL_WIDTH // DIFF_V_DIM
DIFF_QK_WIDTH = DIFF_HEADS * DIFF_HEAD_DIM
DIFF_V_WIDTH = DIFF_HEADS * DIFF_V_DIM
VT_ROWS = DIFF_V_DIM + 16
KV_WIDTH = 2 * DIFF_QK_WIDTH + DIFF_V_WIDTH
ROPE_THETA = 500000.0
ROPE_DIM = DIFF_HEAD_DIM // 4
D_FF = 4 * D_MODEL
EPS = 1e-6
NEG_INF = -1e30

MAX_SAFE_SCORE_BOUND = 40.0

LANES = 128
SUBLANES = 8
MXU_DIM = 256
VMEM_LIMIT_BYTES = 56 * 1024 * 1024

BF16 = jnp.bfloat16
F32 = jnp.float32


def _resident(shape, index_map):
    return pl.BlockSpec(shape, index_map, pipeline_mode=pl.Buffered(1))


def _rstd(x):
    return lax.rsqrt(jnp.mean(x * x, axis=-1, keepdims=True) + EPS)


def _dot(a, b):
    return jnp.dot(a, b, preferred_element_type=F32)


def _dot_nt(a, b):
    return lax.dot_general(a, b, (((1,), (1,)), ((), ())), preferred_element_type=F32)


def _memkv_kernel(mem_ref, g_ref, w_ref, kg_ref, win_ref, wout_ref,
                  k_ref, v_ref, winbf_ref, woutbf_ref):
    winbf_ref[...] = win_ref[...].astype(BF16)
    woutbf_ref[...] = wout_ref[...].astype(BF16)
    x = mem_ref[...]
    xn = (x * _rstd(x) * g_ref[0]).astype(BF16)
    kv = _dot(xn, w_ref[0])
    for h in range(MEM_HEADS):
        hs = slice(h * MEM_HEAD_DIM, (h + 1) * MEM_HEAD_DIM)
        kh = kv[:, hs]
        k_ref[0, :, hs] = (kh * _rstd(kh) * kg_ref[0]).astype(BF16)
    v_ref[0] = kv[:, MEM_WIDTH:].astype(BF16)


def _memkv(mem2d, mem_norm, w_mem_kv_bf, mem_k_norm, w_in, w_out, batch):
    depth = mem_norm.shape[0]
    nsteps = depth * batch
    rows = D_MODEL // nsteps
    chunk_in = pl.BlockSpec((None, rows, D_MODEL), lambda l, b: (0, l * batch + b, 0))
    chunk_out = pl.BlockSpec((rows, D_MODEL), lambda l, b: (l * batch + b, 0))
    out_sd = jax.ShapeDtypeStruct((depth, batch * MEM_LEN, MEM_WIDTH), BF16)
    w_sd = jax.ShapeDtypeStruct((D_MODEL, D_MODEL), BF16)
    return pl.pallas_call(
        _memkv_kernel,
        grid=(depth, batch),
        in_specs=[
            pl.BlockSpec((MEM_LEN, D_MODEL), lambda l, b: (b, 0)),
            pl.BlockSpec((1, 1, D_MODEL), lambda l, b: (l, 0, 0)),
            pl.BlockSpec((1, D_MODEL, 2 * MEM_WIDTH), lambda l, b: (l, 0, 0)),
            pl.BlockSpec((1, 1, MEM_HEAD_DIM), lambda l, b: (l, 0, 0)),
            chunk_in,
            chunk_in,
        ],
        out_specs=[
            pl.BlockSpec((1, MEM_LEN, MEM_WIDTH), lambda l, b: (l, b, 0)),
            pl.BlockSpec((1, MEM_LEN, MEM_WIDTH), lambda l, b: (l, b, 0)),
            chunk_out,
            chunk_out,
        ],
        out_shape=[out_sd, out_sd, w_sd, w_sd],
        compiler_params=pltpu.CompilerParams(
            dimension_semantics=("arbitrary", "arbitrary"),
            vmem_limit_bytes=VMEM_LIMIT_BYTES),
        name="mem_kv",
    )(mem2d, mem_norm.reshape(depth, 1, D_MODEL), w_mem_kv_bf,
      mem_k_norm.reshape(depth, 1, MEM_HEAD_DIM), w_in, w_out)


def _mem_attention_head(qh, qg, kh, vh):
    qn = (qh * _rstd(qh) * qg).astype(BF16)
    s = _dot_nt(qn, kh) * (MEM_HEAD_DIM ** -0.5)
    p = jnp.exp(s - jnp.max(s, axis=-1, keepdims=True))
    l = jnp.sum(p, axis=-1, keepdims=True)
    return _dot(p.astype(BF16), vh) / l


def _pool_bands():
    r = jnp.arange(MXU_DIM)[:, None]
    c = jnp.arange(MXU_DIM)[None, :]
    rh = jnp.arange(POOL_HALO)[:, None]
    ch = jnp.arange(POOL_HALO)[None, :]
    band = jnp.stack([((r - c >= 0) & (r - c < w)) for w in POOL_WINDOWS])
    bandh = jnp.stack([(rh + POOL_HALO - ch < w) for w in POOL_WINDOWS])
    return band.astype(BF16), bandh.astype(BF16)


def _ffn_cast_specs(layer, nsteps, step, tf):
    chunk = D_FF // nsteps
    per_block = tf // chunk
    in_specs = [pl.BlockSpec((None, D_MODEL, chunk), lambda *ix: (layer, 0, step(*ix))),
                pl.BlockSpec((None, chunk, D_MODEL), lambda *ix: (layer, step(*ix), 0))]
    out_specs = [pl.BlockSpec((None, D_MODEL, chunk),
                              lambda *ix: (step(*ix) // per_block, 0, step(*ix) % per_block)),
                 pl.BlockSpec((chunk, D_MODEL), lambda *ix: (step(*ix), 0))]
    out_shapes = [jax.ShapeDtypeStruct((D_FF // tf, D_MODEL, tf), BF16),
                  jax.ShapeDtypeStruct((D_FF, D_MODEL), BF16)]
    return in_specs, out_specs, out_shapes


def _layer_a_kernel(x_ref, xh_ref, g_ref, win_ref, band_ref, bandh_ref, poolw_ref, pscale_ref,
                    mk_ref, mv_ref, mqg_ref, wout_ref, w1_ref, w2_ref,
                    o_ref, w1bf_ref, w2bf_ref, cat_ref, *, tm):
    w1bf_ref[...] = w1_ref[...].astype(BF16)
    w2bf_ref[...] = w2_ref[...].astype(BF16)
    i = pl.program_id(1)
    x = x_ref[...]
    rstd = _rstd(x)
    xn = (x * g_ref[...]).astype(BF16)

    xh = xh_ref[...]
    uh = _dot((xh * g_ref[...]).astype(BF16), win_ref[:, :POOL_WIDTH]) * _rstd(xh)
    uh = jnp.where(i == 0, 0.0, uh).astype(BF16)

    row = lax.broadcasted_iota(jnp.int32, (MXU_DIM, 1), 0)
    half_width = 2 * POOL_GROUP

    def pool_epilogue(u_half, half):
        ub = u_half.astype(BF16)
        for gl in range(2):
            g = 2 * half + gl
            w = POOL_WINDOWS[g]
            ls = slice(gl * POOL_GROUP, (gl + 1) * POOL_GROUP)
            cs = slice(g * POOL_GROUP, (g + 1) * POOL_GROUP)
            for sb in range(tm // MXU_DIM):
                r0 = sb * MXU_DIM
                rs = slice(r0, r0 + MXU_DIM)
                main = _dot(band_ref[g], ub[rs, ls])
                halo = uh[:, cs] if sb == 0 else ub[r0 - POOL_HALO:r0, ls]
                top = _dot(bandh_ref[g], halo)
                wsum = jnp.concatenate([main[:POOL_HALO] + top, main[POOL_HALO:]], axis=0)
                t = i * tm + r0 + row
                count = jnp.minimum(t + 1, w).astype(F32)
                pooled = wsum / count - u_half[rs, ls]
                tok = _dot(pooled.astype(BF16), poolw_ref[g]) * pscale_ref[:, cs]
                cat_ref[rs, cs] = tok.astype(BF16)

    def out_partial(c0, c1):
        return _dot(cat_ref[:, c0:c1], wout_ref[c0:c1, :])

    uq = _dot(xn, win_ref[:, POOL_WIDTH:]) * rstd
    u0 = _dot(xn, win_ref[:, :half_width])
    mk, mv = mk_ref[0], mv_ref[0]
    for h in range(MEM_HEADS):
        hs = slice(h * MEM_HEAD_DIM, (h + 1) * MEM_HEAD_DIM)
        o = _mem_attention_head(uq[:, hs], mqg_ref[...], mk[:, hs], mv[:, hs])
        cat_ref[:, POOL_WIDTH + h * MEM_HEAD_DIM:POOL_WIDTH + (h + 1) * MEM_HEAD_DIM] = o.astype(BF16)
    u1 = _dot(xn, win_ref[:, half_width:POOL_WIDTH])
    pool_epilogue(u0 * rstd, 0)
    acc = x + out_partial(POOL_WIDTH, D_MODEL)
    pool_epilogue(u1 * rstd, 1)
    o_ref[...] = acc + out_partial(0, POOL_WIDTH)


def _layer_a(x2d, g, win_bf, wout_bf, layer, poolw_bf, pscale, mk, mv, mqg, w_ff1, w_ff2,
             batch, seq, tm, tf):
    nt = seq // tm
    wslab = lambda b, i: (layer, 0, 0)
    cast_in, cast_out, cast_shapes = _ffn_cast_specs(
        layer, batch * nt, lambda b, i: b * nt + i, tf)
    band, bandh = _pool_bands()
    ngrp = len(POOL_WINDOWS)
    const2 = lambda b, i: (0, 0)
    const3 = lambda b, i: (0, 0, 0)
    halo_blocks = tm // POOL_HALO
    return pl.pallas_call(
        functools.partial(_layer_a_kernel, tm=tm),
        grid=(batch, nt),
        in_specs=[
            pl.BlockSpec((tm, D_MODEL), lambda b, i: (b * nt + i, 0)),
            pl.BlockSpec((POOL_HALO, D_MODEL),
                         lambda b, i: (jnp.maximum((b * nt + i) * halo_blocks - 1, 0), 0)),
            _resident((1, D_MODEL), const2),
            _resident((None, D_MODEL, D_MODEL), wslab),
            _resident((ngrp, MXU_DIM, MXU_DIM), const3),
            _resident((ngrp, POOL_HALO, POOL_HALO), const3),
            _resident((ngrp, POOL_GROUP, POOL_GROUP), const3),
            _resident((1, POOL_WIDTH), const2),
            pl.BlockSpec((1, MEM_LEN, MEM_WIDTH), lambda b, i: (layer, b, 0)),
            pl.BlockSpec((1, MEM_LEN, MEM_WIDTH), lambda b, i: (layer, b, 0)),
            _resident((1, MEM_HEAD_DIM), const2),
            _resident((None, D_MODEL, D_MODEL), wslab),
        ] + cast_in,
        out_specs=[pl.BlockSpec((tm, D_MODEL), lambda b, i: (b * nt + i, 0))] + cast_out,
        out_shape=[jax.ShapeDtypeStruct(x2d.shape, F32)] + cast_shapes,
        scratch_shapes=[pltpu.VMEM((tm, D_MODEL), BF16)],
        compiler_params=pltpu.CompilerParams(
            dimension_semantics=("arbitrary", "arbitrary"),
            vmem_limit_bytes=VMEM_LIMIT_BYTES),
        name="layer_a_mixer",
    )(x2d, x2d, g, win_bf, band, bandh, poolw_bf, pscale, mk, mv, mqg, wout_bf, w_ff1, w_ff2)


def _ffn_kernel(x_ref, g_ref, w1_ref, w2_ref, *rest, nk, ncast):
    cast_in = rest[:ncast]
    o_ref = rest[ncast]
    cast_out = rest[ncast + 1:2 * ncast + 1]
    xg_ref, r2_ref = rest[2 * ncast + 1:]
    k = pl.program_id(1)

    def partial(xg):
        z = jnp.maximum(_dot(xg, w1_ref[...]), 0.0)
        return _dot((z * z).astype(BF16), w2_ref[...])

    @pl.when(k == 0)
    def _():
        for src_ref, dst_ref in zip(cast_in, cast_out):
            dst_ref[...] = src_ref[...].astype(BF16)
        x = x_ref[...]
        xg = (x * g_ref[...]).astype(BF16)
        xg_ref[...] = xg
        r2_ref[...] = 1.0 / (jnp.mean(x * x, axis=-1, keepdims=True) + EPS)
        o_ref[...] = partial(xg)

    @pl.when(jnp.logical_and(k > 0, k < nk - 1))
    def _():
        o_ref[...] += partial(xg_ref[...])

    @pl.when(k == nk - 1)
    def _():
        o_ref[...] = x_ref[...] + r2_ref[...] * (o_ref[...] + partial(xg_ref[...]))


def _ffn(x2d, g, w1_bf, w2_bf, tm, tf, cast=()):
    m = x2d.shape[0]
    assert w1_bf.shape == (D_FF // tf, D_MODEL, tf)
    nk = D_FF // tf
    assert nk >= 2
    rows = D_MODEL // (m // tm)
    cast_in, cast_out, cast_shapes = [], [], []
    for w, layer in cast:
        cols = w.shape[-1]
        if layer is None:
            cast_in.append(pl.BlockSpec((rows, cols), lambda i, k: (i, 0)))
        else:
            cast_in.append(pl.BlockSpec((None, rows, cols), lambda i, k, layer=layer: (layer, i, 0)))
        cast_out.append(pl.BlockSpec((rows, cols), lambda i, k: (i, 0)))
        cast_shapes.append(jax.ShapeDtypeStruct((D_MODEL, cols), BF16))
    return pl.pallas_call(
        functools.partial(_ffn_kernel, nk=nk, ncast=len(cast)),
        grid=(m // tm, nk),
        in_specs=[
            pl.BlockSpec((tm, D_MODEL), lambda i, k: (i, 0)),
            _resident((1, D_MODEL), lambda i, k: (0, 0)),
            pl.BlockSpec((None, D_MODEL, tf), lambda i, k: (k, 0, 0)),
            pl.BlockSpec((tf, D_MODEL), lambda i, k: (k, 0)),
        ] + cast_in,
        out_specs=[pl.BlockSpec((tm, D_MODEL), lambda i, k: (i, 0))] + cast_out,
        out_shape=[jax.ShapeDtypeStruct(x2d.shape, F32)] + cast_shapes,
        scratch_shapes=[pltpu.VMEM((tm, D_MODEL), BF16), pltpu.VMEM((tm, 1), F32)],
        compiler_params=pltpu.CompilerParams(
            dimension_semantics=("arbitrary", "arbitrary"),
            vmem_limit_bytes=VMEM_LIMIT_BYTES),
        name="ffn",
    )(x2d, g, w1_bf, w2_bf, *[w for w, _ in cast])


def _rope_coeffs(seq):
    half = ROPE_DIM // 2
    pos = jnp.arange(seq, dtype=F32)
    inv = ROPE_THETA ** (-(jnp.arange(half, dtype=F32) * 2.0) / ROPE_DIM)
    ang = pos[:, None] * inv[None, :]
    cos, sin = jnp.cos(ang), jnp.sin(ang)
    ones = jnp.ones((seq, DIFF_HEAD_DIM - ROPE_DIM), F32)
    zeros = jnp.zeros((seq, DIFF_HEAD_DIM - ROPE_DIM), F32)
    zh = jnp.zeros((seq, half), F32)
    c0 = jnp.concatenate([cos, cos, ones], axis=-1)
    c_up = jnp.concatenate([-sin, zh, zeros], axis=-1)
    c_dn = jnp.concatenate([zh, sin, zeros], axis=-1)
    rep = LANES // DIFF_HEAD_DIM
    return tuple(jnp.tile(c, (1, rep)) for c in (c0, c_up, c_dn))


def _head_norm_rope(t, gain, bd, c0, c_up, c_dn):
    ss = _dot((t * t).astype(BF16), bd)
    tn = t * lax.rsqrt(ss * (1.0 / DIFF_HEAD_DIM) + EPS) * gain
    outs = []
    half = ROPE_DIM // 2
    for p in range(MXU_DIM // LANES):
        tp = tn[:, p * LANES:(p + 1) * LANES]
        up = pltpu.roll(tp, LANES - half, axis=1)
        dn = pltpu.roll(tp, half, axis=1)
        outs.append(tp * c0 + up * c_up + dn * c_dn)
    return jnp.concatenate(outs, axis=-1)


def _layer_b_proj_kernel(x_ref, gmix_ref, gkv_ref, win_ref, wkv_ref, bd_ref, qg_ref, kg_ref,
                         c0_ref, cup_ref, cdn_ref, mk_ref, mv_ref, mqg_ref, w1_ref, w2_ref,
                         q1t_ref, q2t_ref, k1_ref, k2_ref, vt_ref, mo_ref, w1bf_ref, w2bf_ref):
    w1bf_ref[...] = w1_ref[...].astype(BF16)
    w2bf_ref[...] = w2_ref[...].astype(BF16)
    x = x_ref[...]
    tm = x.shape[0]
    rstd = _rstd(x)
    xn_mix = (x * gmix_ref[...]).astype(BF16)
    xn_kv = (x * gkv_ref[...]).astype(BF16)
    bd = bd_ref[...]
    c0, cup, cdn = c0_ref[...], cup_ref[...], cdn_ref[...]
    qg = qg_ref[...] * (DIFF_HEAD_DIM ** -0.5 * math.log2(math.e))
    kg = kg_ref[...]
    pairs_per_chunk = MXU_DIM // LANES
    pad_rows = VT_ROWS - DIFF_V_DIM
    ones_row = (lax.broadcasted_iota(jnp.int32, (pad_rows, tm), 0) == 0).astype(BF16)
    mk, mv = mk_ref[0], mv_ref[0]

    def q_epilogue(qt_ref, c):
        def run(t):
            qt = _head_norm_rope(t, qg, bd, c0, cup, cdn).T.astype(BF16)
            for p in range(pairs_per_chunk):
                qt_ref[0, pairs_per_chunk * c + p, 0] = qt[p * LANES:(p + 1) * LANES]
        return run

    def k_epilogue(k_ref, c):
        def run(t):
            k_ref[:, c * MXU_DIM:(c + 1) * MXU_DIM] = _head_norm_rope(
                t, kg, bd, c0, cup, cdn).astype(BF16)
        return run

    def v_epilogue(c):
        def run(t):
            for p in range(pairs_per_chunk):
                h = pairs_per_chunk * c + p
                vt_ref[0, h, 0, :DIFF_V_DIM, :] = t[:, p * DIFF_V_DIM:(p + 1) * DIFF_V_DIM].T.astype(BF16)
                vt_ref[0, h, 0, DIFF_V_DIM:, :] = ones_row
        return run

    def mem_epilogue(c):
        def run(t):
            for p in range(pairs_per_chunk):
                h = pairs_per_chunk * c + p
                hs = slice(h * MEM_HEAD_DIM, (h + 1) * MEM_HEAD_DIM)
                o = _mem_attention_head(t[:, p * MEM_HEAD_DIM:(p + 1) * MEM_HEAD_DIM],
                                        mqg_ref[...], mk[:, hs], mv[:, hs])
                mo_ref[:, hs] = o.astype(BF16)
        return run

    chunks = []
    for c in range(DIFF_QK_WIDTH // MXU_DIM):
        chunks += [
            (xn_mix, win_ref, c * MXU_DIM, q_epilogue(q1t_ref, c)),
            (xn_kv, wkv_ref, c * MXU_DIM, k_epilogue(k1_ref, c)),
            (xn_kv, wkv_ref, 2 * DIFF_QK_WIDTH + 2 * c * MXU_DIM, v_epilogue(2 * c)),
            (xn_mix, win_ref, DIFF_QK_WIDTH + c * MXU_DIM, q_epilogue(q2t_ref, c)),
            (xn_kv, wkv_ref, DIFF_QK_WIDTH + c * MXU_DIM, k_epilogue(k2_ref, c)),
            (xn_kv, wkv_ref, 2 * DIFF_QK_WIDTH + (2 * c + 1) * MXU_DIM, v_epilogue(2 * c + 1)),
        ]
    for c in range(MEM_WIDTH // MXU_DIM):
        chunks.append((xn_mix, win_ref, 2 * DIFF_QK_WIDTH + c * MXU_DIM, mem_epilogue(c)))

    def project(chunk):
        xn, w_ref, col, _ = chunk
        return _dot(xn, w_ref[:, col:col + MXU_DIM])

    pending = project(chunks[0])
    for n, chunk in enumerate(chunks):
        ready = pending
        if n + 1 < len(chunks):
            pending = project(chunks[n + 1])
        chunk[3](ready * rstd)


def _layer_b_proj(x2d, gmix, gkv, win_bf, layer, wkv_bf, q_norm, k_norm, mk, mv, mqg,
                  w_ff1, w_ff2, batch, seq, tm, tq, tf):
    nt = seq // tm
    cast_in, cast_out, cast_shapes = _ffn_cast_specs(
        layer, batch * nt, lambda b, i: b * nt + i, tf)
    nq = seq // tq
    sub = tq // tm
    npairs = DIFF_QK_WIDTH // LANES
    m = x2d.shape[0]
    c0, cup, cdn = _rope_coeffs(seq)
    lane_head = jnp.arange(MXU_DIM) // DIFF_HEAD_DIM
    bd = (lane_head[:, None] == lane_head[None, :]).astype(BF16)
    qg = jnp.tile(q_norm.reshape(1, DIFF_HEAD_DIM), (1, MXU_DIM // DIFF_HEAD_DIM))
    kg = jnp.tile(k_norm.reshape(1, DIFF_HEAD_DIM), (1, MXU_DIM // DIFF_HEAD_DIM))
    const2 = lambda b, i: (0, 0)
    rows = lambda b, i: (b * nt + i, 0)
    qk_sd = jax.ShapeDtypeStruct((m, DIFF_QK_WIDTH), BF16)
    qt_sd = jax.ShapeDtypeStruct((batch, npairs, nq, LANES, tq), BF16)
    qt_spec = pl.BlockSpec((1, npairs, 1, LANES, tm), lambda b, i: (b, 0, i // sub, 0, i % sub))
    return pl.pallas_call(
        _layer_b_proj_kernel,
        grid=(batch, nt),
        in_specs=[
            pl.BlockSpec((tm, D_MODEL), rows),
            _resident((1, D_MODEL), const2),
            _resident((1, D_MODEL), const2),
            _resident((D_MODEL, D_MODEL), const2),
            _resident((D_MODEL, KV_WIDTH), const2),
            _resident((MXU_DIM, MXU_DIM), const2),
            _resident((1, MXU_DIM), const2),
            _resident((1, MXU_DIM), const2),
            pl.BlockSpec((tm, LANES), lambda b, i: (i, 0)),
            pl.BlockSpec((tm, LANES), lambda b, i: (i, 0)),
            pl.BlockSpec((tm, LANES), lambda b, i: (i, 0)),
            pl.BlockSpec((1, MEM_LEN, MEM_WIDTH), lambda b, i: (layer, b, 0)),
            pl.BlockSpec((1, MEM_LEN, MEM_WIDTH), lambda b, i: (layer, b, 0)),
            _resident((1, MEM_HEAD_DIM), const2),
        ] + cast_in,
        out_specs=[
            qt_spec,
            qt_spec,
            pl.BlockSpec((tm, DIFF_QK_WIDTH), rows),
            pl.BlockSpec((tm, DIFF_QK_WIDTH), rows),
            pl.BlockSpec((1, DIFF_HEADS, 1, VT_ROWS, tm),
                         lambda b, i: (b, 0, i // sub, 0, i % sub)),
            pl.BlockSpec((tm, MEM_WIDTH), rows),
        ] + cast_out,
        out_shape=[qt_sd, qt_sd, qk_sd, qk_sd,
                   jax.ShapeDtypeStruct((batch, DIFF_HEADS, nq, VT_ROWS, tq), BF16),
                   jax.ShapeDtypeStruct((m, MEM_WIDTH), BF16)] + cast_shapes,
        compiler_params=pltpu.CompilerParams(
            dimension_semantics=("arbitrary", "arbitrary"),
            vmem_limit_bytes=VMEM_LIMIT_BYTES),
        name="layer_b_proj",
    )(x2d, gmix, gkv, win_bf, wkv_bf, bd, qg, kg, c0, cup, cdn, mk, mv, mqg, w_ff1, w_ff2)


def _diff_attn_kernel(q1t_ref, q2t_ref, k1_ref, k2_ref, vt_ref, lam_ref, sg_ref, o_ref,
                      s_ref, m_ref, acc_ref, b_ref, *, tq, nq, pairs, lam_init):
    feat = lax.broadcasted_iota(jnp.int32, (LANES, 1), 0)
    heads_per_pair = LANES // DIFF_HEAD_DIM
    combos = [(pr, hh, mp) for pr in range(pairs) for hh in range(heads_per_pair)
              for mp in range(2)]
    in_head = [(feat >= hh * DIFF_HEAD_DIM) & (feat < (hh + 1) * DIFF_HEAD_DIM)
               for hh in range(heads_per_pair)]
    key_pos = lax.broadcasted_iota(jnp.int32, (tq, tq), 0)
    q_pos = lax.broadcasted_iota(jnp.int32, (tq, tq), 1)

    def scores(i, blk):
        r0 = pl.multiple_of(blk * tq, tq)
        out = []
        for pr, hh, mp in combos:
            qt = (q1t_ref, q2t_ref)[mp][0, pr, i]
            qm = jnp.where(in_head[hh], qt, jnp.zeros_like(qt))
            kblk = (k1_ref, k2_ref)[mp][pl.ds(r0, tq), pr * LANES:(pr + 1) * LANES]
            out.append(_dot(kblk, qm))
        return out

    def store_scores(sts, diagonal):
        for c, st in enumerate(sts):
            s_ref[c] = jnp.where(key_pos <= q_pos, st, NEG_INF) if diagonal else st

    def score_bounds():
        lane = lax.broadcasted_iota(jnp.int32, (1, LANES), 1)
        sub = lax.broadcasted_iota(jnp.int32, (2 * SUBLANES, LANES), 0)
        largest = jnp.zeros((1, tq), F32)
        for pr in range(pairs):
            for mp in range(2):
                kall = (k1_ref, k2_ref)[mp][:, pr * LANES:(pr + 1) * LANES]
                kmax = jnp.max(jnp.abs(kall.astype(F32)), axis=0, keepdims=True)
                lhs = jnp.zeros((2 * SUBLANES, LANES), F32)
                for hh in range(heads_per_pair):
                    head_lanes = (lane >= hh * DIFF_HEAD_DIM) & (lane < (hh + 1) * DIFF_HEAD_DIM)
                    lhs = jnp.where((sub == hh) & head_lanes, kmax, lhs)
                lhs = lhs.astype(BF16)
                for i in range(nq):
                    qabs = jnp.abs((q1t_ref, q2t_ref)[mp][0, pr, i])
                    bound = _dot(lhs, qabs)
                    for hh in range(heads_per_pair):
                        c = 2 * (heads_per_pair * pr + hh) + mp
                        b_ref[i, c] = bound[hh:hh + 1]
                        largest = jnp.maximum(largest, bound[hh:hh + 1])
        return jnp.max(largest)

    def run(fast):
        def consume(i, blk, first):
            for c, (pr, hh, mp) in enumerate(combos):
                st = s_ref[c]
                vtblk = vt_ref[0, heads_per_pair * pr + hh, blk]
                if fast:
                    p = jnp.exp2(st - b_ref[i, c]).astype(BF16)
                    acc_ref[c] += _dot(vtblk, p)
                else:
                    m_old = jnp.where(first, NEG_INF, m_ref[c])
                    m_new = jnp.maximum(m_old, jnp.max(st, axis=0, keepdims=True))
                    alpha = jnp.exp2(m_old - m_new)
                    p = jnp.exp2(st - m_new).astype(BF16)
                    acc_ref[c] = alpha * acc_ref[c] + _dot(vtblk, p)
                    m_ref[c] = m_new

        def finalize(i):
            lq = lam_ref[...]
            lam = (jnp.exp(jnp.sum(lq[0:1] * lq[1:2], axis=-1, keepdims=True))
                   - jnp.exp(jnp.sum(lq[2:3] * lq[3:4], axis=-1, keepdims=True)) + lam_init)
            r0 = pl.multiple_of(i * tq, tq)
            for hh in range(heads_per_pair * pairs):
                a1, a2 = acc_ref[2 * hh], acc_ref[2 * hh + 1]
                ot = (a1[:DIFF_V_DIM] * (1.0 / a1[DIFF_V_DIM:DIFF_V_DIM + 1])
                      - lam * (a2[:DIFF_V_DIM] * (1.0 / a2[DIFF_V_DIM:DIFF_V_DIM + 1])))
                ot = ot * lax.rsqrt(jnp.mean(ot * ot, axis=0, keepdims=True) + EPS)
                o = ot.T * sg_ref[...] * (1.0 - lam_init)
                o_ref[pl.ds(r0, tq), hh * DIFF_V_DIM:(hh + 1) * DIFF_V_DIM] = o.astype(BF16)
            if fast:
                acc_ref[...] = jnp.zeros(acc_ref.shape, F32)

        store_scores(scores(0, 0), True)

        def body(t, carry):
            i, pos = carry
            last = pos == i
            ni = jnp.where(last, i + 1, i)
            npos = jnp.where(last, 0, pos + 1)
            blk = jnp.where(pos == 0, i, pos - 1)
            nblk = jnp.where(npos == 0, ni, npos - 1)

            @pl.when(last)
            def _():
                nxt = scores(ni, nblk)
                consume(i, blk, pos == 0)
                store_scores(nxt, True)
                finalize(i)

            @pl.when(jnp.logical_not(last))
            def _():
                nxt = scores(ni, nblk)
                consume(i, blk, pos == 0)
                store_scores(nxt, False)

            return ni, npos

        ntasks = nq * (nq + 1) // 2
        lax.fori_loop(0, ntasks - 1, body, (jnp.int32(0), jnp.int32(0)))
        consume(nq - 1, max(nq - 2, 0), nq == 1)
        finalize(nq - 1)

    acc_ref[...] = jnp.zeros(acc_ref.shape, F32)
    m_ref[...] = jnp.full(m_ref.shape, NEG_INF, F32)
    use_bounds = score_bounds() <= MAX_SAFE_SCORE_BOUND

    @pl.when(use_bounds)
    def _():
        run(True)

    @pl.when(jnp.logical_not(use_bounds))
    def _():
        run(False)


def _diff_attention(q1t, q2t, k1, k2, vt, diff_lambda, subln, batch, seq, tq, pairs, lam_init):
    nq = seq // tq
    m = k1.shape[0]
    npairs = DIFF_QK_WIDTH // LANES
    heads_per_step = pairs * (LANES // DIFF_HEAD_DIM)
    vw = heads_per_step * DIFF_V_DIM
    ncombo = 2 * heads_per_step
    qspec = pl.BlockSpec((1, pairs, nq, LANES, tq), lambda b, hp: (b, hp, 0, 0, 0))
    kspec = pl.BlockSpec((seq, pairs * LANES), lambda b, hp: (b, hp))
    return pl.pallas_call(
        functools.partial(_diff_attn_kernel, tq=tq, nq=nq, pairs=pairs, lam_init=lam_init),
        grid=(batch, npairs // pairs),
        in_specs=[
            qspec, qspec, kspec, kspec,
            pl.BlockSpec((1, heads_per_step, nq, VT_ROWS, tq), lambda b, hp: (b, hp, 0, 0, 0)),
            _resident((4, DIFF_HEAD_DIM), lambda b, hp: (0, 0)),
            _resident((1, DIFF_V_DIM), lambda b, hp: (0, 0)),
        ],
        out_specs=pl.BlockSpec((seq, vw), lambda b, hp: (b, hp)),
        out_shape=jax.ShapeDtypeStruct((m, DIFF_V_WIDTH), BF16),
        scratch_shapes=[
            pltpu.VMEM((ncombo, tq, tq), F32),
            pltpu.VMEM((ncombo, 1, tq), F32),
            pltpu.VMEM((ncombo, VT_ROWS, tq), F32),
            pltpu.VMEM((nq, ncombo, 1, tq), F32),
        ],
        compiler_params=pltpu.CompilerParams(
            dimension_semantics=("arbitrary", "arbitrary"),
            vmem_limit_bytes=VMEM_LIMIT_BYTES),
        name="diff_attention",
    )(q1t, q2t, k1, k2, vt, diff_lambda, subln)


def _out_proj_kernel(tok_ref, mo_ref, x_ref, w_ref, o_ref):
    o_ref[...] = (x_ref[...] + _dot(tok_ref[...], w_ref[:DIFF_V_WIDTH, :])
                  + _dot(mo_ref[...], w_ref[DIFF_V_WIDTH:, :]))


def _out_proj(tok, mo, x2d, wout_bf, layer, tm):
    m = x2d.shape[0]
    rows = lambda i: (i, 0)
    return pl.pallas_call(
        _out_proj_kernel,
        grid=(m // tm,),
        in_specs=[
            pl.BlockSpec((tm, DIFF_V_WIDTH), rows),
            pl.BlockSpec((tm, MEM_WIDTH), rows),
            pl.BlockSpec((tm, D_MODEL), rows),
            _resident((None, D_MODEL, D_MODEL), lambda i: (layer, 0, 0)),
        ],
        out_specs=pl.BlockSpec((tm, D_MODEL), rows),
        out_shape=jax.ShapeDtypeStruct(x2d.shape, F32),
        compiler_params=pltpu.CompilerParams(
            dimension_semantics=("arbitrary",),
            vmem_limit_bytes=VMEM_LIMIT_BYTES),
        name="out_proj",
    )(tok, mo, x2d, wout_bf)


def kernel(x, mem, mix_norm, w_in, w_out, mem_norm, w_mem_kv, mem_q_norm, mem_k_norm,
           ffn_norm, w_ff1, w_ff2, pool_w, pool_scale, kv_norm, w_kv, k_norm, q_norm,
           diff_lambda, subln_norm):
    batch, seq, _ = x.shape
    m = batch * seq
    x2d = x.reshape(m, D_MODEL)
    mem2d = mem.reshape(batch * MEM_LEN, D_MODEL)

    mk, mv, w_in0_bf, w_out0_bf = _memkv(
        mem2d, mem_norm, w_mem_kv.astype(BF16), mem_k_norm, w_in, w_out, batch)

    tf = 1024

    x2d, w1_bf, w2_bf = _layer_a(x2d, mix_norm[0:1], w_in0_bf[None], w_out0_bf[None], 0,
                                 pool_w[0].astype(BF16), pool_scale[0:1], mk, mv,
                                 mem_q_norm[0:1], w_ff1, w_ff2, batch, seq, tm=512, tf=tf)
    x2d, w_in1_bf, w_out1_bf, w_kv_bf = _ffn(
        x2d, ffn_norm[0:1], w1_bf, w2_bf, tm=512, tf=tf,
        cast=((w_in, 1), (w_out, 1), (w_kv, None)))

    lam_init = 0.8 - 0.6 * math.exp(-0.3 * 1)
    tq = 256
    q1t, q2t, k1, k2, vt, mo, w1_bf, w2_bf = _layer_b_proj(
        x2d, mix_norm[1:2], kv_norm.reshape(1, D_MODEL), w_in1_bf, 1, w_kv_bf,
        q_norm[0], k_norm, mk, mv, mem_q_norm[1:2], w_ff1, w_ff2, batch, seq,
        tm=256, tq=tq, tf=tf)
    tok = _diff_attention(q1t, q2t, k1, k2, vt, diff_lambda[0], subln_norm[0:1],
                          batch, seq, tq=tq, pairs=2, lam_init=lam_init)
    x2d = _out_proj(tok, mo, x2d, w_out1_bf[None], 0, tm=1024)
    (x2d,) = _ffn(x2d, ffn_norm[1:2], w1_bf, w2_bf, tm=512, tf=tf)
    return x2d.reshape(batch, seq, D_MODEL)
```
